```python
import math
import numpy as np
import jax
import jax.numpy as jnp
from jax import lax

D_MODEL = 1024
BATCH = 8
SEQ = 4096
DEPTH = 2

N_MIXERS = 2
HEAD_DIM = 64
MIX_WIDTH = D_MODEL
MEM_HEADS = 4
MEM_WIDTH = MEM_HEADS * HEAD_DIM
TOK_WIDTH = MIX_WIDTH - MEM_WIDTH
MEM_LEN = 256
S5_GROUP_CH = 16
S5_GROUPS = TOK_WIDTH // S5_GROUP_CH
S5_STATE = 64
DT_MIN = 0.001
DT_MAX = 0.1
MOBA_HEADS = TOK_WIDTH // HEAD_DIM
MOBA_BLOCK = 256
MOBA_TOPK = 3
Q_CHUNK = 32
FFN_HIDDEN = -(-(8 * D_MODEL) // (3 * 256)) * 256
N_S5_LAYERS = (DEPTH + 1) // 2
N_MOBA_LAYERS = DEPTH // 2
RMS_EPS = 1e-6
NEG_INF = -1e30

kernel_name = "hybrid_s5_moba_memory_trunk"


def _rmsnorm(x, g):
    xf = x.astype(jnp.float32)
    y = xf * lax.rsqrt(jnp.mean(xf * xf, axis=-1, keepdims=True) + RMS_EPS) * g.astype(jnp.float32)
    return y.astype(x.dtype)


def _alibi_slopes(n):
    return jnp.asarray(2.0 ** (-8.0 * (np.arange(n) + 1) / n), dtype=jnp.float32)


def _swiglu(h, w_gate, w_up, w_down):
    return (jax.nn.silu(h @ w_gate) * (h @ w_up)) @ w_down


def _memory_attention(q, mem_k, mem_v):
    bsz, s, _ = q.shape
    qh = q.astype(jnp.float32).reshape(bsz, s, MEM_HEADS, HEAD_DIM) * (HEAD_DIM ** -0.5)
    scores = jnp.einsum('bshd,bmhd->bhsm', qh, mem_k.astype(jnp.float32))
    p = jax.nn.softmax(scores, axis=-1)
    out = jnp.einsum('bhsm,bmhd->bshd', p, mem_v.astype(jnp.float32))
    return out.reshape(bsz, s, MEM_WIDTH).astype(q.dtype)


def _s5_mixer(u, lam_re, lam_im, log_dt, b_re, b_im, c_re, c_im, d, w_glu, b_glu):
    bsz, s, _ = u.shape
    f32 = jnp.float32
    uf = u.astype(f32)
    ug = uf.reshape(bsz, s, S5_GROUPS, S5_GROUP_CH)
    lr = lam_re.astype(f32)
    li = lam_im.astype(f32)
    dt = jnp.exp(log_dt.astype(f32))[:, None]
    mag = jnp.exp(lr * dt)
    ar = mag * jnp.cos(li * dt)
    ai = mag * jnp.sin(li * dt)
    den = lr * lr + li * li
    nr = ar - 1.0
    fr = (nr * lr + ai * li) / den
    fi = (ai * lr - nr * li) / den
    br = b_re.astype(f32)
    bi = b_im.astype(f32)
    bbr = fr[..., None] * br - fi[..., None] * bi
    bbi = fr[..., None] * bi + fi[..., None] * br
    bu_r = jnp.einsum('bsgc,gpc->bsgp', ug, bbr)
    bu_i = jnp.einsum('bsgc,gpc->bsgp', ug, bbi)
    a_r = jnp.broadcast_to(ar[None, None], (1, s, S5_GROUPS, S5_STATE))
    a_i = jnp.broadcast_to(ai[None, None], (1, s, S5_GROUPS, S5_STATE))

    def combine(e1, e2):
        a1r, a1i, b1r, b1i = e1
        a2r, a2i, b2r, b2i = e2
        return (a2r * a1r - a2i * a1i,
                a2r * a1i + a2i * a1r,
                a2r * b1r - a2i * b1i + b2r,
                a2r * b1i + a2i * b1r + b2i)

    _, _, xr, xi = lax.associative_scan(combine, (a_r, a_i, bu_r, bu_i), axis=1)
    y = (jnp.einsum('bsgp,gcp->bsgc', xr, c_re.astype(f32))
         - jnp.einsum('bsgp,gcp->bsgc', xi, c_im.astype(f32)))
    y = y.reshape(bsz, s, TOK_WIDTH) + d.astype(f32) * uf
    y = jax.nn.gelu(y, approximate=False)
    y = y * jax.nn.sigmoid(y @ w_glu.astype(f32) + b_glu.astype(f32))
    return y.astype(u.dtype)


def _moba_attention(q, k, v, slopes):
    bsz, nh, s, dh = q.shape
    nb = -(-s // MOBA_BLOCK)
    pad = nb * MOBA_BLOCK - s
    k_blk = jnp.pad(k, ((0, 0), (0, 0), (0, pad), (0, 0))).reshape(bsz, nh, nb, MOBA_BLOCK, dh)
    v_blk = jnp.pad(v, ((0, 0), (0, 0), (0, pad), (0, 0))).reshape(bsz, nh, nb, MOBA_BLOCK, dh)
    k_mean = jnp.mean(k_blk, axis=3)
    top = min(MOBA_TOPK, nb)
    n_chunks = s // Q_CHUNK
    q_chunks = (q * (dh ** -0.5)).reshape(bsz, nh, n_chunks, Q_CHUNK, dh).transpose(2, 0, 1, 3, 4)
    offs = jnp.arange(MOBA_BLOCK, dtype=jnp.int32)
    blk_ids = jnp.arange(nb, dtype=jnp.int32)
    bidx = jnp.arange(bsz)[:, None, None, None]
    hidx = jnp.arange(nh)[None, :, None, None]
    slope5 = slopes[None, :, None, None, None]

    def one_chunk(args):
        ci, qc = args
        start = ci * Q_CHUNK
        t = start + jnp.arange(Q_CHUNK, dtype=jnp.int32)
        qb = start // MOBA_BLOCK
        gate = jnp.einsum('bhqd,bhnd->bhqn', qc, k_mean)
        gate = jnp.where(blk_ids < qb, gate, NEG_INF)
        _, sel = lax.top_k(gate, top)
        valid = jnp.arange(top) < qb
        k_sel = k_blk[bidx, hidx, sel]
        v_sel = v_blk[bidx, hidx, sel]
        s_sel = jnp.einsum('bhqd,bhqnld->bhqnl', qc, k_sel)
        pos_sel = sel[..., None] * MOBA_BLOCK + offs
        dist_sel = (t[None, None, :, None, None] - pos_sel).astype(jnp.float32)
        s_sel = jnp.where(valid[None, None, None, :, None], s_sel - slope5 * dist_sel, NEG_INF)
        k_own = lax.dynamic_index_in_dim(k_blk, qb, axis=2, keepdims=False)
        v_own = lax.dynamic_index_in_dim(v_blk, qb, axis=2, keepdims=False)
        s_own = jnp.einsum('bhqd,bhld->bhql', qc, k_own)
        dist_own = t[:, None] - (qb * MOBA_BLOCK + offs)[None, :]
        s_own = jnp.where(dist_own[None, None] >= 0,
                          s_own - slopes[:, None, None] * dist_own.astype(jnp.float32)[None, None],
                          NEG_INF)
        n_sel = top * MOBA_BLOCK
        s_all = jnp.concatenate([s_sel.reshape(bsz, nh, Q_CHUNK, n_sel), s_own], axis=-1)
        p = jax.nn.softmax(s_all, axis=-1)
        p_sel = p[..., :n_sel].reshape(bsz, nh, Q_CHUNK, top, MOBA_BLOCK)
        p_own = p[..., n_sel:]
        return (jnp.einsum('bhqnl,bhqnld->bhqd', p_sel, v_sel)
                + jnp.einsum('bhql,bhld->bhqd', p_own, v_own))

    out = lax.map(one_chunk, (jnp.arange(n_chunks, dtype=jnp.int32), q_chunks))
    return out.transpose(1, 2, 0, 3, 4).reshape(bsz, nh, s, dh)


def _moba_mixer(qkv, slopes):
    bsz, s, _ = qkv.shape
    t = qkv.astype(jnp.float32).reshape(bsz, s, 3, MOBA_HEADS, HEAD_DIM).transpose(2, 0, 3, 1, 4)
    out = _moba_attention(t[0], t[1], t[2], slopes)
    return out.transpose(0, 2, 1, 3).reshape(bsz, s, TOK_WIDTH).astype(qkv.dtype)


def setup_inputs(seed: int = 0) -> dict:
    key = jax.random.key(seed)
    ks = jax.random.split(key, 24)
    f32 = jnp.float32

    def nrm(k, shape, scale):
        return jax.random.normal(k, shape, f32) * scale

    n_idx = jnp.arange(S5_STATE, dtype=f32)
    return {
        "x": nrm(ks[0], (BATCH, SEQ, D_MODEL), 1.0),
        "mem": nrm(ks[1], (BATCH, MEM_LEN, D_MODEL), 1.0),
        "mem_norm_g": 1.0 + nrm(ks[2], (D_MODEL,), 0.01),
        "w_mem_kv": nrm(ks[3], (D_MODEL, 2 * MEM_WIDTH), D_MODEL ** -0.5),
        "mix_norm_g": 1.0 + nrm(ks[4], (DEPTH, D_MODEL), 0.01),
        "s5_w_in": nrm(ks[5], (N_S5_LAYERS, D_MODEL, TOK_WIDTH + MEM_WIDTH), D_MODEL ** -0.5),
        "s5_lambda_re": -0.5 + nrm(ks[6], (N_S5_LAYERS, S5_GROUPS, S5_STATE), 0.01),
        "s5_lambda_im": math.pi * n_idx + nrm(ks[7], (N_S5_LAYERS, S5_GROUPS, S5_STATE), 0.01),
        "s5_log_dt": jax.random.uniform(ks[8], (N_S5_LAYERS, S5_GROUPS), f32,
                                        minval=math.log(DT_MIN), maxval=math.log(DT_MAX)),
        "s5_b_re": nrm(ks[9], (N_S5_LAYERS, S5_GROUPS, S5_STATE, S5_GROUP_CH), (2 * S5_GROUP_CH) ** -0.5),
        "s5_b_im": nrm(ks[10], (N_S5_LAYERS, S5_GROUPS, S5_STATE, S5_GROUP_CH), (2 * S5_GROUP_CH) ** -0.5),
        "s5_c_re": nrm(ks[11], (N_S5_LAYERS, S5_GROUPS, S5_GROUP_CH, S5_STATE), (2 * S5_STATE) ** -0.5),
        "s5_c_im": nrm(ks[12], (N_S5_LAYERS, S5_GROUPS, S5_GROUP_CH, S5_STATE), (2 * S5_STATE) ** -0.5),
        "s5_d": nrm(ks[13], (N_S5_LAYERS, TOK_WIDTH), 1.0),
        "s5_w_glu": nrm(ks[14], (N_S5_LAYERS, TOK_WIDTH, TOK_WIDTH), TOK_WIDTH ** -0.5),
        "s5_b_glu": nrm(ks[15], (N_S5_LAYERS, TOK_WIDTH), 0.01),
        "moba_w_in": nrm(ks[16], (N_MOBA_LAYERS, D_MODEL, 3 * TOK_WIDTH + MEM_WIDTH), D_MODEL ** -0.5),
        "w_out": nrm(ks[17], (DEPTH, MIX_WIDTH, D_MODEL), MIX_WIDTH ** -0.5),
        "ffn_norm_g": 1.0 + nrm(ks[18], (DEPTH, D_MODEL), 0.01),
        "w_gate": nrm(ks[19], (DEPTH, D_MODEL, FFN_HIDDEN), D_MODEL ** -0.5),
        "w_up": nrm(ks[20], (DEPTH, D_MODEL, FFN_HIDDEN), D_MODEL ** -0.5),
        "w_down": nrm(ks[21], (DEPTH, FFN_HIDDEN, D_MODEL), FFN_HIDDEN ** -0.5),
        "final_norm_g": 1.0 + nrm(ks[22], (D_MODEL,), 0.01),
    }


def reference(x, mem, mem_norm_g, w_mem_kv, mix_norm_g, s5_w_in, s5_lambda_re, s5_lambda_im,
              s5_log_dt, s5_b_re, s5_b_im, s5_c_re, s5_c_im, s5_d, s5_w_glu, s5_b_glu,
              moba_w_in, w_out, ffn_norm_g, w_gate, w_up, w_down, final_norm_g):
    bsz = x.shape[0]
    mem_kv = _rmsnorm(mem, mem_norm_g) @ w_mem_kv
    mem_k = mem_kv[..., :MEM_WIDTH].reshape(bsz, -1, MEM_HEADS, HEAD_DIM)
    mem_v = mem_kv[..., MEM_WIDTH:].reshape(bsz, -1, MEM_HEADS, HEAD_DIM)
    slopes = _alibi_slopes(MOBA_HEADS)

    for i in range(DEPTH):
        j = i // N_MIXERS
        h = _rmsnorm(x, mix_norm_g[i])
        if i % N_MIXERS == 0:
            proj = h @ s5_w_in[j]
            tok = _s5_mixer(proj[..., :TOK_WIDTH], s5_lambda_re[j], s5_lambda_im[j], s5_log_dt[j],
                            s5_b_re[j], s5_b_im[j], s5_c_re[j], s5_c_im[j], s5_d[j],
                            s5_w_glu[j], s5_b_glu[j])
        else:
            proj = h @ moba_w_in[j]
            tok = _moba_mixer(proj[..., :3 * TOK_WIDTH], slopes)
        mem_out = _memory_attention(proj[..., -MEM_WIDTH:], mem_k, mem_v)
        x = x + jnp.concatenate([tok, mem_out], axis=-1) @ w_out[i]
        h = _rmsnorm(x, ffn_norm_g[i])
        x = x + _swiglu(h, w_gate[i], w_up[i], w_down[i])
    return _rmsnorm(x, final_norm_g)
```

```python
import functools
import math

import numpy as np
import jax
import jax.numpy as jnp
from jax import lax
from jax.experimental import pallas as pl
from jax.experimental.pallas import tpu as pltpu

F32 = jnp.float32
BF16 = jnp.bfloat16

HEAD_DIM = 64
MEM_HEADS = 4
MEM_WIDTH = MEM_HEADS * HEAD_DIM
S5_GROUP_CH = 16
S5_STATE = 64
S5_CHUNK = 16
MOBA_BLOCK = 256
MOBA_TOPK = 3
RMS_EPS = 1e-6
NEG_INF = -1e30

LANES = 128
VMEM_LIMIT_BYTES = 56 * 1024 * 1024
ROW_TILE = 512


def _params(n_axes):
    return pltpu.CompilerParams(dimension_semantics=("arbitrary",) * n_axes,
                                vmem_limit_bytes=VMEM_LIMIT_BYTES)


def _rms_scale(x, g):
    ms = jnp.mean(x * x, axis=-1, keepdims=True)
    return x * lax.rsqrt(ms + RMS_EPS) * g


def _norm_matmul_kernel(x_ref, g_ref, w_ref, o_ref):
    h = _rms_scale(x_ref[...], g_ref[...])
    o_ref[...] = jnp.dot(h.astype(BF16), w_ref[...], preferred_element_type=F32)


def _norm_matmul(x2d, g, w_bf16):
    n, d = x2d.shape
    f = w_bf16.shape[1]
    tm = min(ROW_TILE, n)
    return pl.pallas_call(
        _norm_matmul_kernel,
        grid=(n // tm,),
        in_specs=[pl.BlockSpec((tm, d), lambda i: (i, 0)),
                  pl.BlockSpec((1, d), lambda i: (0, 0)),
                  pl.BlockSpec((d, f), lambda i: (0, 0))],
        out_specs=pl.BlockSpec((tm, f), lambda i: (i, 0)),
        out_shape=jax.ShapeDtypeStruct((n, f), F32),
        compiler_params=_params(1),
        name="norm_matmul",
    )(x2d, g.reshape(1, d), w_bf16)


def _s5_operators(lam_re, lam_im, log_dt, b_re, b_im, c_re, c_im):
    hi = lax.Precision.HIGHEST
    t_len = S5_CHUNK
    g_cnt, p_cnt = lam_re.shape
    c_cnt = b_re.shape[-1]
    lr = lam_re.astype(F32)
    li = lam_im.astype(F32)
    dt = jnp.exp(log_dt.astype(F32))[:, None]
    mag = jnp.exp(lr * dt)
    ar = mag * jnp.cos(li * dt)
    ai = mag * jnp.sin(li * dt)
    den = lr * lr + li * li
    nr = ar - 1.0
    fr = (nr * lr + ai * li) / den
    fi = (ai * lr - nr * li) / den
    br = b_re.astype(F32)
    bi = b_im.astype(F32)
    bbr = fr[..., None] * br - fi[..., None] * bi
    bbi = fr[..., None] * bi + fi[..., None] * br
    cr = c_re.astype(F32)
    ci = c_im.astype(F32)

    k = jnp.arange(t_len + 1, dtype=F32)[:, None, None]
    pm = jnp.exp(k * (lr * dt)[None])
    pr = pm * jnp.cos(k * (li * dt)[None])
    pi = pm * jnp.sin(k * (li * dt)[None])

    wr = pr[..., None] * bbr[None] - pi[..., None] * bbi[None]
    wi = pr[..., None] * bbi[None] + pi[..., None] * bbr[None]
    kern = (jnp.einsum('gcp,kgpd->kgcd', cr, wr[:t_len], precision=hi)
            - jnp.einsum('gcp,kgpd->kgcd', ci, wi[:t_len], precision=hi))
    lag = jnp.arange(t_len)[None, :] - jnp.arange(t_len)[:, None]
    mt = jnp.where((lag >= 0)[:, :, None, None, None],
                   kern[jnp.clip(lag, 0, t_len - 1)], 0.0)
    mt = mt.transpose(2, 0, 4, 1, 3).reshape(g_cnt, t_len * c_cnt, t_len * c_cnt)

    rev = wr[:t_len][::-1], wi[:t_len][::-1]
    pt_re = rev[0].transpose(1, 0, 3, 2).reshape(g_cnt, t_len * c_cnt, p_cnt)
    pt_im = rev[1].transpose(1, 0, 3, 2).reshape(g_cnt, t_len * c_cnt, p_cnt)

    qr = (cr[None] * pr[1:, :, None, :] - ci[None] * pi[1:, :, None, :])
    qi = -(cr[None] * pi[1:, :, None, :] + ci[None] * pr[1:, :, None, :])
    qt_re = qr.transpose(1, 3, 0, 2).reshape(g_cnt, p_cnt, t_len * c_cnt)
    qt_im = qi.transpose(1, 3, 0, 2).reshape(g_cnt, p_cnt, t_len * c_cnt)

    n_pair = g_cnt // 2
    zp = jnp.zeros_like(pt_re)
    zq = jnp.zeros_like(qt_re)
    p_even = jnp.concatenate([pt_re, zp, pt_im, zp], axis=-1)
    p_odd = jnp.concatenate([zp, pt_re, zp, pt_im], axis=-1)
    q_even = jnp.concatenate([qt_re, zq, qt_im, zq], axis=1)
    q_odd = jnp.concatenate([zq, qt_re, zq, qt_im], axis=1)
    sel = (jnp.arange(g_cnt) % 2 == 0)[:, None, None]
    pt = jnp.where(sel, p_even, p_odd).reshape(n_pair, 2, t_len * c_cnt, 4 * p_cnt)
    qt = jnp.where(sel, q_even, q_odd).reshape(n_pair, 2, 4 * p_cnt, t_len * c_cnt)
    mt = mt.reshape(n_pair, 2, t_len * c_cnt, t_len * c_cnt)
    coef = jnp.stack([pr[t_len].reshape(n_pair, 2 * p_cnt),
                      pi[t_len].reshape(n_pair, 2 * p_cnt)], axis=1)
    return mt.astype(BF16), pt.astype(BF16), qt.astype(BF16), coef


def _s5_ssm_kernel(x_ref, mt_ref, pt_ref, qt_ref, coef_ref, y_ref, s_scr, xp_scr, *, n_batch, n_chunks):
    half = 2 * S5_STATE
    x0 = x_ref[0, 0].astype(BF16)
    x1 = x_ref[0, 1].astype(BF16)
    s_scr[...] = (jnp.dot(x0, pt_ref[0, 0], preferred_element_type=F32)
                  + jnp.dot(x1, pt_ref[0, 1], preferred_element_type=F32))
    cr = coef_ref[0, 0:1, :]
    ci = coef_ref[0, 1:2, :]

    def step(n, carry):
        xr, xi = carry
        r0 = pl.multiple_of(n * n_batch, n_batch)
        xp_scr[pl.ds(r0, n_batch), 0:half] = xr
        xp_scr[pl.ds(r0, n_batch), half:2 * half] = xi
        sr = s_scr[pl.ds(r0, n_batch), 0:half]
        si = s_scr[pl.ds(r0, n_batch), half:2 * half]
        return cr * xr - ci * xi + sr, cr * xi + ci * xr + si

    zero = jnp.zeros((n_batch, half), F32)
    lax.fori_loop(0, n_chunks, step, (zero, zero))
    xp = xp_scr[...].astype(BF16)
    y_ref[0, 0] = (jnp.dot(x0, mt_ref[0, 0], preferred_element_type=F32)
                   + jnp.dot(xp, qt_ref[0, 0], preferred_element_type=F32))
    y_ref[0, 1] = (jnp.dot(x1, mt_ref[0, 1], preferred_element_type=F32)
                   + jnp.dot(xp, qt_ref[0, 1], preferred_element_type=F32))


def _s5_ssm(u, ops):
    mt, pt, qt, coef = ops
    bsz, s, w = u.shape
    n_pair = mt.shape[0]
    t_len, c_cnt = S5_CHUNK, S5_GROUP_CH
    n_chunks = s // t_len
    rows = n_chunks * bsz
    width = t_len * c_cnt
    x = u.reshape(bsz, n_chunks, t_len, n_pair, 2, c_cnt).transpose(3, 4, 1, 0, 2, 5)
    x = x.reshape(n_pair, 2, rows, width)
    y = pl.pallas_call(
        functools.partial(_s5_ssm_kernel, n_batch=bsz, n_chunks=n_chunks),
        grid=(n_pair,),
        in_specs=[pl.BlockSpec((1, 2, rows, width), lambda i: (i, 0, 0, 0)),
                  pl.BlockSpec((1, 2, width, width), lambda i: (i, 0, 0, 0)),
                  pl.BlockSpec((1, 2, width, 4 * S5_STATE), lambda i: (i, 0, 0, 0)),
                  pl.BlockSpec((1, 2, 4 * S5_STATE, width), lambda i: (i, 0, 0, 0)),
                  pl.BlockSpec((1, 2, 2 * S5_STATE), lambda i: (i, 0, 0))],
        out_specs=pl.BlockSpec((1, 2, rows, width), lambda i: (i, 0, 0, 0)),
        out_shape=jax.ShapeDtypeStruct((n_pair, 2, rows, width), F32),
        scratch_shapes=[pltpu.VMEM((rows, 4 * S5_STATE), F32),
                        pltpu.VMEM((rows, 4 * S5_STATE), F32)],
        compiler_params=_params(1),
        name="s5_ssm",
    )(x, mt, pt, qt, coef)
    y = y.reshape(n_pair, 2, n_chunks, bsz, t_len, c_cnt).transpose(3, 2, 4, 0, 1, 5)
    return y.reshape(bsz, s, w)


def _moba_kernel(slope_ref, q_ref, k_ref, v_ref, o_ref,
                 ka_ref, kb_ref, vt_ref, km_ref, rel_ref, sel_ref, acc_ref, *, n_blocks):
    blk = MOBA_BLOCK
    hp = pl.program_id(1)
    qb = pl.program_id(2)
    slope_a = slope_ref[2 * hp]
    slope_b = slope_ref[2 * hp + 1]

    @pl.when(qb == 0)
    def _prepare():
        lane = lax.broadcasted_iota(jnp.int32, (blk, LANES), 1)
        for j in range(n_blocks):
            kj = k_ref[j * blk:(j + 1) * blk, :]
            ka_ref[j] = jnp.where(lane < HEAD_DIM, kj, 0.0).astype(BF16)
            kb_ref[j] = jnp.where(lane >= HEAD_DIM, kj, 0.0).astype(BF16)
            vt_ref[j] = v_ref[j * blk:(j + 1) * blk, :].T.astype(BF16)
            km_ref[j:j + 1, :] = jnp.mean(kj, axis=0, keepdims=True)
        d = (lax.broadcasted_iota(jnp.int32, (blk, blk), 1)
             - lax.broadcasted_iota(jnp.int32, (blk, blk), 0))
        df = d.astype(F32)
        rel_ref[0] = slope_a * df
        rel_ref[1] = jnp.where(d >= 0, slope_a * df, -NEG_INF)
        rel_ref[2] = slope_b * df
        rel_ref[3] = jnp.where(d >= 0, slope_b * df, -NEG_INF)

    qs = (q_ref[...] * (HEAD_DIM ** -0.5)).astype(BF16)
    nt = (((1,), (1,)), ((), ()))

    km = km_ref[...]
    lane_m = lax.broadcasted_iota(jnp.int32, km.shape, 1)
    blk_id = lax.broadcasted_iota(jnp.int32, (n_blocks, blk), 0)
    for e in range(2):
        in_head = (lane_m < HEAD_DIM) if e == 0 else (lane_m >= HEAD_DIM)
        km_e = jnp.where(in_head, km, 0.0).astype(BF16)
        gate = lax.dot_general(km_e, qs, nt, preferred_element_type=F32)
        gate = jnp.where(blk_id < qb, gate, NEG_INF)
        rank = jnp.zeros((n_blocks, blk), jnp.int32)
        for jp in range(n_blocks):
            gj = gate[jp:jp + 1, :]
            beats = (gj > gate) | ((gj == gate) & (jp < blk_id))
            rank = rank + beats.astype(jnp.int32)
        chosen = (blk_id < qb) & (rank < MOBA_TOPK)
        sel_ref[e] = jnp.where(chosen, 0.0, NEG_INF)

    def visit(j, rel_idx, use_sel, stats):
        new_stats = []
        for e in range(2):
            m_old, l_old = stats[e]
            k_ref_e = ka_ref if e == 0 else kb_ref
            slope = slope_a if e == 0 else slope_b
            s_t = lax.dot_general(k_ref_e[j], qs, nt, preferred_element_type=F32)
            s_t = s_t - rel_ref[2 * e + rel_idx]
            if use_sel:
                s_t = s_t + sel_ref[e, pl.ds(j, 1), :]
            shift = slope * ((qb - j) * blk).astype(F32)
            m_new = jnp.maximum(m_old, jnp.max(s_t, axis=0, keepdims=True) - shift)
            p = jnp.exp(s_t - (m_new + shift))
            alpha = jnp.exp(m_old - m_new)
            l_new = alpha * l_old + jnp.sum(p, axis=0, keepdims=True)
            pv = jnp.dot(vt_ref[j, e * HEAD_DIM:(e + 1) * HEAD_DIM, :], p.astype(BF16),
                         preferred_element_type=F32)
            acc_ref[e] = alpha * acc_ref[e] + pv
            new_stats.append((m_new, l_new))
        return tuple(new_stats)

    acc_ref[...] = jnp.zeros_like(acc_ref)
    init = (jnp.full((1, blk), NEG_INF, F32), jnp.zeros((1, blk), F32))
    stats = visit(qb, 1, False, (init, init))
    stats = lax.fori_loop(0, qb, lambda j, st: visit(j, 0, True, st), stats)
    out_t = jnp.concatenate([acc_ref[0] / stats[0][1], acc_ref[1] / stats[1][1]], axis=0)
    o_ref[...] = out_t.T


def _moba(qkv, slopes, bsz, s, n_heads):
    n_blocks = s // MOBA_BLOCK
    n_pairs = n_heads // 2
    blk = MOBA_BLOCK
    return pl.pallas_call(
        functools.partial(_moba_kernel, n_blocks=n_blocks),
        grid=(bsz, n_pairs, n_blocks),
        in_specs=[pl.BlockSpec(memory_space=pltpu.SMEM),
                  pl.BlockSpec((blk, LANES), lambda b, h, i: (b * n_blocks + i, h)),
                  pl.BlockSpec((s, LANES), lambda b, h, i: (b, n_pairs + h)),
                  pl.BlockSpec((s, LANES), lambda b, h, i: (b, 2 * n_pairs + h))],
        out_specs=pl.BlockSpec((blk, LANES), lambda b, h, i: (b * n_blocks + i, h)),
        out_shape=jax.ShapeDtypeStruct((bsz * s, n_heads * HEAD_DIM), F32),
        scratch_shapes=[pltpu.VMEM((n_blocks, blk, LANES), BF16),
                        pltpu.VMEM((n_blocks, blk, LANES), BF16),
                        pltpu.VMEM((n_blocks, LANES, blk), BF16),
                        pltpu.VMEM((n_blocks, LANES), F32),
                        pltpu.VMEM((4, blk, blk), F32),
                        pltpu.VMEM((2, n_blocks, blk), F32),
                        pltpu.VMEM((2, HEAD_DIM, blk), F32)],
        compiler_params=_params(3),
        name="moba",
    )(slopes, qkv, qkv, qkv)


def _memory_attention(q, mem_k, mem_v):
    q = q * (HEAD_DIM ** -0.5)
    lane = lax.broadcasted_iota(jnp.int32, q.shape, 1)
    out = jnp.zeros(q.shape, F32)
    for h in range(MEM_HEADS):
        in_head = (lane >= h * HEAD_DIM) & (lane < (h + 1) * HEAD_DIM)
        qh = jnp.where(in_head, q, 0.0).astype(BF16)
        s = lax.dot_general(qh, mem_k, (((1,), (1,)), ((), ())), preferred_element_type=F32)
        m = jnp.max(s, axis=-1, keepdims=True)
        p = jnp.exp(s - m)
        l = jnp.sum(p, axis=-1, keepdims=True)
        pv = jnp.dot(p.astype(BF16), mem_v, preferred_element_type=F32)
        out = jnp.where(in_head, pv / l, out)
    return out


def _project_out(tok, mem_out, wo_ref, x):
    tw = tok.shape[1]
    return (x + jnp.dot(tok.astype(BF16), wo_ref[0:tw, :], preferred_element_type=F32)
            + jnp.dot(mem_out.astype(BF16), wo_ref[tw:, :], preferred_element_type=F32))


def _s5_mix_out_kernel(y_ref, u_ref, d_ref, wglu_ref, bglu_ref, mq_ref, mk_ref, mv_ref, wo_ref, x_ref, o_ref):
    y = y_ref[...] + d_ref[...] * u_ref[...]
    y = 0.5 * y * (1.0 + lax.erf(y * math.sqrt(0.5)))
    z = jnp.dot(y.astype(BF16), wglu_ref[...], preferred_element_type=F32) + bglu_ref[...]
    tok = y * jax.nn.sigmoid(z)
    mem_out = _memory_attention(mq_ref[...], mk_ref[...], mv_ref[...])
    o_ref[...] = _project_out(tok, mem_out, wo_ref, x_ref[...])


def _moba_mix_out_kernel(tok_ref, mq_ref, mk_ref, mv_ref, wo_ref, x_ref, o_ref):
    mem_out = _memory_attention(mq_ref[...], mk_ref[...], mv_ref[...])
    o_ref[...] = _project_out(tok_ref[...], mem_out, wo_ref, x_ref[...])


def _mix_out_specs(n, s, d, tok_w, mem_len, mq_block_col):
    tm = min(ROW_TILE, s)
    per_b = s // tm
    row = lambda i: (i, 0)
    const = lambda i: (0, 0)
    mem_specs = [pl.BlockSpec((tm, MEM_WIDTH), lambda i: (i, mq_block_col)),
                 pl.BlockSpec((mem_len, MEM_WIDTH), lambda i: (i // per_b, 0)),
                 pl.BlockSpec((mem_len, MEM_WIDTH), lambda i: (i // per_b, 1)),
                 pl.BlockSpec((d, d), const),
                 pl.BlockSpec((tm, d), row)]
    return tm, row, const, mem_specs


def _s5_mix_out(y_ssm, proj, d_skip, w_glu, b_glu, mem_kv, w_out, x2d, s, mem_len):
    n, d = x2d.shape
    tok_w = y_ssm.shape[1]
    tm, row, const, mem_specs = _mix_out_specs(n, s, d, tok_w, mem_len, tok_w // MEM_WIDTH)
    return pl.pallas_call(
        _s5_mix_out_kernel,
        grid=(n // tm,),
        in_specs=[pl.BlockSpec((tm, tok_w), row),
                  pl.BlockSpec((tm, tok_w), row),
                  pl.BlockSpec((1, tok_w), const),
                  pl.BlockSpec((tok_w, tok_w), const),
                  pl.BlockSpec((1, tok_w), const)] + mem_specs,
        out_specs=pl.BlockSpec((tm, d), row),
        out_shape=jax.ShapeDtypeStruct((n, d), F32),
        compiler_params=_params(1),
        name="s5_mix_out",
    )(y_ssm, proj, d_skip.reshape(1, tok_w), w_glu, b_glu.reshape(1, tok_w), proj, mem_kv, mem_kv, w_out, x2d)


def _moba_mix_out(tok, proj, mem_kv, w_out, x2d, s, mem_len):
    n, d = x2d.shape
    tok_w = tok.shape[1]
    tm, row, const, mem_specs = _mix_out_specs(n, s, d, tok_w, mem_len, (proj.shape[1] - MEM_WIDTH) // MEM_WIDTH)
    return pl.pallas_call(
        _moba_mix_out_kernel,
        grid=(n // tm,),
        in_specs=[pl.BlockSpec((tm, tok_w), row)] + mem_specs,
        out_specs=pl.BlockSpec((tm, d), row),
        out_shape=jax.ShapeDtypeStruct((n, d), F32),
        compiler_params=_params(1),
        name="moba_mix_out",
    )(tok, proj, mem_kv, mem_kv, w_out, x2d)


FFN_CHUNK = 256


def _ffn_kernel(x_ref, g_ref, wg_ref, wu_ref, wd_ref, gf_ref, o_ref, act_ref, *, final_norm):
    x = x_ref[...]
    h = _rms_scale(x, g_ref[...]).astype(BF16)
    hidden = wg_ref.shape[1]
    for c in range(hidden // FFN_CHUNK):
        cols = slice(c * FFN_CHUNK, (c + 1) * FFN_CHUNK)
        gate = jnp.dot(h, wg_ref[:, cols], preferred_element_type=F32)
        up = jnp.dot(h, wu_ref[:, cols], preferred_element_type=F32)
        act_ref[:, cols] = (gate * jax.nn.sigmoid(gate) * up).astype(BF16)
    y = x + jnp.dot(act_ref[...], wd_ref[...], preferred_element_type=F32)
    if final_norm:
        y = _rms_scale(y, gf_ref[...])
    o_ref[...] = y


def _ffn(x2d, g, w_gate, w_up, w_down, g_final, final_norm):
    n, d = x2d.shape
    hidden = w_gate.shape[1]
    tm = min(ROW_TILE, n)
    row = lambda i: (i, 0)
    const = lambda i: (0, 0)
    return pl.pallas_call(
        functools.partial(_ffn_kernel, final_norm=final_norm),
        grid=(n // tm,),
        in_specs=[pl.BlockSpec((tm, d), row),
                  pl.BlockSpec((1, d), const),
                  pl.BlockSpec((d, hidden), const),
                  pl.BlockSpec((d, hidden), const),
                  pl.BlockSpec((hidden, d), const),
                  pl.BlockSpec((1, d), const)],
        out_specs=pl.BlockSpec((tm, d), row),
        out_shape=jax.ShapeDtypeStruct((n, d), F32),
        scratch_shapes=[pltpu.VMEM((tm, hidden), BF16)],
        compiler_params=_params(1),
        name="ffn",
    )(x2d, g.reshape(1, d), w_gate, w_up, w_down, g_final.reshape(1, d))


def kernel(x, mem, mem_norm_g, w_mem_kv, mix_norm_g, s5_w_in, s5_lambda_re, s5_lambda_im, s5_log_dt,
           s5_b_re, s5_b_im, s5_c_re, s5_c_im, s5_d, s5_w_glu, s5_b_glu, moba_w_in, w_out, ffn_norm_g,
           w_gate, w_up, w_down, final_norm_g):
    bsz, s, d = x.shape
    mem_len = mem.shape[1]
    depth = mix_norm_g.shape[0]
    tok_w = d - MEM_WIDTH
    n_heads = tok_w // HEAD_DIM
    slopes = jnp.asarray(2.0 ** (-8.0 * (np.arange(n_heads) + 1) / n_heads), dtype=F32)

    mem_kv = _norm_matmul(mem.reshape(bsz * mem_len, d), mem_norm_g, w_mem_kv.astype(BF16)).astype(BF16)
    x2d = x.reshape(bsz * s, d)
    for i in range(depth):
        j = i // 2
        if i % 2 == 0:
            proj = _norm_matmul(x2d, mix_norm_g[i], s5_w_in[j].astype(BF16))
            ops = _s5_operators(s5_lambda_re[j], s5_lambda_im[j], s5_log_dt[j],
                                s5_b_re[j], s5_b_im[j], s5_c_re[j], s5_c_im[j])
            y_ssm = _s5_ssm(proj[:, :tok_w].reshape(bsz, s, tok_w), ops).reshape(bsz * s, tok_w)
            x2d = _s5_mix_out(y_ssm, proj, s5_d[j], s5_w_glu[j].astype(BF16), s5_b_glu[j],
                              mem_kv, w_out[i].astype(BF16), x2d, s, mem_len)
        else:
            proj = _norm_matmul(x2d, mix_norm_g[i], moba_w_in[j].astype(BF16))
            tok = _moba(proj, slopes, bsz, s, n_heads)
            x2d = _moba_mix_out(tok, proj, mem_kv, w_out[i].astype(BF16), x2d, s, mem_len)
        x2d = _ffn(x2d, ffn_norm_g[i], w_gate[i].astype(BF16), w_up[i].astype(BF16),
                   w_down[i].astype(BF16), final_norm_g, final_norm=(i == depth - 1))
    return x2d.reshape(bsz, s, d)
```

```python
import functools
import math

import numpy as np
import jax
import jax.numpy as jnp
from jax import lax
from jax.experimental import pallas as pl
from jax.experimental.pallas import tpu as pltpu

F32 = jnp.float32
BF16 = jnp.bfloat16

HEAD_DIM = 64
MEM_HEADS = 4
MEM_WIDTH = MEM_HEADS * HEAD_DIM
S5_GROUP_CH = 16
S5_STATE = 64
S5_CHUNK = 16
MOBA_BLOCK = 256
MOBA_TOPK = 3
RMS_EPS = 1e-6
NEG_INF = -1e30

LANES = 128
MXU_TILE = 256
VMEM_LIMIT_BYTES = 56 * 1024 * 1024
ROW_TILE = 512

S5_LANE_GROUPS = LANES // S5_GROUP_CH
S5_STEPS_PER_TILE = MXU_TILE // LANES
S5_LAG_TILES = S5_CHUNK // S5_STEPS_PER_TILE


def _params(n_axes):
    return pltpu.CompilerParams(dimension_semantics=("arbitrary",) * n_axes,
                                vmem_limit_bytes=VMEM_LIMIT_BYTES)


def _rms_scale(x, g):
    ms = jnp.mean(x * x, axis=-1, keepdims=True)
    return x * lax.rsqrt(ms + RMS_EPS) * g


def _norm_matmul_kernel(x_ref, g_ref, w_ref, o_ref):
    h = _rms_scale(x_ref[...], g_ref[...])
    o_ref[...] = jnp.dot(h.astype(BF16), w_ref[...], preferred_element_type=F32)


def _norm_matmul(x2d, g, w_bf16):
    n, d = x2d.shape
    f = w_bf16.shape[1]
    tm = min(ROW_TILE, n)
    return pl.pallas_call(
        _norm_matmul_kernel,
        grid=(n // tm,),
        in_specs=[pl.BlockSpec((tm, d), lambda i: (i, 0)),
                  pl.BlockSpec((1, d), lambda i: (0, 0)),
                  pl.BlockSpec((d, f), lambda i: (0, 0))],
        out_specs=pl.BlockSpec((tm, f), lambda i: (i, 0)),
        out_shape=jax.ShapeDtypeStruct((n, f), F32),
        compiler_params=_params(1),
        name="norm_matmul",
    )(x2d, g.reshape(1, d), w_bf16)


def _s5_operators(lam_re, lam_im, log_dt, b_re, b_im, c_re, c_im):
    hi = lax.Precision.HIGHEST
    t_len = S5_CHUNK
    g_cnt, p_cnt = lam_re.shape
    c_cnt = b_re.shape[-1]
    gpb = S5_LANE_GROUPS
    n_lb = g_cnt // gpb
    lr = lam_re.astype(F32)
    li = lam_im.astype(F32)
    dt = jnp.exp(log_dt.astype(F32))[:, None]
    mag = jnp.exp(lr * dt)
    ar = mag * jnp.cos(li * dt)
    ai = mag * jnp.sin(li * dt)
    den = lr * lr + li * li
    nr = ar - 1.0
    fr = (nr * lr + ai * li) / den
    fi = (ai * lr - nr * li) / den
    br = b_re.astype(F32)
    bi = b_im.astype(F32)
    bbr = fr[..., None] * br - fi[..., None] * bi
    bbi = fr[..., None] * bi + fi[..., None] * br
    cr = c_re.astype(F32)
    ci = c_im.astype(F32)

    k = jnp.arange(t_len + 1, dtype=F32)[:, None, None]
    pm = jnp.exp(k * (lr * dt)[None])
    pr = pm * jnp.cos(k * (li * dt)[None])
    pi = pm * jnp.sin(k * (li * dt)[None])

    wr = pr[..., None] * bbr[None] - pi[..., None] * bbi[None]
    wi = pr[..., None] * bbi[None] + pi[..., None] * bbr[None]
    kern = (jnp.einsum('gcp,kgpd->kgcd', cr, wr[:t_len], precision=hi)
            - jnp.einsum('gcp,kgpd->kgcd', ci, wi[:t_len], precision=hi))
    eye = jnp.eye(gpb, dtype=F32)

    spt = S5_STEPS_PER_TILE
    lag = (jnp.arange(S5_LAG_TILES)[:, None, None] * spt
           + jnp.arange(spt)[None, None, :] - jnp.arange(spt)[None, :, None])
    kd = jnp.where((lag >= 0)[..., None, None, None], kern[jnp.clip(lag, 0, t_len - 1)], 0.0)
    kd = kd.reshape(S5_LAG_TILES, spt, spt, n_lb, gpb, c_cnt, c_cnt)
    toep = jnp.einsum('ljtbgoc,gh->bljgctho', kd, eye).reshape(
        n_lb, S5_LAG_TILES, spt * gpb * c_cnt, spt * gpb * c_cnt)

    w_rev = jnp.stack([wr[:t_len][::-1], wi[:t_len][::-1]])
    w_rev = w_rev.reshape(2, t_len, n_lb, gpb, p_cnt, c_cnt)
    p_op = jnp.einsum('rjbgpc,gh->bjgcrhp', w_rev, eye).reshape(
        n_lb, t_len * gpb * c_cnt, 2 * gpb * p_cnt)

    qr = cr[None] * pr[1:, :, None, :] - ci[None] * pi[1:, :, None, :]
    qi = -(cr[None] * pi[1:, :, None, :] + ci[None] * pr[1:, :, None, :])
    q_all = jnp.stack([qr, qi]).reshape(2, t_len, n_lb, gpb, c_cnt, p_cnt)
    q_op = jnp.einsum('rtbgop,gh->brhptgo', q_all, eye).reshape(
        n_lb, 2 * gpb * p_cnt, t_len * gpb * c_cnt)

    coef = jnp.stack([pr[t_len].reshape(n_lb, gpb * p_cnt),
                      pi[t_len].reshape(n_lb, gpb * p_cnt)], axis=1)
    return toep.astype(BF16), p_op.astype(BF16), q_op.astype(BF16), coef


def _s5_ssm_kernel(u_ref, toep_ref, p_ref, q_ref, coef_ref, y_ref, h_scr, s_scr, xp_scr, *, n_chunks):
    t_len = S5_CHUNK
    half = S5_LANE_GROUPS * S5_STATE
    for j in range(t_len):
        h_scr[:, j * LANES:(j + 1) * LANES] = u_ref[pl.ds(j, n_chunks, stride=t_len), :].astype(BF16)
    s_scr[...] = jnp.dot(h_scr[...], p_ref[0], preferred_element_type=F32)
    cr = coef_ref[0, 0:1, :]
    ci = coef_ref[0, 1:2, :]

    def step(n, carry):
        xr, xi = carry
        xp_scr[pl.ds(n, 1), 0:half] = xr
        xp_scr[pl.ds(n, 1), half:2 * half] = xi
        sr = s_scr[pl.ds(n, 1), 0:half]
        si = s_scr[pl.ds(n, 1), half:2 * half]
        return cr * xr - ci * xi + sr, cr * xi + ci * xr + si

    zero = jnp.zeros((1, half), F32)
    lax.fori_loop(0, n_chunks, step, (zero, zero))
    xp = xp_scr[...].astype(BF16)
    for nt in range(S5_LAG_TILES):
        cols = slice(nt * MXU_TILE, (nt + 1) * MXU_TILE)
        acc = jnp.dot(xp, q_ref[0, :, cols], preferred_element_type=F32)
        for kt in range(nt + 1):
            acc = acc + jnp.dot(h_scr[:, kt * MXU_TILE:(kt + 1) * MXU_TILE], toep_ref[0, nt - kt],
                                preferred_element_type=F32)
        for tt in range(S5_STEPS_PER_TILE):
            t = nt * S5_STEPS_PER_TILE + tt
            y_ref[pl.ds(t, n_chunks, stride=t_len), :] = acc[:, tt * LANES:(tt + 1) * LANES]


def _s5_ssm(proj, ops, bsz, s, tok_w):
    toep, p_op, q_op, coef = ops
    n_lb = tok_w // LANES
    n_chunks = s // S5_CHUNK
    state_w = p_op.shape[2]
    flat_w = p_op.shape[1]
    const3 = lambda l, b: (l, 0, 0)
    return pl.pallas_call(
        functools.partial(_s5_ssm_kernel, n_chunks=n_chunks),
        grid=(n_lb, bsz),
        in_specs=[pl.BlockSpec((s, LANES), lambda l, b: (b, l)),
                  pl.BlockSpec((1, S5_LAG_TILES, MXU_TILE, MXU_TILE), lambda l, b: (l, 0, 0, 0)),
                  pl.BlockSpec((1, flat_w, state_w), const3),
                  pl.BlockSpec((1, state_w, flat_w), const3),
                  pl.BlockSpec((1, 2, state_w // 2), const3)],
        out_specs=pl.BlockSpec((s, LANES), lambda l, b: (b, l)),
        out_shape=jax.ShapeDtypeStruct((bsz * s, tok_w), F32),
        scratch_shapes=[pltpu.VMEM((n_chunks, flat_w), BF16),
                        pltpu.VMEM((n_chunks, state_w), F32),
                        pltpu.VMEM((n_chunks, state_w), F32)],
        compiler_params=_params(2),
        name="s5_ssm",
    )(proj, toep, p_op, q_op, coef)


MOBA_Q_BLOCKS = 2


def _moba_kernel(slope_ref, q_ref, k_ref, v_ref, o_ref,
                 ka_ref, kb_ref, vt_ref, km_ref, rel_ref, sel_ref, acc_ref, *, n_blocks):
    blk = MOBA_BLOCK
    nq = MOBA_Q_BLOCKS
    tq = nq * blk
    hp = pl.program_id(1)
    qi = pl.program_id(2)
    slopes = (slope_ref[2 * hp], slope_ref[2 * hp + 1])
    k_refs = (ka_ref, kb_ref)

    @pl.when(qi == 0)
    def _prepare():
        lane = lax.broadcasted_iota(jnp.int32, (blk, LANES), 1)
        for j in range(n_blocks):
            kj = k_ref[j * blk:(j + 1) * blk, :]
            ka_ref[j] = jnp.where(lane < HEAD_DIM, kj, 0.0).astype(BF16)
            kb_ref[j] = jnp.where(lane >= HEAD_DIM, kj, 0.0).astype(BF16)
            vt_ref[j] = v_ref[j * blk:(j + 1) * blk, :].T.astype(BF16)
            km_ref[j:j + 1, :] = jnp.mean(kj, axis=0, keepdims=True)
        d = (lax.broadcasted_iota(jnp.int32, (blk, blk), 1)
             - lax.broadcasted_iota(jnp.int32, (blk, blk), 0))
        df = d.astype(F32)
        for e in range(2):
            rel_ref[2 * e] = slopes[e] * df
            rel_ref[2 * e + 1] = jnp.where(d >= 0, slopes[e] * df, -NEG_INF)

    qs = (q_ref[...] * (HEAD_DIM ** -0.5)).astype(BF16)
    nt = (((1,), (1,)), ((), ()))

    km = km_ref[...]
    lane_m = lax.broadcasted_iota(jnp.int32, km.shape, 1)
    blk_id = lax.broadcasted_iota(jnp.int32, (n_blocks, tq), 0)
    q_blk = nq * qi + lax.broadcasted_iota(jnp.int32, (n_blocks, tq), 1) // blk
    past = blk_id < q_blk
    for e in range(2):
        in_head = (lane_m < HEAD_DIM) if e == 0 else (lane_m >= HEAD_DIM)
        km_e = jnp.where(in_head, km, 0.0).astype(BF16)
        gate = lax.dot_general(km_e, qs, nt, preferred_element_type=F32)
        gate = jnp.where(past, gate, NEG_INF)
        rank = jnp.zeros((n_blocks, tq), jnp.int32)
        for jp in range(n_blocks):
            gj = gate[jp:jp + 1, :]
            beats = (gj > gate) | ((gj == gate) & (jp < blk_id))
            rank = rank + beats.astype(jnp.int32)
        visible = (past & (rank < MOBA_TOPK)) | (blk_id == q_blk)
        offset = slopes[e] * ((q_blk - blk_id) * blk).astype(F32)
        sel_ref[e] = jnp.where(visible, 0.0, NEG_INF) - offset

    def chunk(c, stats):
        new_stats = []
        for e in range(2):
            m_old, l_old = stats[e]
            scores = []
            for w in range(nq):
                j = nq * c + w
                s_t = lax.dot_general(k_refs[e][j], qs, nt, preferred_element_type=F32)
                parts = []
                for h in range(nq):
                    own = (j == nq * qi + h).astype(jnp.int32)
                    parts.append(s_t[:, h * blk:(h + 1) * blk] - rel_ref[2 * e + own])
                s_t = jnp.concatenate(parts, axis=1) + sel_ref[e, pl.ds(j, 1), :]
                scores.append(s_t)
            m_blk = jnp.max(scores[0], axis=0, keepdims=True)
            for s_t in scores[1:]:
                m_blk = jnp.maximum(m_blk, jnp.max(s_t, axis=0, keepdims=True))
            m_new = jnp.maximum(m_old, m_blk)
            alpha = jnp.exp(m_old - m_new)
            l_new = alpha * l_old
            pv = None
            for w in range(nq):
                j = nq * c + w
                p = jnp.exp(scores[w] - m_new)
                l_new = l_new + jnp.sum(p, axis=0, keepdims=True)
                d = jnp.dot(vt_ref[j, e * HEAD_DIM:(e + 1) * HEAD_DIM, :], p.astype(BF16),
                            preferred_element_type=F32)
                pv = d if pv is None else pv + d
            acc_ref[e] = alpha * acc_ref[e] + pv
            new_stats.append((m_new, l_new))
        return tuple(new_stats)

    acc_ref[...] = jnp.zeros_like(acc_ref)
    init = (jnp.full((1, tq), NEG_INF, F32), jnp.zeros((1, tq), F32))
    stats = lax.fori_loop(0, qi + 1, chunk, (init, init))
    out_t = jnp.concatenate([acc_ref[0] / stats[0][1], acc_ref[1] / stats[1][1]], axis=0)
    o_ref[...] = out_t.T


def _moba(qkv, slopes, bsz, s, n_heads):
    n_blocks = s // MOBA_BLOCK
    n_pairs = n_heads // 2
    blk = MOBA_BLOCK
    tq = MOBA_Q_BLOCKS * blk
    n_tiles = s // tq
    return pl.pallas_call(
        functools.partial(_moba_kernel, n_blocks=n_blocks),
        grid=(bsz, n_pairs, n_tiles),
        in_specs=[pl.BlockSpec(memory_space=pltpu.SMEM),
                  pl.BlockSpec((tq, LANES), lambda b, h, i: (b * n_tiles + i, h)),
                  pl.BlockSpec((s, LANES), lambda b, h, i: (b, n_pairs + h)),
                  pl.BlockSpec((s, LANES), lambda b, h, i: (b, 2 * n_pairs + h))],
        out_specs=pl.BlockSpec((tq, LANES), lambda b, h, i: (b * n_tiles + i, h)),
        out_shape=jax.ShapeDtypeStruct((bsz * s, n_heads * HEAD_DIM), F32),
        scratch_shapes=[pltpu.VMEM((n_blocks, blk, LANES), BF16),
                        pltpu.VMEM((n_blocks, blk, LANES), BF16),
                        pltpu.VMEM((n_blocks, LANES, blk), BF16),
                        pltpu.VMEM((n_blocks, LANES), F32),
                        pltpu.VMEM((4, blk, blk), F32),
                        pltpu.VMEM((2, n_blocks, tq), F32),
                        pltpu.VMEM((2, HEAD_DIM, tq), F32)],
        compiler_params=_params(3),
        name="moba",
    )(slopes, qkv, qkv, qkv)


def _memory_attention(q, mem_k, mem_v):
    q = q * (HEAD_DIM ** -0.5)
    lane = lax.broadcasted_iota(jnp.int32, q.shape, 1)
    out = jnp.zeros(q.shape, F32)
    for h in range(MEM_HEADS):
        in_head = (lane >= h * HEAD_DIM) & (lane < (h + 1) * HEAD_DIM)
        qh = jnp.where(in_head, q, 0.0).astype(BF16)
        s = lax.dot_general(qh, mem_k, (((1,), (1,)), ((), ())), preferred_element_type=F32)
        m = jnp.max(s, axis=-1, keepdims=True)
        p = jnp.exp(s - m)
        l = jnp.sum(p, axis=-1, keepdims=True)
        pv = jnp.dot(p.astype(BF16), mem_v, preferred_element_type=F32)
        out = jnp.where(in_head, pv / l, out)
    return out


def _project_out(tok, mem_out, wo_ref, x):
    tw = tok.shape[1]
    return (x + jnp.dot(tok.astype(BF16), wo_ref[0:tw, :], preferred_element_type=F32)
            + jnp.dot(mem_out.astype(BF16), wo_ref[tw:, :], preferred_element_type=F32))


def _s5_mix_out_kernel(y_ref, u_ref, d_ref, wglu_ref, bglu_ref, mq_ref, mk_ref, mv_ref, wo_ref, x_ref, o_ref):
    y = y_ref[...] + d_ref[...] * u_ref[...]
    y = 0.5 * y * (1.0 + lax.erf(y * math.sqrt(0.5)))
    z = jnp.dot(y.astype(BF16), wglu_ref[...], preferred_element_type=F32) + bglu_ref[...]
    tok = y * jax.nn.sigmoid(z)
    mem_out = _memory_attention(mq_ref[...], mk_ref[...], mv_ref[...])
    o_ref[...] = _project_out(tok, mem_out, wo_ref, x_ref[...])


def _moba_mix_out_kernel(tok_ref, mq_ref, mk_ref, mv_ref, wo_ref, x_ref, o_ref):
    mem_out = _memory_attention(mq_ref[...], mk_ref[...], mv_ref[...])
    o_ref[...] = _project_out(tok_ref[...], mem_out, wo_ref, x_ref[...])


def _mix_out_specs(n, s, d, tok_w, mem_len, mq_block_col):
    tm = min(ROW_TILE, s)
    per_b = s // tm
    row = lambda i: (i, 0)
    const = lambda i: (0, 0)
    mem_specs = [pl.BlockSpec((tm, MEM_WIDTH), lambda i: (i, mq_block_col)),
                 pl.BlockSpec((mem_len, MEM_WIDTH), lambda i: (i // per_b, 0)),
                 pl.BlockSpec((mem_len, MEM_WIDTH), lambda i: (i // per_b, 1)),
                 pl.BlockSpec((d, d), const),
                 pl.BlockSpec((tm, d), row)]
    return tm, row, const, mem_specs


def _s5_mix_out(y_ssm, proj, d_skip, w_glu, b_glu, mem_kv, w_out, x2d, s, mem_len):
    n, d = x2d.shape
    tok_w = y_ssm.shape[1]
    tm, row, const, mem_specs = _mix_out_specs(n, s, d, tok_w, mem_len, tok_w // MEM_WIDTH)
    return pl.pallas_call(
        _s5_mix_out_kernel,
        grid=(n // tm,),
        in_specs=[pl.BlockSpec((tm, tok_w), row),
                  pl.BlockSpec((tm, tok_w), row),
                  pl.BlockSpec((1, tok_w), const),
                  pl.BlockSpec((tok_w, tok_w), const),
                  pl.BlockSpec((1, tok_w), const)] + mem_specs,
        out_specs=pl.BlockSpec((tm, d), row),
        out_shape=jax.ShapeDtypeStruct((n, d), F32),
        compiler_params=_params(1),
        name="s5_mix_out",
    )(y_ssm, proj, d_skip.reshape(1, tok_w), w_glu, b_glu.reshape(1, tok_w), proj, mem_kv, mem_kv, w_out, x2d)


def _moba_mix_out(tok, proj, mem_kv, w_out, x2d, s, mem_len):
    n, d = x2d.shape
    tok_w = tok.shape[1]
    tm, row, const, mem_specs = _mix_out_specs(n, s, d, tok_w, mem_len, (proj.shape[1] - MEM_WIDTH) // MEM_WIDTH)
    return pl.pallas_call(
        _moba_mix_out_kernel,
        grid=(n // tm,),
        in_specs=[pl.BlockSpec((tm, tok_w), row)] + mem_specs,
        out_specs=pl.BlockSpec((tm, d), row),
        out_shape=jax.ShapeDtypeStruct((n, d), F32),
        compiler_params=_params(1),
        name="moba_mix_out",
    )(tok, proj, mem_kv, mem_kv, w_out, x2d)


FFN_CHUNK = MXU_TILE


def _ffn_kernel(x_ref, g_ref, wg_ref, wu_ref, wd_ref, gf_ref, o_ref, act_ref, *, final_norm):
    x = x_ref[...]
    h = _rms_scale(x, g_ref[...]).astype(BF16)
    hidden = wg_ref.shape[1]
    for c in range(hidden // FFN_CHUNK):
        cols = slice(c * FFN_CHUNK, (c + 1) * FFN_CHUNK)
        gate = jnp.dot(h, wg_ref[:, cols], preferred_element_type=F32)
        up = jnp.dot(h, wu_ref[:, cols], preferred_element_type=F32)
        act_ref[:, cols] = (gate * jax.nn.sigmoid(gate) * up).astype(BF16)
    y = x + jnp.dot(act_ref[...], wd_ref[...], preferred_element_type=F32)
    if final_norm:
        y = _rms_scale(y, gf_ref[...])
    o_ref[...] = y


def _ffn(x2d, g, w_gate, w_up, w_down, g_final, final_norm):
    n, d = x2d.shape
    hidden = w_gate.shape[1]
    tm = min(ROW_TILE, n)
    row = lambda i: (i, 0)
    const = lambda i: (0, 0)
    return pl.pallas_call(
        functools.partial(_ffn_kernel, final_norm=final_norm),
        grid=(n // tm,),
        in_specs=[pl.BlockSpec((tm, d), row),
                  pl.BlockSpec((1, d), const),
                  pl.BlockSpec((d, hidden), const),
                  pl.BlockSpec((d, hidden), const),
                  pl.BlockSpec((hidden, d), const),
                  pl.BlockSpec((1, d), const)],
        out_specs=pl.BlockSpec((tm, d), row),
        out_shape=jax.ShapeDtypeStruct((n, d), F32),
        scratch_shapes=[pltpu.VMEM((tm, hidden), BF16)],
        compiler_params=_params(1),
        name="ffn",
    )(x2d, g.reshape(1, d), w_gate, w_up, w_down, g_final.reshape(1, d))


def kernel(x, mem, mem_norm_g, w_mem_kv, mix_norm_g, s5_w_in, s5_lambda_re, s5_lambda_im, s5_log_dt,
           s5_b_re, s5_b_im, s5_c_re, s5_c_im, s5_d, s5_w_glu, s5_b_glu, moba_w_in, w_out, ffn_norm_g,
           w_gate, w_up, w_down, final_norm_g):
    bsz, s, d = x.shape
    mem_len = mem.shape[1]
    depth = mix_norm_g.shape[0]
    tok_w = d - MEM_WIDTH
    n_heads = tok_w // HEAD_DIM
    slopes = jnp.asarray(2.0 ** (-8.0 * (np.arange(n_heads) + 1) / n_heads), dtype=F32)

    mem_kv = _norm_matmul(mem.reshape(bsz * mem_len, d), mem_norm_g, w_mem_kv.astype(BF16)).astype(BF16)
    x2d = x.reshape(bsz * s, d)
    for i in range(depth):
        j = i // 2
        if i % 2 == 0:
            proj = _norm_matmul(x2d, mix_norm_g[i], s5_w_in[j].astype(BF16))
            ops = _s5_operators(s5_lambda_re[j], s5_lambda_im[j], s5_log_dt[j],
                                s5_b_re[j], s5_b_im[j], s5_c_re[j], s5_c_im[j])
            y_ssm = _s5_ssm(proj, ops, bsz, s, tok_w)
            x2d = _s5_mix_out(y_ssm, proj, s5_d[j], s5_w_glu[j].astype(BF16), s5_b_glu[j],
                              mem_kv, w_out[i].astype(BF16), x2d, s, mem_len)
        else:
            proj = _norm_matmul(x2d, mix_norm_g[i], moba_w_in[j].astype(BF16))
            tok = _moba(proj, slopes, bsz, s, n_heads)
            x2d = _moba_mix_out(tok, proj, mem_kv, w_out[i].astype(BF16), x2d, s, mem_len)
        x2d = _ffn(x2d, ffn_norm_g[i], w_gate[i].astype(BF16), w_up[i].astype(BF16),
                   w_down[i].astype(BF16), final_norm_g, final_norm=(i == depth - 1))
    return x2d.reshape(bsz, s, d)
```

```python
import functools
import math

import numpy as np
import jax
import jax.numpy as jnp
from jax import lax
from jax.experimental import pallas as pl
from jax.experimental.pallas import tpu as pltpu

F32 = jnp.float32
BF16 = jnp.bfloat16

HEAD_DIM = 64
MEM_HEADS = 4
MEM_WIDTH = MEM_HEADS * HEAD_DIM
S5_GROUP_CH = 16
S5_STATE = 64
S5_CHUNK = 16
MOBA_BLOCK = 256
MOBA_TOPK = 3
RMS_EPS = 1e-6
NEG_INF = -1e30

LANES = 128
MXU_TILE = 256
VMEM_LIMIT_BYTES = 56 * 1024 * 1024
ROW_TILE = 512

S5_LANE_GROUPS = LANES // S5_GROUP_CH
S5_STEPS_PER_TILE = MXU_TILE // LANES
S5_LAG_TILES = S5_CHUNK // S5_STEPS_PER_TILE


def _params(n_axes):
    return pltpu.CompilerParams(dimension_semantics=("arbitrary",) * n_axes,
                                vmem_limit_bytes=VMEM_LIMIT_BYTES)


def _rms_scale(x, g):
    ms = jnp.mean(x * x, axis=-1, keepdims=True)
    return x * lax.rsqrt(ms + RMS_EPS) * g


def _norm_matmul_kernel(x_ref, g_ref, w_ref, o_ref):
    h = _rms_scale(x_ref[...], g_ref[...])
    o_ref[...] = jnp.dot(h.astype(BF16), w_ref[...], preferred_element_type=F32)


def _norm_matmul(x2d, g, w_bf16):
    n, d = x2d.shape
    f = w_bf16.shape[1]
    tm = min(ROW_TILE, n)
    return pl.pallas_call(
        _norm_matmul_kernel,
        grid=(n // tm,),
        in_specs=[pl.BlockSpec((tm, d), lambda i: (i, 0)),
                  pl.BlockSpec((1, d), lambda i: (0, 0)),
                  pl.BlockSpec((d, f), lambda i: (0, 0))],
        out_specs=pl.BlockSpec((tm, f), lambda i: (i, 0)),
        out_shape=jax.ShapeDtypeStruct((n, f), F32),
        compiler_params=_params(1),
        name="norm_matmul",
    )(x2d, g.reshape(1, d), w_bf16)


def _s5_operators(lam_re, lam_im, log_dt, b_re, b_im, c_re, c_im):
    hi = lax.Precision.HIGHEST
    t_len = S5_CHUNK
    g_cnt, p_cnt = lam_re.shape
    c_cnt = b_re.shape[-1]
    gpb = S5_LANE_GROUPS
    n_lb = g_cnt // gpb
    lr = lam_re.astype(F32)
    li = lam_im.astype(F32)
    dt = jnp.exp(log_dt.astype(F32))[:, None]
    mag = jnp.exp(lr * dt)
    ar = mag * jnp.cos(li * dt)
    ai = mag * jnp.sin(li * dt)
    den = lr * lr + li * li
    nr = ar - 1.0
    fr = (nr * lr + ai * li) / den
    fi = (ai * lr - nr * li) / den
    br = b_re.astype(F32)
    bi = b_im.astype(F32)
    bbr = fr[..., None] * br - fi[..., None] * bi
    bbi = fr[..., None] * bi + fi[..., None] * br
    cr = c_re.astype(F32)
    ci = c_im.astype(F32)

    k = jnp.arange(t_len + 1, dtype=F32)[:, None, None]
    pm = jnp.exp(k * (lr * dt)[None])
    pr = pm * jnp.cos(k * (li * dt)[None])
    pi = pm * jnp.sin(k * (li * dt)[None])

    wr = pr[..., None] * bbr[None] - pi[..., None] * bbi[None]
    wi = pr[..., None] * bbi[None] + pi[..., None] * bbr[None]
    kern = (jnp.einsum('gcp,kgpd->kgcd', cr, wr[:t_len], precision=hi)
            - jnp.einsum('gcp,kgpd->kgcd', ci, wi[:t_len], precision=hi))
    eye = jnp.eye(gpb, dtype=F32)

    spt = S5_STEPS_PER_TILE
    lag = (jnp.arange(S5_LAG_TILES)[:, None, None] * spt
           + jnp.arange(spt)[None, None, :] - jnp.arange(spt)[None, :, None])
    kd = jnp.where((lag >= 0)[..., None, None, None], kern[jnp.clip(lag, 0, t_len - 1)], 0.0)
    kd = kd.reshape(S5_LAG_TILES, spt, spt, n_lb, gpb, c_cnt, c_cnt)
    toep = jnp.einsum('ljtbgoc,gh->bljgctho', kd, eye).reshape(
        n_lb, S5_LAG_TILES, spt * gpb * c_cnt, spt * gpb * c_cnt)

    w_rev = jnp.stack([wr[:t_len][::-1], wi[:t_len][::-1]])
    w_rev = w_rev.reshape(2, t_len, n_lb, gpb, p_cnt, c_cnt)
    p_op = jnp.einsum('rjbgpc,gh->bjgcrhp', w_rev, eye).reshape(
        n_lb, t_len * gpb * c_cnt, 2 * gpb * p_cnt)

    qr = cr[None] * pr[1:, :, None, :] - ci[None] * pi[1:, :, None, :]
    qi = -(cr[None] * pi[1:, :, None, :] + ci[None] * pr[1:, :, None, :])
    q_all = jnp.stack([qr, qi]).reshape(2, t_len, n_lb, gpb, c_cnt, p_cnt)
    q_op = jnp.einsum('rtbgop,gh->brhptgo', q_all, eye).reshape(
        n_lb, 2 * gpb * p_cnt, t_len * gpb * c_cnt)

    coef = jnp.stack([pr[t_len].reshape(n_lb, gpb * p_cnt),
                      pi[t_len].reshape(n_lb, gpb * p_cnt)], axis=1)
    return toep.astype(BF16), p_op.astype(BF16), q_op.astype(BF16), coef


def _s5_ssm_kernel(u_ref, toep_ref, p_ref, q_ref, coef_ref, y_ref, h_scr, s_scr, xp_scr, *, n_chunks):
    t_len = S5_CHUNK
    half = S5_LANE_GROUPS * S5_STATE
    for j in range(t_len):
        h_scr[:, j * LANES:(j + 1) * LANES] = u_ref[pl.ds(j, n_chunks, stride=t_len), :].astype(BF16)
    s_scr[...] = jnp.dot(h_scr[...], p_ref[0], preferred_element_type=F32)
    cr = coef_ref[0, 0:1, :]
    ci = coef_ref[0, 1:2, :]

    def step(n, carry):
        xr, xi = carry
        xp_scr[pl.ds(n, 1), 0:half] = xr
        xp_scr[pl.ds(n, 1), half:2 * half] = xi
        sr = s_scr[pl.ds(n, 1), 0:half]
        si = s_scr[pl.ds(n, 1), half:2 * half]
        return cr * xr - ci * xi + sr, cr * xi + ci * xr + si

    zero = jnp.zeros((1, half), F32)
    lax.fori_loop(0, n_chunks, step, (zero, zero))
    xp = xp_scr[...].astype(BF16)
    for nt in range(S5_LAG_TILES):
        cols = slice(nt * MXU_TILE, (nt + 1) * MXU_TILE)
        acc = jnp.dot(xp, q_ref[0, :, cols], preferred_element_type=F32)
        for kt in range(nt + 1):
            acc = acc + jnp.dot(h_scr[:, kt * MXU_TILE:(kt + 1) * MXU_TILE], toep_ref[0, nt - kt],
                                preferred_element_type=F32)
        for tt in range(S5_STEPS_PER_TILE):
            t = nt * S5_STEPS_PER_TILE + tt
            y_ref[pl.ds(t, n_chunks, stride=t_len), :] = acc[:, tt * LANES:(tt + 1) * LANES]


def _s5_ssm(proj, ops, bsz, s, tok_w):
    toep, p_op, q_op, coef = ops
    n_lb = tok_w // LANES
    n_chunks = s // S5_CHUNK
    state_w = p_op.shape[2]
    flat_w = p_op.shape[1]
    const3 = lambda l, b: (l, 0, 0)
    return pl.pallas_call(
        functools.partial(_s5_ssm_kernel, n_chunks=n_chunks),
        grid=(n_lb, bsz),
        in_specs=[pl.BlockSpec((s, LANES), lambda l, b: (b, l)),
                  pl.BlockSpec((1, S5_LAG_TILES, MXU_TILE, MXU_TILE), lambda l, b: (l, 0, 0, 0)),
                  pl.BlockSpec((1, flat_w, state_w), const3),
                  pl.BlockSpec((1, state_w, flat_w), const3),
                  pl.BlockSpec((1, 2, state_w // 2), const3)],
        out_specs=pl.BlockSpec((s, LANES), lambda l, b: (b, l)),
        out_shape=jax.ShapeDtypeStruct((bsz * s, tok_w), F32),
        scratch_shapes=[pltpu.VMEM((n_chunks, flat_w), BF16),
                        pltpu.VMEM((n_chunks, state_w), F32),
                        pltpu.VMEM((n_chunks, state_w), F32)],
        compiler_params=_params(2),
        name="s5_ssm",
    )(proj, toep, p_op, q_op, coef)


MOBA_Q_BLOCKS = 2
MOBA_VT_ROWS = HEAD_DIM + 16
LOG2E = math.log2(math.e)


def _moba_kernel(slope_ref, q_ref, k_ref, v_ref, o_ref,
                 ka_ref, kb_ref, vt_ref, km_ref, rel_ref, sel_ref, acc_ref, sca_ref, scb_ref, *, n_blocks):
    blk = MOBA_BLOCK
    nq = MOBA_Q_BLOCKS
    tq = nq * blk
    hp = pl.program_id(1)
    qi = pl.program_id(2)
    slopes = (slope_ref[2 * hp] * LOG2E, slope_ref[2 * hp + 1] * LOG2E)
    k_refs = (ka_ref, kb_ref)

    @pl.when(qi == 0)
    def _prepare():
        lane = lax.broadcasted_iota(jnp.int32, (blk, LANES), 1)
        ones_row = (lax.broadcasted_iota(jnp.int32, (MOBA_VT_ROWS - HEAD_DIM, blk), 0) == 0).astype(BF16)
        for j in range(n_blocks):
            kj = k_ref[j * blk:(j + 1) * blk, :]
            ka_ref[j] = jnp.where(lane < HEAD_DIM, kj, 0.0).astype(BF16)
            kb_ref[j] = jnp.where(lane >= HEAD_DIM, kj, 0.0).astype(BF16)
            v_t = v_ref[j * blk:(j + 1) * blk, :].T
            for e in range(2):
                vt_ref[j, e, 0:HEAD_DIM, :] = v_t[e * HEAD_DIM:(e + 1) * HEAD_DIM, :].astype(BF16)
                vt_ref[j, e, HEAD_DIM:MOBA_VT_ROWS, :] = ones_row
            km_ref[j:j + 1, :] = jnp.mean(kj, axis=0, keepdims=True)
        d = (lax.broadcasted_iota(jnp.int32, (blk, blk), 1)
             - lax.broadcasted_iota(jnp.int32, (blk, blk), 0))
        df = d.astype(F32)
        for e in range(2):
            plain = slopes[e] * df
            causal = jnp.where(d >= 0, plain, -NEG_INF)
            rel_ref[e, 0] = jnp.concatenate([plain, plain], axis=1)
            rel_ref[e, 1] = jnp.concatenate([causal, plain], axis=1)
            rel_ref[e, 2] = jnp.concatenate([plain, causal], axis=1)

    qs = (q_ref[...] * (HEAD_DIM ** -0.5 * LOG2E)).astype(BF16)
    nt = (((1,), (1,)), ((), ()))

    km = km_ref[...]
    lane_m = lax.broadcasted_iota(jnp.int32, km.shape, 1)
    blk_id = lax.broadcasted_iota(jnp.int32, (n_blocks, tq), 0)
    q_blk = nq * qi + lax.broadcasted_iota(jnp.int32, (n_blocks, tq), 1) // blk
    past = blk_id < q_blk
    for e in range(2):
        in_head = (lane_m < HEAD_DIM) if e == 0 else (lane_m >= HEAD_DIM)
        km_e = jnp.where(in_head, km, 0.0).astype(BF16)
        gate = lax.dot_general(km_e, qs, nt, preferred_element_type=F32)
        gate = jnp.where(past, gate, NEG_INF)
        rank = jnp.zeros((n_blocks, tq), jnp.int32)
        for jp in range(n_blocks):
            gj = gate[jp:jp + 1, :]
            beats = (gj > gate) | ((gj == gate) & (jp < blk_id))
            rank = rank + beats.astype(jnp.int32)
        visible = (past & (rank < MOBA_TOPK)) | (blk_id == q_blk)
        offset = slopes[e] * ((q_blk - blk_id) * blk).astype(F32)
        sel_ref[e] = jnp.where(visible, 0.0, NEG_INF) - offset

    def score_chunk(c, sc_ref):
        on_diag = c == qi
        maxes = []
        for e in range(2):
            m_blk = None
            for w in range(nq):
                j = nq * c + w
                kind = jnp.where(on_diag, 1 + w, 0)
                s_t = lax.dot_general(k_refs[e][j], qs, nt, preferred_element_type=F32)
                s_t = s_t - rel_ref[e, kind] + sel_ref[e, pl.ds(j, 1), :]
                sc_ref[e, w] = s_t
                mx = jnp.max(s_t, axis=0, keepdims=True)
                m_blk = mx if m_blk is None else jnp.maximum(m_blk, mx)
            maxes.append(m_blk)
        return tuple(maxes)

    def accumulate_chunk(c, sc_ref, m_run, m_blk):
        m_next = []
        for e in range(2):
            m_new = jnp.maximum(m_run[e], m_blk[e])
            pv = None
            for w in range(nq):
                p = jnp.exp2(sc_ref[e, w] - m_new).astype(BF16)
                d = jnp.dot(vt_ref[nq * c + w, e], p, preferred_element_type=F32)
                pv = d if pv is None else pv + d
            acc_ref[e] = jnp.exp2(m_run[e] - m_new) * acc_ref[e] + pv
            m_next.append(m_new)
        return tuple(m_next)

    acc_ref[...] = jnp.zeros_like(acc_ref)
    m_init = jnp.full((1, tq), NEG_INF, F32)
    n_chunks = qi + 1

    def body(t, carry):
        m_run, m_blk = carry
        c = 2 * t
        m_blk_b = score_chunk(c + 1, scb_ref)
        m_run = accumulate_chunk(c, sca_ref, m_run, m_blk)
        m_blk_a = score_chunk(jnp.minimum(c + 2, qi), sca_ref)
        m_run = accumulate_chunk(c + 1, scb_ref, m_run, m_blk_b)
        return m_run, m_blk_a

    m_run, m_blk = lax.fori_loop(0, n_chunks // 2, body, ((m_init, m_init), score_chunk(0, sca_ref)))

    @pl.when(n_chunks % 2 == 1)
    def _last_chunk():
        accumulate_chunk(qi, sca_ref, m_run, m_blk)

    out_t = jnp.concatenate([acc_ref[e, 0:HEAD_DIM, :] / acc_ref[e, HEAD_DIM:HEAD_DIM + 1, :]
                             for e in range(2)], axis=0)
    o_ref[...] = out_t.T


def _moba(qkv, slopes, bsz, s, n_heads):
    n_blocks = s // MOBA_BLOCK
    n_pairs = n_heads // 2
    blk = MOBA_BLOCK
    tq = MOBA_Q_BLOCKS * blk
    n_tiles = s // tq
    return pl.pallas_call(
        functools.partial(_moba_kernel, n_blocks=n_blocks),
        grid=(bsz, n_pairs, n_tiles),
        in_specs=[pl.BlockSpec(memory_space=pltpu.SMEM),
                  pl.BlockSpec((tq, LANES), lambda b, h, i: (b * n_tiles + i, h)),
                  pl.BlockSpec((s, LANES), lambda b, h, i: (b, n_pairs + h)),
                  pl.BlockSpec((s, LANES), lambda b, h, i: (b, 2 * n_pairs + h))],
        out_specs=pl.BlockSpec((tq, LANES), lambda b, h, i: (b * n_tiles + i, h)),
        out_shape=jax.ShapeDtypeStruct((bsz * s, n_heads * HEAD_DIM), F32),
        scratch_shapes=[pltpu.VMEM((n_blocks, blk, LANES), BF16),
                        pltpu.VMEM((n_blocks, blk, LANES), BF16),
                        pltpu.VMEM((n_blocks, 2, MOBA_VT_ROWS, blk), BF16),
                        pltpu.VMEM((n_blocks, LANES), F32),
                        pltpu.VMEM((2, 3, blk, tq), F32),
                        pltpu.VMEM((2, n_blocks, tq), F32),
                        pltpu.VMEM((2, MOBA_VT_ROWS, tq), F32),
                        pltpu.VMEM((2, MOBA_Q_BLOCKS, blk, tq), F32),
                        pltpu.VMEM((2, MOBA_Q_BLOCKS, blk, tq), F32)],
        compiler_params=_params(3),
        name="moba",
    )(slopes, qkv, qkv, qkv)


def _memory_attention(q, mem_k, mem_v):
    q = q * (HEAD_DIM ** -0.5)
    lane = lax.broadcasted_iota(jnp.int32, q.shape, 1)
    out = jnp.zeros(q.shape, F32)
    for h in range(MEM_HEADS):
        in_head = (lane >= h * HEAD_DIM) & (lane < (h + 1) * HEAD_DIM)
        qh = jnp.where(in_head, q, 0.0).astype(BF16)
        s = lax.dot_general(qh, mem_k, (((1,), (1,)), ((), ())), preferred_element_type=F32)
        m = jnp.max(s, axis=-1, keepdims=True)
        p = jnp.exp(s - m)
        l = jnp.sum(p, axis=-1, keepdims=True)
        pv = jnp.dot(p.astype(BF16), mem_v, preferred_element_type=F32)
        out = jnp.where(in_head, pv / l, out)
    return out


def _project_out(tok, mem_out, wo_ref, x):
    tw = tok.shape[1]
    return (x + jnp.dot(tok.astype(BF16), wo_ref[0:tw, :], preferred_element_type=F32)
            + jnp.dot(mem_out.astype(BF16), wo_ref[tw:, :], preferred_element_type=F32))


def _s5_mix_out_kernel(y_ref, u_ref, d_ref, wglu_ref, bglu_ref, mq_ref, mk_ref, mv_ref, wo_ref, x_ref, o_ref):
    y = y_ref[...] + d_ref[...] * u_ref[...]
    y = 0.5 * y * (1.0 + lax.erf(y * math.sqrt(0.5)))
    z = jnp.dot(y.astype(BF16), wglu_ref[...], preferred_element_type=F32) + bglu_ref[...]
    tok = y * jax.nn.sigmoid(z)
    mem_out = _memory_attention(mq_ref[...], mk_ref[...], mv_ref[...])
    o_ref[...] = _project_out(tok, mem_out, wo_ref, x_ref[...])


def _moba_mix_out_kernel(tok_ref, mq_ref, mk_ref, mv_ref, wo_ref, x_ref, o_ref):
    mem_out = _memory_attention(mq_ref[...], mk_ref[...], mv_ref[...])
    o_ref[...] = _project_out(tok_ref[...], mem_out, wo_ref, x_ref[...])


def _mix_out_specs(n, s, d, tok_w, mem_len, mq_block_col):
    tm = min(ROW_TILE, s)
    per_b = s // tm
    row = lambda i: (i, 0)
    const = lambda i: (0, 0)
    mem_specs = [pl.BlockSpec((tm, MEM_WIDTH), lambda i: (i, mq_block_col)),
                 pl.BlockSpec((mem_len, MEM_WIDTH), lambda i: (i // per_b, 0)),
                 pl.BlockSpec((mem_len, MEM_WIDTH), lambda i: (i // per_b, 1)),
                 pl.BlockSpec((d, d), const),
                 pl.BlockSpec((tm, d), row)]
    return tm, row, const, mem_specs


def _s5_mix_out(y_ssm, proj, d_skip, w_glu, b_glu, mem_kv, w_out, x2d, s, mem_len):
    n, d = x2d.shape
    tok_w = y_ssm.shape[1]
    tm, row, const, mem_specs = _mix_out_specs(n, s, d, tok_w, mem_len, tok_w // MEM_WIDTH)
    return pl.pallas_call(
        _s5_mix_out_kernel,
        grid=(n // tm,),
        in_specs=[pl.BlockSpec((tm, tok_w), row),
                  pl.BlockSpec((tm, tok_w), row),
                  pl.BlockSpec((1, tok_w), const),
                  pl.BlockSpec((tok_w, tok_w), const),
                  pl.BlockSpec((1, tok_w), const)] + mem_specs,
        out_specs=pl.BlockSpec((tm, d), row),
        out_shape=jax.ShapeDtypeStruct((n, d), F32),
        compiler_params=_params(1),
        name="s5_mix_out",
    )(y_ssm, proj, d_skip.reshape(1, tok_w), w_glu, b_glu.reshape(1, tok_w), proj, mem_kv, mem_kv, w_out, x2d)


def _moba_mix_out(tok, proj, mem_kv, w_out, x2d, s, mem_len):
    n, d = x2d.shape
    tok_w = tok.shape[1]
    tm, row, const, mem_specs = _mix_out_specs(n, s, d, tok_w, mem_len, (proj.shape[1] - MEM_WIDTH) // MEM_WIDTH)
    return pl.pallas_call(
        _moba_mix_out_kernel,
        grid=(n // tm,),
        in_specs=[pl.BlockSpec((tm, tok_w), row)] + mem_specs,
        out_specs=pl.BlockSpec((tm, d), row),
        out_shape=jax.ShapeDtypeStruct((n, d), F32),
        compiler_params=_params(1),
        name="moba_mix_out",
    )(tok, proj, mem_kv, mem_kv, w_out, x2d)


FFN_CHUNK = MXU_TILE


def _ffn_kernel(x_ref, g_ref, wg_ref, wu_ref, wd_ref, gf_ref, o_ref, act_ref, *, final_norm):
    x = x_ref[...]
    h = _rms_scale(x, g_ref[...]).astype(BF16)
    hidden = wg_ref.shape[1]
    for c in range(hidden // FFN_CHUNK):
        cols = slice(c * FFN_CHUNK, (c + 1) * FFN_CHUNK)
        gate = jnp.dot(h, wg_ref[:, cols], preferred_element_type=F32)
        up = jnp.dot(h, wu_ref[:, cols], preferred_element_type=F32)
        act_ref[:, cols] = (gate * jax.nn.sigmoid(gate) * up).astype(BF16)
    y = x + jnp.dot(act_ref[...], wd_ref[...], preferred_element_type=F32)
    if final_norm:
        y = _rms_scale(y, gf_ref[...])
    o_ref[...] = y


def _ffn(x2d, g, w_gate, w_up, w_down, g_final, final_norm):
    n, d = x2d.shape
    hidden = w_gate.shape[1]
    tm = min(ROW_TILE, n)
    row = lambda i: (i, 0)
    const = lambda i: (0, 0)
    return pl.pallas_call(
        functools.partial(_ffn_kernel, final_norm=final_norm),
        grid=(n // tm,),
        in_specs=[pl.BlockSpec((tm, d), row),
                  pl.BlockSpec((1, d), const),
                  pl.BlockSpec((d, hidden), const),
                  pl.BlockSpec((d, hidden), const),
                  pl.BlockSpec((hidden, d), const),
                  pl.BlockSpec((1, d), const)],
        out_specs=pl.BlockSpec((tm, d), row),
        out_shape=jax.ShapeDtypeStruct((n, d), F32),
        scratch_shapes=[pltpu.VMEM((tm, hidden), BF16)],
        compiler_params=_params(1),
        name="ffn",
    )(x2d, g.reshape(1, d), w_gate, w_up, w_down, g_final.reshape(1, d))


def kernel(x, mem, mem_norm_g, w_mem_kv, mix_norm_g, s5_w_in, s5_lambda_re, s5_lambda_im, s5_log_dt,
           s5_b_re, s5_b_im, s5_c_re, s5_c_im, s5_d, s5_w_glu, s5_b_glu, moba_w_in, w_out, ffn_norm_g,
           w_gate, w_up, w_down, final_norm_g):
    bsz, s, d = x.shape
    mem_len = mem.shape[1]
    depth = mix_norm_g.shape[0]
    tok_w = d - MEM_WIDTH
    n_heads = tok_w // HEAD_DIM
    slopes = jnp.asarray(2.0 ** (-8.0 * (np.arange(n_heads) + 1) / n_heads), dtype=F32)

    mem_kv = _norm_matmul(mem.reshape(bsz * mem_len, d), mem_norm_g, w_mem_kv.astype(BF16)).astype(BF16)
    x2d = x.reshape(bsz * s, d)
    for i in range(depth):
        j = i // 2
        if i % 2 == 0:
            proj = _norm_matmul(x2d, mix_norm_g[i], s5_w_in[j].astype(BF16))
            ops = _s5_operators(s5_lambda_re[j], s5_lambda_im[j], s5_log_dt[j],
                                s5_b_re[j], s5_b_im[j], s5_c_re[j], s5_c_im[j])
            y_ssm = _s5_ssm(proj, ops, bsz, s, tok_w)
            x2d = _s5_mix_out(y_ssm, proj, s5_d[j], s5_w_glu[j].astype(BF16), s5_b_glu[j],
                              mem_kv, w_out[i].astype(BF16), x2d, s, mem_len)
        else:
            proj = _norm_matmul(x2d, mix_norm_g[i], moba_w_in[j].astype(BF16))
            tok = _moba(proj, slopes, bsz, s, n_heads)
            x2d = _moba_mix_out(tok, proj, mem_kv, w_out[i].astype(BF16), x2d, s, mem_len)
        x2d = _ffn(x2d, ffn_norm_g[i], w_gate[i].astype(BF16), w_up[i].astype(BF16),
                   w_down[i].astype(BF16), final_norm_g, final_norm=(i == depth - 1))
    return x2d.reshape(bsz, s, d)
```

```python
import functools
import math

import numpy as np
import jax
import jax.numpy as jnp
from jax import lax
from jax.experimental import pallas as pl
from jax.experimental.pallas import tpu as pltpu

F32 = jnp.float32
BF16 = jnp.bfloat16

HEAD_DIM = 64
MEM_HEADS = 4
MEM_WIDTH = MEM_HEADS * HEAD_DIM
S5_GROUP_CH = 16
S5_STATE = 64
S5_CHUNK = 16
MOBA_BLOCK = 256
MOBA_TOPK = 3
RMS_EPS = 1e-6
NEG_INF = -1e30

LANES = 128
MXU_TILE = 256
VMEM_LIMIT_BYTES = 56 * 1024 * 1024
ROW_TILE = 512

S5_LANE_GROUPS = LANES // S5_GROUP_CH
S5_STEPS_PER_TILE = MXU_TILE // LANES
S5_LAG_TILES = S5_CHUNK // S5_STEPS_PER_TILE


def _params(n_axes, flags=None):
    return pltpu.CompilerParams(dimension_semantics=("arbitrary",) * n_axes,
                                vmem_limit_bytes=VMEM_LIMIT_BYTES, flags=flags)


def _rms_scale(x, g):
    ms = jnp.mean(x * x, axis=-1, keepdims=True)
    return x * lax.rsqrt(ms + RMS_EPS) * g


def _norm_matmul_kernel(x_ref, g_ref, w_ref, o_ref):
    h = _rms_scale(x_ref[...], g_ref[...])
    o_ref[...] = jnp.dot(h.astype(BF16), w_ref[...], preferred_element_type=F32)


def _norm_matmul(x2d, g, w_bf16):
    n, d = x2d.shape
    f = w_bf16.shape[1]
    tm = min(ROW_TILE, n)
    return pl.pallas_call(
        _norm_matmul_kernel,
        grid=(n // tm,),
        in_specs=[pl.BlockSpec((tm, d), lambda i: (i, 0)),
                  pl.BlockSpec((1, d), lambda i: (0, 0)),
                  pl.BlockSpec((d, f), lambda i: (0, 0))],
        out_specs=pl.BlockSpec((tm, f), lambda i: (i, 0)),
        out_shape=jax.ShapeDtypeStruct((n, f), F32),
        compiler_params=_params(1),
        name="norm_matmul",
    )(x2d, g.reshape(1, d), w_bf16)


def _s5_operators(lam_re, lam_im, log_dt, b_re, b_im, c_re, c_im):
    hi = lax.Precision.HIGHEST
    t_len = S5_CHUNK
    g_cnt, p_cnt = lam_re.shape
    c_cnt = b_re.shape[-1]
    gpb = S5_LANE_GROUPS
    n_lb = g_cnt // gpb
    lr = lam_re.astype(F32)
    li = lam_im.astype(F32)
    dt = jnp.exp(log_dt.astype(F32))[:, None]
    mag = jnp.exp(lr * dt)
    ar = mag * jnp.cos(li * dt)
    ai = mag * jnp.sin(li * dt)
    den = lr * lr + li * li
    nr = ar - 1.0
    fr = (nr * lr + ai * li) / den
    fi = (ai * lr - nr * li) / den
    br = b_re.astype(F32)
    bi = b_im.astype(F32)
    bbr = fr[..., None] * br - fi[..., None] * bi
    bbi = fr[..., None] * bi + fi[..., None] * br
    cr = c_re.astype(F32)
    ci = c_im.astype(F32)

    k = jnp.arange(t_len + 1, dtype=F32)[:, None, None]
    pm = jnp.exp(k * (lr * dt)[None])
    pr = pm * jnp.cos(k * (li * dt)[None])
    pi = pm * jnp.sin(k * (li * dt)[None])

    wr = pr[..., None] * bbr[None] - pi[..., None] * bbi[None]
    wi = pr[..., None] * bbi[None] + pi[..., None] * bbr[None]
    kern = (jnp.einsum('gcp,kgpd->kgcd', cr, wr[:t_len], precision=hi)
            - jnp.einsum('gcp,kgpd->kgcd', ci, wi[:t_len], precision=hi))
    eye = jnp.eye(gpb, dtype=F32)

    spt = S5_STEPS_PER_TILE
    lag = (jnp.arange(S5_LAG_TILES)[:, None, None] * spt
           + jnp.arange(spt)[None, None, :] - jnp.arange(spt)[None, :, None])
    kd = jnp.where((lag >= 0)[..., None, None, None], kern[jnp.clip(lag, 0, t_len - 1)], 0.0)
    kd = kd.reshape(S5_LAG_TILES, spt, spt, n_lb, gpb, c_cnt, c_cnt)
    toep = jnp.einsum('ljtbgoc,gh->bljgctho', kd, eye).reshape(
        n_lb, S5_LAG_TILES, spt * gpb * c_cnt, spt * gpb * c_cnt)

    w_rev = jnp.stack([wr[:t_len][::-1], wi[:t_len][::-1]])
    w_rev = w_rev.reshape(2, t_len, n_lb, gpb, p_cnt, c_cnt)
    p_op = jnp.einsum('rjbgpc,gh->bjgcrhp', w_rev, eye).reshape(
        n_lb, t_len * gpb * c_cnt, 2 * gpb * p_cnt)

    qr = cr[None] * pr[1:, :, None, :] - ci[None] * pi[1:, :, None, :]
    qi = -(cr[None] * pi[1:, :, None, :] + ci[None] * pr[1:, :, None, :])
    q_all = jnp.stack([qr, qi]).reshape(2, t_len, n_lb, gpb, c_cnt, p_cnt)
    q_op = jnp.einsum('rtbgop,gh->brhptgo', q_all, eye).reshape(
        n_lb, 2 * gpb * p_cnt, t_len * gpb * c_cnt)

    coef = jnp.stack([pr[t_len].reshape(n_lb, gpb * p_cnt),
                      pi[t_len].reshape(n_lb, gpb * p_cnt)], axis=1)
    return toep.astype(BF16), p_op.astype(BF16), q_op.astype(BF16), coef


def _s5_ssm_kernel(u_ref, toep_ref, p_ref, q_ref, coef_ref, y_ref, h_scr, s_scr, xp_scr, *, n_chunks):
    t_len = S5_CHUNK
    half = S5_LANE_GROUPS * S5_STATE
    for j in range(t_len):
        h_scr[:, j * LANES:(j + 1) * LANES] = u_ref[pl.ds(j, n_chunks, stride=t_len), :].astype(BF16)
    s_scr[...] = jnp.dot(h_scr[...], p_ref[0], preferred_element_type=F32)
    cr = coef_ref[0, 0:1, :]
    ci = coef_ref[0, 1:2, :]

    def step(n, carry):
        xr, xi = carry
        xp_scr[pl.ds(n, 1), 0:half] = xr
        xp_scr[pl.ds(n, 1), half:2 * half] = xi
        sr = s_scr[pl.ds(n, 1), 0:half]
        si = s_scr[pl.ds(n, 1), half:2 * half]
        return cr * xr - ci * xi + sr, cr * xi + ci * xr + si

    zero = jnp.zeros((1, half), F32)
    lax.fori_loop(0, n_chunks, step, (zero, zero))
    xp = xp_scr[...].astype(BF16)
    for nt in range(S5_LAG_TILES):
        cols = slice(nt * MXU_TILE, (nt + 1) * MXU_TILE)
        acc = jnp.dot(xp, q_ref[0, :, cols], preferred_element_type=F32)
        for kt in range(nt + 1):
            acc = acc + jnp.dot(h_scr[:, kt * MXU_TILE:(kt + 1) * MXU_TILE], toep_ref[0, nt - kt],
                                preferred_element_type=F32)
        for tt in range(S5_STEPS_PER_TILE):
            t = nt * S5_STEPS_PER_TILE + tt
            y_ref[pl.ds(t, n_chunks, stride=t_len), :] = acc[:, tt * LANES:(tt + 1) * LANES]


def _s5_ssm(proj, ops, bsz, s, tok_w):
    toep, p_op, q_op, coef = ops
    n_lb = tok_w // LANES
    n_chunks = s // S5_CHUNK
    state_w = p_op.shape[2]
    flat_w = p_op.shape[1]
    const3 = lambda l, b: (l, 0, 0)
    return pl.pallas_call(
        functools.partial(_s5_ssm_kernel, n_chunks=n_chunks),
        grid=(n_lb, bsz),
        in_specs=[pl.BlockSpec((s, LANES), lambda l, b: (b, l)),
                  pl.BlockSpec((1, S5_LAG_TILES, MXU_TILE, MXU_TILE), lambda l, b: (l, 0, 0, 0)),
                  pl.BlockSpec((1, flat_w, state_w), const3),
                  pl.BlockSpec((1, state_w, flat_w), const3),
                  pl.BlockSpec((1, 2, state_w // 2), const3)],
        out_specs=pl.BlockSpec((s, LANES), lambda l, b: (b, l)),
        out_shape=jax.ShapeDtypeStruct((bsz * s, tok_w), F32),
        scratch_shapes=[pltpu.VMEM((n_chunks, flat_w), BF16),
                        pltpu.VMEM((n_chunks, state_w), F32),
                        pltpu.VMEM((n_chunks, state_w), F32)],
        compiler_params=_params(2),
        name="s5_ssm",
    )(proj, toep, p_op, q_op, coef)


MOBA_Q_BLOCKS = 2
MOBA_VT_ROWS = HEAD_DIM + 16
LOG2E = math.log2(math.e)
MOBA_AUG_HI = 8


def _moba_kernel(slope_ref, q_ref, k_ref, v_ref, o_ref,
                 ka_ref, kb_ref, vt_ref, km_ref, mask_ref, acc_ref, sca_ref, scb_ref, scd_ref, *, n_blocks):
    blk = MOBA_BLOCK
    nq = MOBA_Q_BLOCKS
    tq = nq * blk
    hp = pl.program_id(1)
    qi = pl.program_id(2)
    slopes = (slope_ref[2 * hp] * LOG2E, slope_ref[2 * hp + 1] * LOG2E)
    k_refs = (ka_ref, kb_ref)

    @pl.when(qi == 0)
    def _prepare():
        lane = lax.broadcasted_iota(jnp.int32, (blk, LANES), 1)
        key_off = lax.broadcasted_iota(jnp.int32, (blk, LANES), 0).astype(F32)
        ones_row = (lax.broadcasted_iota(jnp.int32, (MOBA_VT_ROWS - HEAD_DIM, blk), 0) == 0).astype(BF16)
        in_heads = (lane < HEAD_DIM, lane >= HEAD_DIM)
        aug_lane = (lane - HEAD_DIM, lane)
        key_bias = []
        for e in range(2):
            a = slopes[e] * key_off
            a_hi = a.astype(BF16).astype(F32)
            key_bias.append(jnp.where(aug_lane[e] == 0, a_hi, jnp.where(aug_lane[e] == 1, a - a_hi, 0.0)))
        for j in range(n_blocks):
            kj = k_ref[j * blk:(j + 1) * blk, :]
            for e in range(2):
                one_hot = (aug_lane[e] == MOBA_AUG_HI + j) | (aug_lane[e] == MOBA_AUG_HI + n_blocks + j)
                aug = jnp.where(one_hot, 1.0, key_bias[e])
                k_refs[e][j] = jnp.where(in_heads[e], kj, aug).astype(BF16)
            v_t = v_ref[j * blk:(j + 1) * blk, :].T
            for e in range(2):
                vt_ref[j, e, 0:HEAD_DIM, :] = v_t[e * HEAD_DIM:(e + 1) * HEAD_DIM, :].astype(BF16)
                vt_ref[j, e, HEAD_DIM:MOBA_VT_ROWS, :] = ones_row
            km_ref[j:j + 1, :] = jnp.mean(kj, axis=0, keepdims=True)
        d = (lax.broadcasted_iota(jnp.int32, (blk, blk), 1)
             - lax.broadcasted_iota(jnp.int32, (blk, blk), 0))
        causal = jnp.where(d >= 0, 0.0, NEG_INF)
        free = jnp.zeros((blk, blk), F32)
        mask_ref[0] = jnp.concatenate([causal, free], axis=1)
        mask_ref[1] = jnp.concatenate([free, causal], axis=1)

    q_t = (q_ref[...] * (HEAD_DIM ** -0.5 * LOG2E)).T

    km = km_ref[...]
    lane_m = lax.broadcasted_iota(jnp.int32, km.shape, 1)
    blk_id = lax.broadcasted_iota(jnp.int32, (n_blocks, tq), 0)
    q_pos = lax.broadcasted_iota(jnp.int32, (n_blocks, tq), 1)
    q_blk = nq * qi + q_pos // blk
    q_off = (q_pos % blk).astype(F32)
    past = blk_id < q_blk
    q_bf = q_t.astype(BF16)
    ones_rows = (lax.broadcasted_iota(jnp.int32, (MOBA_AUG_HI, tq), 0) < 2).astype(F32)
    pad_rows = jnp.zeros((HEAD_DIM - MOBA_AUG_HI - 2 * n_blocks, tq), F32)
    q_aug = []
    for e in range(2):
        in_head = (lane_m < HEAD_DIM) if e == 0 else (lane_m >= HEAD_DIM)
        km_e = jnp.where(in_head, km, 0.0).astype(BF16)
        gate = jnp.dot(km_e, q_bf, preferred_element_type=F32)
        gate = jnp.where(past, gate, NEG_INF)
        rank = jnp.zeros((n_blocks, tq), jnp.int32)
        for jp in range(n_blocks):
            gj = gate[jp:jp + 1, :]
            beats = (gj > gate) | ((gj == gate) & (jp < blk_id))
            rank = rank + beats.astype(jnp.int32)
        visible = (past & (rank < MOBA_TOPK)) | (blk_id == q_blk)
        bias = -slopes[e] * (((q_blk - blk_id) * blk).astype(F32) + q_off)
        bias = jnp.where(visible, bias, NEG_INF)
        bias_hi = bias.astype(BF16).astype(F32)
        aug = jnp.concatenate([ones_rows, bias_hi, bias - bias_hi, pad_rows], axis=0)
        q_head = q_t[e * HEAD_DIM:(e + 1) * HEAD_DIM, :]
        rows = [q_head, aug] if e == 0 else [aug, q_head]
        q_aug.append(jnp.concatenate(rows, axis=0).astype(BF16))

    def score_chunk(c, sc_ref, on_diag):
        maxes = []
        for e in range(2):
            m_blk = None
            for w in range(nq):
                s_t = jnp.dot(k_refs[e][nq * c + w], q_aug[e], preferred_element_type=F32)
                if on_diag:
                    s_t = s_t + mask_ref[w]
                sc_ref[e, w] = s_t
                mx = jnp.max(s_t, axis=0, keepdims=True)
                m_blk = mx if m_blk is None else jnp.maximum(m_blk, mx)
            maxes.append(m_blk)
        return tuple(maxes)

    def accumulate_chunk(c, sc_ref, m_run, m_blk):
        m_next = []
        for e in range(2):
            m_new = jnp.maximum(m_run[e], m_blk[e])
            pv = None
            for w in range(nq):
                p = jnp.exp2(sc_ref[e, w] - m_new).astype(BF16)
                d = jnp.dot(vt_ref[nq * c + w, e], p, preferred_element_type=F32)
                pv = d if pv is None else pv + d
            acc_ref[e] = jnp.exp2(m_run[e] - m_new) * acc_ref[e] + pv
            m_next.append(m_new)
        return tuple(m_next)

    acc_ref[...] = jnp.zeros_like(acc_ref)
    m_init = jnp.full((1, tq), NEG_INF, F32)
    m_diag = score_chunk(qi, scd_ref, True)
    m_first = score_chunk(0, sca_ref, False)
    m_run = accumulate_chunk(qi, scd_ref, (m_init, m_init), m_diag)
    last = jnp.maximum(qi - 1, 0)

    def body(t, carry):
        m_run, m_blk = carry
        c = 2 * t
        m_blk_b = score_chunk(c + 1, scb_ref, False)
        m_run = accumulate_chunk(c, sca_ref, m_run, m_blk)
        m_blk_a = score_chunk(jnp.minimum(c + 2, last), sca_ref, False)
        m_run = accumulate_chunk(c + 1, scb_ref, m_run, m_blk_b)
        return m_run, m_blk_a

    m_run, m_blk = lax.fori_loop(0, qi // 2, body, (m_run, m_first))

    @pl.when(qi % 2 == 1)
    def _odd_chunk():
        accumulate_chunk(qi - 1, sca_ref, m_run, m_blk)

    out_t = jnp.concatenate([acc_ref[e, 0:HEAD_DIM, :] / acc_ref[e, HEAD_DIM:HEAD_DIM + 1, :]
                             for e in range(2)], axis=0)
    o_ref[...] = out_t.T


def _moba(qkv, slopes, bsz, s, n_heads):
    n_blocks = s // MOBA_BLOCK
    n_pairs = n_heads // 2
    blk = MOBA_BLOCK
    tq = MOBA_Q_BLOCKS * blk
    n_tiles = s // tq
    return pl.pallas_call(
        functools.partial(_moba_kernel, n_blocks=n_blocks),
        grid=(bsz, n_pairs, n_tiles),
        in_specs=[pl.BlockSpec(memory_space=pltpu.SMEM),
                  pl.BlockSpec((tq, LANES), lambda b, h, i: (b * n_tiles + i, h)),
                  pl.BlockSpec((s, LANES), lambda b, h, i: (b, n_pairs + h)),
                  pl.BlockSpec((s, LANES), lambda b, h, i: (b, 2 * n_pairs + h))],
        out_specs=pl.BlockSpec((tq, LANES), lambda b, h, i: (b * n_tiles + i, h)),
        out_shape=jax.ShapeDtypeStruct((bsz * s, n_heads * HEAD_DIM), F32),
        scratch_shapes=[pltpu.VMEM((n_blocks, blk, LANES), BF16),
                        pltpu.VMEM((n_blocks, blk, LANES), BF16),
                        pltpu.VMEM((n_blocks, 2, MOBA_VT_ROWS, blk), BF16),
                        pltpu.VMEM((n_blocks, LANES), F32),
                        pltpu.VMEM((MOBA_Q_BLOCKS, blk, tq), F32),
                        pltpu.VMEM((2, MOBA_VT_ROWS, tq), F32),
                        pltpu.VMEM((2, MOBA_Q_BLOCKS, blk, tq), F32),
                        pltpu.VMEM((2, MOBA_Q_BLOCKS, blk, tq), F32),
                        pltpu.VMEM((2, MOBA_Q_BLOCKS, blk, tq), F32)],
        compiler_params=_params(3),
        name="moba",
    )(slopes, qkv, qkv, qkv)


def _memory_attention(q, mem_k, mem_v):
    q = q * (HEAD_DIM ** -0.5)
    lane = lax.broadcasted_iota(jnp.int32, q.shape, 1)
    out = jnp.zeros(q.shape, F32)
    for h in range(MEM_HEADS):
        in_head = (lane >= h * HEAD_DIM) & (lane < (h + 1) * HEAD_DIM)
        qh = jnp.where(in_head, q, 0.0).astype(BF16)
        s = lax.dot_general(qh, mem_k, (((1,), (1,)), ((), ())), preferred_element_type=F32)
        m = jnp.max(s, axis=-1, keepdims=True)
        p = jnp.exp(s - m)
        l = jnp.sum(p, axis=-1, keepdims=True)
        pv = jnp.dot(p.astype(BF16), mem_v, preferred_element_type=F32)
        out = jnp.where(in_head, pv / l, out)
    return out


def _project_out(tok, mem_out, wo_ref, x):
    tw = tok.shape[1]
    return (x + jnp.dot(tok.astype(BF16), wo_ref[0:tw, :], preferred_element_type=F32)
            + jnp.dot(mem_out.astype(BF16), wo_ref[tw:, :], preferred_element_type=F32))


def _s5_mix_out_kernel(y_ref, u_ref, d_ref, wglu_ref, bglu_ref, mq_ref, mk_ref, mv_ref, wo_ref, x_ref, o_ref):
    y = y_ref[...] + d_ref[...] * u_ref[...]
    y = 0.5 * y * (1.0 + lax.erf(y * math.sqrt(0.5)))
    z = jnp.dot(y.astype(BF16), wglu_ref[...], preferred_element_type=F32) + bglu_ref[...]
    tok = y * jax.nn.sigmoid(z)
    mem_out = _memory_attention(mq_ref[...], mk_ref[...], mv_ref[...])
    o_ref[...] = _project_out(tok, mem_out, wo_ref, x_ref[...])


def _moba_mix_out_kernel(tok_ref, mq_ref, mk_ref, mv_ref, wo_ref, x_ref, o_ref):
    mem_out = _memory_attention(mq_ref[...], mk_ref[...], mv_ref[...])
    o_ref[...] = _project_out(tok_ref[...], mem_out, wo_ref, x_ref[...])


def _mix_out_specs(n, s, d, tok_w, mem_len, mq_block_col):
    tm = min(ROW_TILE, s)
    per_b = s // tm
    row = lambda i: (i, 0)
    const = lambda i: (0, 0)
    mem_specs = [pl.BlockSpec((tm, MEM_WIDTH), lambda i: (i, mq_block_col)),
                 pl.BlockSpec((mem_len, MEM_WIDTH), lambda i: (i // per_b, 0)),
                 pl.BlockSpec((mem_len, MEM_WIDTH), lambda i: (i // per_b, 1)),
                 pl.BlockSpec((d, d), const),
                 pl.BlockSpec((tm, d), row)]
    return tm, row, const, mem_specs


def _s5_mix_out(y_ssm, proj, d_skip, w_glu, b_glu, mem_kv, w_out, x2d, s, mem_len):
    n, d = x2d.shape
    tok_w = y_ssm.shape[1]
    tm, row, const, mem_specs = _mix_out_specs(n, s, d, tok_w, mem_len, tok_w // MEM_WIDTH)
    return pl.pallas_call(
        _s5_mix_out_kernel,
        grid=(n // tm,),
        in_specs=[pl.BlockSpec((tm, tok_w), row),
                  pl.BlockSpec((tm, tok_w), row),
                  pl.BlockSpec((1, tok_w), const),
                  pl.BlockSpec((tok_w, tok_w), const),
                  pl.BlockSpec((1, tok_w), const)] + mem_specs,
        out_specs=pl.BlockSpec((tm, d), row),
        out_shape=jax.ShapeDtypeStruct((n, d), F32),
        compiler_params=_params(1),
        name="s5_mix_out",
    )(y_ssm, proj, d_skip.reshape(1, tok_w), w_glu, b_glu.reshape(1, tok_w), proj, mem_kv, mem_kv, w_out, x2d)


def _moba_mix_out(tok, proj, mem_kv, w_out, x2d, s, mem_len):
    n, d = x2d.shape
    tok_w = tok.shape[1]
    tm, row, const, mem_specs = _mix_out_specs(n, s, d, tok_w, mem_len, (proj.shape[1] - MEM_WIDTH) // MEM_WIDTH)
    return pl.pallas_call(
        _moba_mix_out_kernel,
        grid=(n // tm,),
        in_specs=[pl.BlockSpec((tm, tok_w), row)] + mem_specs,
        out_specs=pl.BlockSpec((tm, d), row),
        out_shape=jax.ShapeDtypeStruct((n, d), F32),
        compiler_params=_params(1),
        name="moba_mix_out",
    )(tok, proj, mem_kv, mem_kv, w_out, x2d)


FFN_CHUNK = MXU_TILE


def _ffn_kernel(x_ref, g_ref, wg_ref, wu_ref, wd_ref, gf_ref, o_ref, act_ref, *, final_norm):
    x = x_ref[...]
    h = _rms_scale(x, g_ref[...]).astype(BF16)
    hidden = wg_ref.shape[1]
    for c in range(hidden // FFN_CHUNK):
        cols = slice(c * FFN_CHUNK, (c + 1) * FFN_CHUNK)
        gate = jnp.dot(h, wg_ref[:, cols], preferred_element_type=F32)
        up = jnp.dot(h, wu_ref[:, cols], preferred_element_type=F32)
        act_ref[:, cols] = (gate * jax.nn.sigmoid(gate) * up).astype(BF16)
    y = x + jnp.dot(act_ref[...], wd_ref[...], preferred_element_type=F32)
    if final_norm:
        y = _rms_scale(y, gf_ref[...])
    o_ref[...] = y


def _ffn(x2d, g, w_gate, w_up, w_down, g_final, final_norm):
    n, d = x2d.shape
    hidden = w_gate.shape[1]
    tm = min(ROW_TILE, n)
    row = lambda i: (i, 0)
    const = lambda i: (0, 0)
    return pl.pallas_call(
        functools.partial(_ffn_kernel, final_norm=final_norm),
        grid=(n // tm,),
        in_specs=[pl.BlockSpec((tm, d), row),
                  pl.BlockSpec((1, d), const),
                  pl.BlockSpec((d, hidden), const),
                  pl.BlockSpec((d, hidden), const),
                  pl.BlockSpec((hidden, d), const),
                  pl.BlockSpec((1, d), const)],
        out_specs=pl.BlockSpec((tm, d), row),
        out_shape=jax.ShapeDtypeStruct((n, d), F32),
        scratch_shapes=[pltpu.VMEM((tm, hidden), BF16)],
        compiler_params=_params(1),
        name="ffn",
    )(x2d, g.reshape(1, d), w_gate, w_up, w_down, g_final.reshape(1, d))


def kernel(x, mem, mem_norm_g, w_mem_kv, mix_norm_g, s5_w_in, s5_lambda_re, s5_lambda_im, s5_log_dt,
           s5_b_re, s5_b_im, s5_c_re, s5_c_im, s5_d, s5_w_glu, s5_b_glu, moba_w_in, w_out, ffn_norm_g,
           w_gate, w_up, w_down, final_norm_g):
    bsz, s, d = x.shape
    mem_len = mem.shape[1]
    depth = mix_norm_g.shape[0]
    tok_w = d - MEM_WIDTH
    n_heads = tok_w // HEAD_DIM
    slopes = jnp.asarray(2.0 ** (-8.0 * (np.arange(n_heads) + 1) / n_heads), dtype=F32)

    mem_kv = _norm_matmul(mem.reshape(bsz * mem_len, d), mem_norm_g, w_mem_kv.astype(BF16)).astype(BF16)
    x2d = x.reshape(bsz * s, d)
    for i in range(depth):
        j = i // 2
        if i % 2 == 0:
            proj = _norm_matmul(x2d, mix_norm_g[i], s5_w_in[j].astype(BF16))
            ops = _s5_operators(s5_lambda_re[j], s5_lambda_im[j], s5_log_dt[j],
                                s5_b_re[j], s5_b_im[j], s5_c_re[j], s5_c_im[j])
            y_ssm = _s5_ssm(proj, ops, bsz, s, tok_w)
            x2d = _s5_mix_out(y_ssm, proj, s5_d[j], s5_w_glu[j].astype(BF16), s5_b_glu[j],
                              mem_kv, w_out[i].astype(BF16), x2d, s, mem_len)
        else:
            proj = _norm_matmul(x2d, mix_norm_g[i], moba_w_in[j].astype(BF16))
            tok = _moba(proj, slopes, bsz, s, n_heads)
            x2d = _moba_mix_out(tok, proj, mem_kv, w_out[i].astype(BF16), x2d, s, mem_len)
        x2d = _ffn(x2d, ffn_norm_g[i], w_gate[i].astype(BF16), w_up[i].astype(BF16),
                   w_down[i].astype(BF16), final_norm_g, final_norm=(i == depth - 1))
    return x2d.reshape(bsz, s, d)
```

```python
import functools
import math

import numpy as np
import jax
import jax.numpy as jnp
from jax import lax
from jax.experimental import pallas as pl
from jax.experimental.pallas import tpu as pltpu

F32 = jnp.float32
BF16 = jnp.bfloat16

HEAD_DIM = 64
MEM_HEADS = 4
MEM_WIDTH = MEM_HEADS * HEAD_DIM
S5_GROUP_CH = 16
S5_STATE = 64
S5_CHUNK = 16
MOBA_BLOCK = 256
MOBA_TOPK = 3
RMS_EPS = 1e-6
NEG_INF = -1e30

LANES = 128
MXU_TILE = 256
VMEM_LIMIT_BYTES = 56 * 1024 * 1024
ROW_TILE = 512

S5_LANE_GROUPS = LANES // S5_GROUP_CH
S5_STEPS_PER_TILE = MXU_TILE // LANES
S5_LAG_TILES = S5_CHUNK // S5_STEPS_PER_TILE


def _params(n_axes, flags=None):
    return pltpu.CompilerParams(dimension_semantics=("arbitrary",) * n_axes,
                                vmem_limit_bytes=VMEM_LIMIT_BYTES, flags=flags)


def _rms_scale(x, g):
    ms = jnp.mean(x * x, axis=-1, keepdims=True)
    return x * lax.rsqrt(ms + RMS_EPS) * g


def _norm_matmul_kernel(x_ref, g_ref, w_ref, o_ref):
    h = _rms_scale(x_ref[...], g_ref[...])
    o_ref[...] = jnp.dot(h.astype(BF16), w_ref[...], preferred_element_type=F32)


def _norm_matmul(x2d, g, w_bf16):
    n, d = x2d.shape
    f = w_bf16.shape[1]
    tm = min(ROW_TILE, n)
    return pl.pallas_call(
        _norm_matmul_kernel,
        grid=(n // tm,),
        in_specs=[pl.BlockSpec((tm, d), lambda i: (i, 0)),
                  pl.BlockSpec((1, d), lambda i: (0, 0)),
                  pl.BlockSpec((d, f), lambda i: (0, 0))],
        out_specs=pl.BlockSpec((tm, f), lambda i: (i, 0)),
        out_shape=jax.ShapeDtypeStruct((n, f), F32),
        compiler_params=_params(1),
        name="norm_matmul",
    )(x2d, g.reshape(1, d), w_bf16)


def _s5_operators(lam_re, lam_im, log_dt, b_re, b_im, c_re, c_im):
    hi = lax.Precision.HIGHEST
    t_len = S5_CHUNK
    g_cnt, p_cnt = lam_re.shape
    c_cnt = b_re.shape[-1]
    gpb = S5_LANE_GROUPS
    n_lb = g_cnt // gpb
    spt = S5_STEPS_PER_TILE
    lr = lam_re.astype(F32)
    li = lam_im.astype(F32)
    dt = jnp.exp(log_dt.astype(F32))[:, None]
    mag = jnp.exp(lr * dt)
    ar = mag * jnp.cos(li * dt)
    ai = mag * jnp.sin(li * dt)
    den = lr * lr + li * li
    nr = ar - 1.0
    fr = (nr * lr + ai * li) / den
    fi = (ai * lr - nr * li) / den
    br = jnp.swapaxes(b_re.astype(F32), 1, 2)
    bi = jnp.swapaxes(b_im.astype(F32), 1, 2)
    bbr = fr[:, None, :] * br - fi[:, None, :] * bi
    bbi = fr[:, None, :] * bi + fi[:, None, :] * br
    cr = c_re.astype(F32)
    ci = c_im.astype(F32)

    k = jnp.arange(t_len + 1, dtype=F32)[:, None, None]
    pm = jnp.exp(k * (lr * dt)[None])
    pr = pm * jnp.cos(k * (li * dt)[None])
    pi = pm * jnp.sin(k * (li * dt)[None])

    wr = pr[:, :, None, :] * bbr[None] - pi[:, :, None, :] * bbi[None]
    wi = pr[:, :, None, :] * bbi[None] + pi[:, :, None, :] * bbr[None]
    kern = (jnp.einsum('gop,kgcp->kgoc', cr, wr[:t_len], precision=hi)
            - jnp.einsum('gop,kgcp->kgoc', ci, wi[:t_len], precision=hi))
    lag = (jnp.arange(S5_LAG_TILES)[:, None, None] * spt
           + jnp.arange(spt)[None, None, :] - jnp.arange(spt)[None, :, None])
    kd = jnp.where((lag >= 0)[..., None, None, None], kern[jnp.clip(lag, 0, t_len - 1)], 0.0)
    kd = kd.reshape(S5_LAG_TILES, spt, spt, n_lb, gpb, c_cnt, c_cnt)
    lag_t = kd.transpose(3, 0, 2, 5, 1, 4, 6).reshape(n_lb, S5_LAG_TILES, spt * c_cnt, spt * gpb * c_cnt)

    w_rev = jnp.concatenate([wr[:t_len][::-1], wi[:t_len][::-1]], axis=-1)
    p_c = w_rev.reshape(t_len, n_lb, gpb, c_cnt, 2 * p_cnt).transpose(1, 0, 2, 3, 4).reshape(
        n_lb, t_len * gpb * c_cnt, 2 * p_cnt)

    qr = cr[None] * pr[1:, :, None, :] - ci[None] * pi[1:, :, None, :]
    qi = -(cr[None] * pi[1:, :, None, :] + ci[None] * pr[1:, :, None, :])
    q_c = jnp.stack([qr, qi]).reshape(2, t_len, n_lb, gpb, c_cnt, p_cnt).transpose(2, 0, 5, 1, 3, 4).reshape(
        n_lb, 2 * p_cnt, t_len * gpb * c_cnt)

    coef = jnp.stack([pr[t_len].reshape(n_lb, gpb * p_cnt),
                      pi[t_len].reshape(n_lb, gpb * p_cnt)], axis=1)
    return lag_t.astype(BF16), p_c.astype(BF16), q_c.astype(BF16), coef


def _iota(shape, dim):
    return lax.broadcasted_iota(jnp.int32, shape, dim)


def _s5_ssm_kernel(u_ref, lag_ref, pc_ref, qc_ref, coef_ref, y_ref,
                   toep_scr, p_scr, q_scr, h_scr, s_scr, xp_scr, *, n_chunks):
    t_len = S5_CHUNK
    gpb = S5_LANE_GROUPS
    st = S5_STATE
    cc = S5_GROUP_CH
    half = gpb * st
    flat_w = t_len * LANES
    nt_dims = (((1,), (1,)), ((), ()))

    @pl.when(pl.program_id(1) == 0)
    def _expand_operators():
        def group_diag(x, row_unit, col_unit):
            same = ((_iota(x.shape, 0) // row_unit) % gpb) == ((_iota(x.shape, 1) // col_unit) % gpb)
            return jnp.where(same, x, 0.0).astype(BF16)

        shape = (2 * st, 2 * half)
        rep = ((_iota(shape, 0) // st == _iota(shape, 1) // half)
               & (_iota(shape, 0) % st == _iota(shape, 1) % st)).astype(BF16)
        for r in range(flat_w // MXU_TILE):
            rows = slice(r * MXU_TILE, (r + 1) * MXU_TILE)
            p_scr[rows, :] = group_diag(jnp.dot(pc_ref[0, rows, :], rep, preferred_element_type=F32), cc, st)
        shape = (2 * half, 2 * st)
        rep = ((_iota(shape, 0) // half == _iota(shape, 1) // st)
               & (_iota(shape, 0) % st == _iota(shape, 1) % st)).astype(BF16)
        for c in range(flat_w // MXU_TILE):
            cols = slice(c * MXU_TILE, (c + 1) * MXU_TILE)
            q_scr[:, cols] = group_diag(jnp.dot(rep, qc_ref[0, :, cols], preferred_element_type=F32), st, cc)
        shape = (MXU_TILE, S5_STEPS_PER_TILE * cc)
        rep = ((_iota(shape, 0) // LANES == _iota(shape, 1) // cc)
               & (_iota(shape, 0) % cc == _iota(shape, 1) % cc)).astype(BF16)
        for l in range(S5_LAG_TILES):
            toep_scr[l] = group_diag(jnp.dot(rep, lag_ref[0, l], preferred_element_type=F32), cc, cc)

    for j in range(t_len):
        h_scr[:, j * LANES:(j + 1) * LANES] = u_ref[pl.ds(j, n_chunks, stride=t_len), :].astype(BF16)
    s_scr[...] = jnp.dot(h_scr[...], p_scr[...], preferred_element_type=F32)
    cr = coef_ref[0, 0:1, :]
    ci = coef_ref[0, 1:2, :]

    def step(n, carry):
        xr, xi = carry
        xp_scr[pl.ds(n, 1), 0:half] = xr
        xp_scr[pl.ds(n, 1), half:2 * half] = xi
        sr = s_scr[pl.ds(n, 1), 0:half]
        si = s_scr[pl.ds(n, 1), half:2 * half]
        return cr * xr - ci * xi + sr, cr * xi + ci * xr + si

    zero = jnp.zeros((1, half), F32)
    lax.fori_loop(0, n_chunks, step, (zero, zero))
    xp = xp_scr[...].astype(BF16)
    for nt in range(S5_LAG_TILES):
        cols = slice(nt * MXU_TILE, (nt + 1) * MXU_TILE)
        acc = jnp.dot(xp, q_scr[:, cols], preferred_element_type=F32)
        for kt in range(nt + 1):
            acc = acc + lax.dot_general(h_scr[:, kt * MXU_TILE:(kt + 1) * MXU_TILE], toep_scr[nt - kt],
                                        nt_dims, preferred_element_type=F32)
        for tt in range(S5_STEPS_PER_TILE):
            t = nt * S5_STEPS_PER_TILE + tt
            y_ref[pl.ds(t, n_chunks, stride=t_len), :] = acc[:, tt * LANES:(tt + 1) * LANES]


def _s5_ssm(proj, ops, bsz, s, tok_w):
    lag_t, p_c, q_c, coef = ops
    n_lb = tok_w // LANES
    n_chunks = s // S5_CHUNK
    flat_w = p_c.shape[1]
    state_w = S5_LANE_GROUPS * p_c.shape[2]
    const3 = lambda l, b: (l, 0, 0)
    return pl.pallas_call(
        functools.partial(_s5_ssm_kernel, n_chunks=n_chunks),
        grid=(n_lb, bsz),
        in_specs=[pl.BlockSpec((s, LANES), lambda l, b: (b, l)),
                  pl.BlockSpec((1,) + lag_t.shape[1:], lambda l, b: (l, 0, 0, 0)),
                  pl.BlockSpec((1,) + p_c.shape[1:], const3),
                  pl.BlockSpec((1,) + q_c.shape[1:], const3),
                  pl.BlockSpec((1,) + coef.shape[1:], const3)],
        out_specs=pl.BlockSpec((s, LANES), lambda l, b: (b, l)),
        out_shape=jax.ShapeDtypeStruct((bsz * s, tok_w), F32),
        scratch_shapes=[pltpu.VMEM((S5_LAG_TILES, MXU_TILE, MXU_TILE), BF16),
                        pltpu.VMEM((flat_w, state_w), BF16),
                        pltpu.VMEM((state_w, flat_w), BF16),
                        pltpu.VMEM((n_chunks, flat_w), BF16),
                        pltpu.VMEM((n_chunks, state_w), F32),
                        pltpu.VMEM((n_chunks, state_w), F32)],
        compiler_params=_params(2),
        name="s5_ssm",
    )(proj, lag_t, p_c, q_c, coef)


MOBA_Q_BLOCKS = 2
MOBA_VT_ROWS = HEAD_DIM + 16
LOG2E = math.log2(math.e)
MOBA_AUG_HI = 8


def _moba_kernel(slope_ref, q_ref, k_ref, v_ref, o_ref,
                 ka_ref, kb_ref, vt_ref, km_ref, mask_ref, acc_ref, sca_ref, scb_ref, scd_ref, *, n_blocks):
    blk = MOBA_BLOCK
    nq = MOBA_Q_BLOCKS
    tq = nq * blk
    hp = pl.program_id(1)
    qi = pl.program_id(2)
    slopes = (slope_ref[2 * hp] * LOG2E, slope_ref[2 * hp + 1] * LOG2E)
    k_refs = (ka_ref, kb_ref)

    @pl.when(qi == 0)
    def _prepare():
        lane = lax.broadcasted_iota(jnp.int32, (blk, LANES), 1)
        key_off = lax.broadcasted_iota(jnp.int32, (blk, LANES), 0).astype(F32)
        ones_row = (lax.broadcasted_iota(jnp.int32, (MOBA_VT_ROWS - HEAD_DIM, blk), 0) == 0).astype(BF16)
        in_heads = (lane < HEAD_DIM, lane >= HEAD_DIM)
        aug_lane = (lane - HEAD_DIM, lane)
        key_bias = []
        for e in range(2):
            a = slopes[e] * key_off
            a_hi = a.astype(BF16).astype(F32)
            key_bias.append(jnp.where(aug_lane[e] == 0, a_hi, jnp.where(aug_lane[e] == 1, a - a_hi, 0.0)))
        for j in range(n_blocks):
            kj = k_ref[j * blk:(j + 1) * blk, :]
            for e in range(2):
                one_hot = (aug_lane[e] == MOBA_AUG_HI + j) | (aug_lane[e] == MOBA_AUG_HI + n_blocks + j)
                aug = jnp.where(one_hot, 1.0, key_bias[e])
                k_refs[e][j] = jnp.where(in_heads[e], kj, aug).astype(BF16)
            v_t = v_ref[j * blk:(j + 1) * blk, :].T
            for e in range(2):
                vt_ref[j, e, 0:HEAD_DIM, :] = v_t[e * HEAD_DIM:(e + 1) * HEAD_DIM, :].astype(BF16)
                vt_ref[j, e, HEAD_DIM:MOBA_VT_ROWS, :] = ones_row
            km_ref[j:j + 1, :] = jnp.mean(kj, axis=0, keepdims=True)
        d = (lax.broadcasted_iota(jnp.int32, (blk, blk), 1)
             - lax.broadcasted_iota(jnp.int32, (blk, blk), 0))
        causal = jnp.where(d >= 0, 0.0, NEG_INF)
        free = jnp.zeros((blk, blk), F32)
        mask_ref[0] = jnp.concatenate([causal, free], axis=1)
        mask_ref[1] = jnp.concatenate([free, causal], axis=1)

    q_t = (q_ref[...] * (HEAD_DIM ** -0.5 * LOG2E)).T

    km = km_ref[...]
    lane_m = lax.broadcasted_iota(jnp.int32, km.shape, 1)
    blk_id = lax.broadcasted_iota(jnp.int32, (n_blocks, tq), 0)
    q_pos = lax.broadcasted_iota(jnp.int32, (n_blocks, tq), 1)
    q_blk = nq * qi + q_pos // blk
    q_off = (q_pos % blk).astype(F32)
    past = blk_id < q_blk
    q_bf = q_t.astype(BF16)
    ones_rows = (lax.broadcasted_iota(jnp.int32, (MOBA_AUG_HI, tq), 0) < 2).astype(F32)
    pad_rows = jnp.zeros((HEAD_DIM - MOBA_AUG_HI - 2 * n_blocks, tq), F32)
    q_aug = []
    for e in range(2):
        in_head = (lane_m < HEAD_DIM) if e == 0 else (lane_m >= HEAD_DIM)
        km_e = jnp.where(in_head, km, 0.0).astype(BF16)
        gate = jnp.dot(km_e, q_bf, preferred_element_type=F32)
        gate = jnp.where(past, gate, NEG_INF)
        rank = jnp.zeros((n_blocks, tq), jnp.int32)
        for jp in range(n_blocks):
            gj = gate[jp:jp + 1, :]
            beats = (gj > gate) | ((gj == gate) & (jp < blk_id))
            rank = rank + beats.astype(jnp.int32)
        visible = (past & (rank < MOBA_TOPK)) | (blk_id == q_blk)
        bias = -slopes[e] * (((q_blk - blk_id) * blk).astype(F32) + q_off)
        bias = jnp.where(visible, bias, NEG_INF)
        bias_hi = bias.astype(BF16).astype(F32)
        aug = jnp.concatenate([ones_rows, bias_hi, bias - bias_hi, pad_rows], axis=0)
        q_head = q_t[e * HEAD_DIM:(e + 1) * HEAD_DIM, :]
        rows = [q_head, aug] if e == 0 else [aug, q_head]
        q_aug.append(jnp.concatenate(rows, axis=0).astype(BF16))

    def score_chunk(c, sc_ref, on_diag):
        maxes = []
        for e in range(2):
            m_blk = None
            for w in range(nq):
                s_t = jnp.dot(k_refs[e][nq * c + w], q_aug[e], preferred_element_type=F32)
                if on_diag:
                    s_t = s_t + mask_ref[w]
                sc_ref[e, w] = s_t
                mx = jnp.max(s_t, axis=0, keepdims=True)
                m_blk = mx if m_blk is None else jnp.maximum(m_blk, mx)
            maxes.append(m_blk)
        return tuple(maxes)

    def accumulate_chunk(c, sc_ref, m_run, m_blk):
        m_next = []
        for e in range(2):
            m_new = jnp.maximum(m_run[e], m_blk[e])
            pv = None
            for w in range(nq):
                p = jnp.exp2(sc_ref[e, w] - m_new).astype(BF16)
                d = jnp.dot(vt_ref[nq * c + w, e], p, preferred_element_type=F32)
                pv = d if pv is None else pv + d
            acc_ref[e] = jnp.exp2(m_run[e] - m_new) * acc_ref[e] + pv
            m_next.append(m_new)
        return tuple(m_next)

    acc_ref[...] = jnp.zeros_like(acc_ref)
    m_init = jnp.full((1, tq), NEG_INF, F32)
    m_diag = score_chunk(qi, scd_ref, True)
    m_first = score_chunk(0, sca_ref, False)
    m_run = accumulate_chunk(qi, scd_ref, (m_init, m_init), m_diag)
    last = jnp.maximum(qi - 1, 0)

    def body(t, carry):
        m_run, m_blk = carry
        c = 2 * t
        m_blk_b = score_chunk(c + 1, scb_ref, False)
        m_run = accumulate_chunk(c, sca_ref, m_run, m_blk)
        m_blk_a = score_chunk(jnp.minimum(c + 2, last), sca_ref, False)
        m_run = accumulate_chunk(c + 1, scb_ref, m_run, m_blk_b)
        return m_run, m_blk_a

    m_run, m_blk = lax.fori_loop(0, qi // 2, body, (m_run, m_first))

    @pl.when(qi % 2 == 1)
    def _odd_chunk():
        accumulate_chunk(qi - 1, sca_ref, m_run, m_blk)

    out_t = jnp.concatenate([acc_ref[e, 0:HEAD_DIM, :] / acc_ref[e, HEAD_DIM:HEAD_DIM + 1, :]
                             for e in range(2)], axis=0)
    o_ref[...] = out_t.T


def _moba(qkv, slopes, bsz, s, n_heads):
    n_blocks = s // MOBA_BLOCK
    n_pairs = n_heads // 2
    blk = MOBA_BLOCK
    tq = MOBA_Q_BLOCKS * blk
    n_tiles = s // tq
    return pl.pallas_call(
        functools.partial(_moba_kernel, n_blocks=n_blocks),
        grid=(bsz, n_pairs, n_tiles),
        in_specs=[pl.BlockSpec(memory_space=pltpu.SMEM),
                  pl.BlockSpec((tq, LANES), lambda b, h, i: (b * n_tiles + i, h)),
                  pl.BlockSpec((s, LANES), lambda b, h, i: (b, n_pairs + h)),
                  pl.BlockSpec((s, LANES), lambda b, h, i: (b, 2 * n_pairs + h))],
        out_specs=pl.BlockSpec((tq, LANES), lambda b, h, i: (b * n_tiles + i, h)),
        out_shape=jax.ShapeDtypeStruct((bsz * s, n_heads * HEAD_DIM), F32),
        scratch_shapes=[pltpu.VMEM((n_blocks, blk, LANES), BF16),
                        pltpu.VMEM((n_blocks, blk, LANES), BF16),
                        pltpu.VMEM((n_blocks, 2, MOBA_VT_ROWS, blk), BF16),
                        pltpu.VMEM((n_blocks, LANES), F32),
                        pltpu.VMEM((MOBA_Q_BLOCKS, blk, tq), F32),
                        pltpu.VMEM((2, MOBA_VT_ROWS, tq), F32),
                        pltpu.VMEM((2, MOBA_Q_BLOCKS, blk, tq), F32),
                        pltpu.VMEM((2, MOBA_Q_BLOCKS, blk, tq), F32),
                        pltpu.VMEM((2, MOBA_Q_BLOCKS, blk, tq), F32)],
        compiler_params=_params(3),
        name="moba",
    )(slopes, qkv, qkv, qkv)


def _memory_attention(q, mem_k, mem_v):
    q = q * (HEAD_DIM ** -0.5)
    lane = lax.broadcasted_iota(jnp.int32, q.shape, 1)
    out = jnp.zeros(q.shape, F32)
    for h in range(MEM_HEADS):
        in_head = (lane >= h * HEAD_DIM) & (lane < (h + 1) * HEAD_DIM)
        qh = jnp.where(in_head, q, 0.0).astype(BF16)
        s = lax.dot_general(qh, mem_k, (((1,), (1,)), ((), ())), preferred_element_type=F32)
        m = jnp.max(s, axis=-1, keepdims=True)
        p = jnp.exp(s - m)
        l = jnp.sum(p, axis=-1, keepdims=True)
        pv = jnp.dot(p.astype(BF16), mem_v, preferred_element_type=F32)
        out = jnp.where(in_head, pv / l, out)
    return out


def _project_out(tok, mem_out, wo_ref, x):
    tw = tok.shape[1]
    return (x + jnp.dot(tok.astype(BF16), wo_ref[0:tw, :], preferred_element_type=F32)
            + jnp.dot(mem_out.astype(BF16), wo_ref[tw:, :], preferred_element_type=F32))


def _s5_mix_out_kernel(y_ref, u_ref, d_ref, wglu_ref, bglu_ref, mq_ref, mk_ref, mv_ref, wo_ref, x_ref, o_ref):
    y = y_ref[...] + d_ref[...] * u_ref[...]
    y = 0.5 * y * (1.0 + lax.erf(y * math.sqrt(0.5)))
    z = jnp.dot(y.astype(BF16), wglu_ref[...], preferred_element_type=F32) + bglu_ref[...]
    tok = y * jax.nn.sigmoid(z)
    mem_out = _memory_attention(mq_ref[...], mk_ref[...], mv_ref[...])
    o_ref[...] = _project_out(tok, mem_out, wo_ref, x_ref[...])


def _moba_mix_out_kernel(tok_ref, mq_ref, mk_ref, mv_ref, wo_ref, x_ref, o_ref):
    mem_out = _memory_attention(mq_ref[...], mk_ref[...], mv_ref[...])
    o_ref[...] = _project_out(tok_ref[...], mem_out, wo_ref, x_ref[...])


def _mix_out_specs(n, s, d, tok_w, mem_len, mq_block_col):
    tm = min(ROW_TILE, s)
    per_b = s // tm
    row = lambda i: (i, 0)
    const = lambda i: (0, 0)
    mem_specs = [pl.BlockSpec((tm, MEM_WIDTH), lambda i: (i, mq_block_col)),
                 pl.BlockSpec((mem_len, MEM_WIDTH), lambda i: (i // per_b, 0)),
                 pl.BlockSpec((mem_len, MEM_WIDTH), lambda i: (i // per_b, 1)),
                 pl.BlockSpec((d, d), const),
                 pl.BlockSpec((tm, d), row)]
    return tm, row, const, mem_specs


def _s5_mix_out(y_ssm, proj, d_skip, w_glu, b_glu, mem_kv, w_out, x2d, s, mem_len):
    n, d = x2d.shape
    tok_w = y_ssm.shape[1]
    tm, row, const, mem_specs = _mix_out_specs(n, s, d, tok_w, mem_len, tok_w // MEM_WIDTH)
    return pl.pallas_call(
        _s5_mix_out_kernel,
        grid=(n // tm,),
        in_specs=[pl.BlockSpec((tm, tok_w), row),
                  pl.BlockSpec((tm, tok_w), row),
                  pl.BlockSpec((1, tok_w), const),
                  pl.BlockSpec((tok_w, tok_w), const),
                  pl.BlockSpec((1, tok_w), const)] + mem_specs,
        out_specs=pl.BlockSpec((tm, d), row),
        out_shape=jax.ShapeDtypeStruct((n, d), F32),
        compiler_params=_params(1),
        name="s5_mix_out",
    )(y_ssm, proj, d_skip.reshape(1, tok_w), w_glu, b_glu.reshape(1, tok_w), proj, mem_kv, mem_kv, w_out, x2d)


def _moba_mix_out(tok, proj, mem_kv, w_out, x2d, s, mem_len):
    n, d = x2d.shape
    tok_w = tok.shape[1]
    tm, row, const, mem_specs = _mix_out_specs(n, s, d, tok_w, mem_len, (proj.shape[1] - MEM_WIDTH) // MEM_WIDTH)
    return pl.pallas_call(
        _moba_mix_out_kernel,
        grid=(n // tm,),
        in_specs=[pl.BlockSpec((tm, tok_w), row)] + mem_specs,
        out_specs=pl.BlockSpec((tm, d), row),
        out_shape=jax.ShapeDtypeStruct((n, d), F32),
        compiler_params=_params(1),
        name="moba_mix_out",
    )(tok, proj, mem_kv, mem_kv, w_out, x2d)


FFN_CHUNK = MXU_TILE


def _ffn_kernel(x_ref, g_ref, wg_ref, wu_ref, wd_ref, gf_ref, o_ref, act_ref, *, final_norm):
    x = x_ref[...]
    h = _rms_scale(x, g_ref[...]).astype(BF16)
    hidden = wg_ref.shape[1]
    for c in range(hidden // FFN_CHUNK):
        cols = slice(c * FFN_CHUNK, (c + 1) * FFN_CHUNK)
        gate = jnp.dot(h, wg_ref[:, cols], preferred_element_type=F32)
        up = jnp.dot(h, wu_ref[:, cols], preferred_element_type=F32)
        act_ref[:, cols] = (gate * jax.nn.sigmoid(gate) * up).astype(BF16)
    y = x + jnp.dot(act_ref[...], wd_ref[...], preferred_element_type=F32)
    if final_norm:
        y = _rms_scale(y, gf_ref[...])
    o_ref[...] = y


def _ffn(x2d, g, w_gate, w_up, w_down, g_final, final_norm):
    n, d = x2d.shape
    hidden = w_gate.shape[1]
    tm = min(ROW_TILE, n)
    row = lambda i: (i, 0)
    const = lambda i: (0, 0)
    return pl.pallas_call(
        functools.partial(_ffn_kernel, final_norm=final_norm),
        grid=(n // tm,),
        in_specs=[pl.BlockSpec((tm, d), row),
                  pl.BlockSpec((1, d), const),
                  pl.BlockSpec((d, hidden), const),
                  pl.BlockSpec((d, hidden), const),
                  pl.BlockSpec((hidden, d), const),
                  pl.BlockSpec((1, d), const)],
        out_specs=pl.BlockSpec((tm, d), row),
        out_shape=jax.ShapeDtypeStruct((n, d), F32),
        scratch_shapes=[pltpu.VMEM((tm, hidden), BF16)],
        compiler_params=_params(1),
        name="ffn",
    )(x2d, g.reshape(1, d), w_gate, w_up, w_down, g_final.reshape(1, d))


def kernel(x, mem, mem_norm_g, w_mem_kv, mix_norm_g, s5_w_in, s5_lambda_re, s5_lambda_im, s5_log_dt,
           s5_b_re, s5_b_im, s5_c_re, s5_c_im, s5_d, s5_w_glu, s5_b_glu, moba_w_in, w_out, ffn_norm_g,
           w_gate, w_up, w_down, final_norm_g):
    bsz, s, d = x.shape
    mem_len = mem.shape[1]
    depth = mix_norm_g.shape[0]
    tok_w = d - MEM_WIDTH
    n_heads = tok_w // HEAD_DIM
    slopes = jnp.asarray(2.0 ** (-8.0 * (np.arange(n_heads) + 1) / n_heads), dtype=F32)

    mem_kv = _norm_matmul(mem.reshape(bsz * mem_len, d), mem_norm_g, w_mem_kv.astype(BF16)).astype(BF16)
    x2d = x.reshape(bsz * s, d)
    for i in range(depth):
        j = i // 2
        if i % 2 == 0:
            proj = _norm_matmul(x2d, mix_norm_g[i], s5_w_in[j].astype(BF16))
            ops = _s5_operators(s5_lambda_re[j], s5_lambda_im[j], s5_log_dt[j],
                                s5_b_re[j], s5_b_im[j], s5_c_re[j], s5_c_im[j])
            y_ssm = _s5_ssm(proj, ops, bsz, s, tok_w)
            x2d = _s5_mix_out(y_ssm, proj, s5_d[j], s5_w_glu[j].astype(BF16), s5_b_glu[j],
                              mem_kv, w_out[i].astype(BF16), x2d, s, mem_len)
        else:
            proj = _norm_matmul(x2d, mix_norm_g[i], moba_w_in[j].astype(BF16))
            tok = _moba(proj, slopes, bsz, s, n_heads)
            x2d = _moba_mix_out(tok, proj, mem_kv, w_out[i].astype(BF16), x2d, s, mem_len)
        x2d = _ffn(x2d, ffn_norm_g[i], w_gate[i].astype(BF16), w_up[i].astype(BF16),
                   w_down[i].astype(BF16), final_norm_g, final_norm=(i == depth - 1))
    return x2d.reshape(bsz, s, d)
```

```python
import functools
import math

import numpy as np
import jax
import jax.numpy as jnp
from jax import lax
from jax.experimental import pallas as pl
from jax.experimental.pallas import tpu as pltpu

F32 = jnp.float32
BF16 = jnp.bfloat16

HEAD_DIM = 64
MEM_HEADS = 4
MEM_WIDTH = MEM_HEADS * HEAD_DIM
S5_GROUP_CH = 16
S5_STATE = 64
S5_CHUNK = 16
MOBA_BLOCK = 256
MOBA_TOPK = 3
RMS_EPS = 1e-6
NEG_INF = -1e30

LANES = 128
MXU_TILE = 256
VMEM_LIMIT_BYTES = 56 * 1024 * 1024
ROW_TILE = 512

S5_LANE_GROUPS = LANES // S5_GROUP_CH
S5_STEPS_PER_TILE = MXU_TILE // LANES
S5_LAG_TILES = S5_CHUNK // S5_STEPS_PER_TILE


def _params(n_axes, flags=None):
    return pltpu.CompilerParams(dimension_semantics=("arbitrary",) * n_axes,
                                vmem_limit_bytes=VMEM_LIMIT_BYTES, flags=flags)


def _rms_scale(x, g):
    ms = jnp.mean(x * x, axis=-1, keepdims=True)
    return x * lax.rsqrt(ms + RMS_EPS) * g


def _norm_matmul_kernel(x_ref, g_ref, w_ref, o_ref):
    h = _rms_scale(x_ref[...], g_ref[...])
    o_ref[...] = jnp.dot(h.astype(BF16), w_ref[...], preferred_element_type=F32)


def _norm_matmul(x2d, g, w_bf16):
    n, d = x2d.shape
    f = w_bf16.shape[1]
    tm = min(ROW_TILE, n)
    return pl.pallas_call(
        _norm_matmul_kernel,
        grid=(n // tm,),
        in_specs=[pl.BlockSpec((tm, d), lambda i: (i, 0)),
                  pl.BlockSpec((1, d), lambda i: (0, 0)),
                  pl.BlockSpec((d, f), lambda i: (0, 0))],
        out_specs=pl.BlockSpec((tm, f), lambda i: (i, 0)),
        out_shape=jax.ShapeDtypeStruct((n, f), F32),
        compiler_params=_params(1),
        name="norm_matmul",
    )(x2d, g.reshape(1, d), w_bf16)


def _s5_operators(lam_re, lam_im, log_dt, b_re, b_im, c_re, c_im):
    hi = lax.Precision.HIGHEST
    t_len = S5_CHUNK
    g_cnt, p_cnt = lam_re.shape
    c_cnt = b_re.shape[-1]
    gpb = S5_LANE_GROUPS
    n_lb = g_cnt // gpb
    spt = S5_STEPS_PER_TILE
    lr = lam_re.astype(F32)
    li = lam_im.astype(F32)
    dt = jnp.exp(log_dt.astype(F32))[:, None]
    mag = jnp.exp(lr * dt)
    ar = mag * jnp.cos(li * dt)
    ai = mag * jnp.sin(li * dt)
    den = lr * lr + li * li
    nr = ar - 1.0
    fr = (nr * lr + ai * li) / den
    fi = (ai * lr - nr * li) / den
    br = jnp.swapaxes(b_re.astype(F32), 1, 2)
    bi = jnp.swapaxes(b_im.astype(F32), 1, 2)
    bbr = fr[:, None, :] * br - fi[:, None, :] * bi
    bbi = fr[:, None, :] * bi + fi[:, None, :] * br
    cr = c_re.astype(F32)
    ci = c_im.astype(F32)

    k = jnp.arange(t_len + 1, dtype=F32)[:, None, None]
    pm = jnp.exp(k * (lr * dt)[None])
    pr = pm * jnp.cos(k * (li * dt)[None])
    pi = pm * jnp.sin(k * (li * dt)[None])

    wr = pr[:, :, None, :] * bbr[None] - pi[:, :, None, :] * bbi[None]
    wi = pr[:, :, None, :] * bbi[None] + pi[:, :, None, :] * bbr[None]
    kern = (jnp.einsum('gop,kgcp->kgoc', cr, wr[:t_len], precision=hi)
            - jnp.einsum('gop,kgcp->kgoc', ci, wi[:t_len], precision=hi))
    lag = (jnp.arange(S5_LAG_TILES)[:, None, None] * spt
           + jnp.arange(spt)[None, None, :] - jnp.arange(spt)[None, :, None])
    kd = jnp.where((lag >= 0)[..., None, None, None], kern[jnp.clip(lag, 0, t_len - 1)], 0.0)
    kd = kd.reshape(S5_LAG_TILES, spt, spt, n_lb, gpb, c_cnt, c_cnt)
    lag_t = kd.transpose(3, 0, 2, 5, 1, 4, 6).reshape(n_lb, S5_LAG_TILES, spt * c_cnt, spt * gpb * c_cnt)

    w_rev = jnp.concatenate([wr[:t_len][::-1], wi[:t_len][::-1]], axis=-1)
    p_c = w_rev.reshape(t_len, n_lb, gpb, c_cnt, 2 * p_cnt).transpose(1, 0, 2, 3, 4).reshape(
        n_lb, t_len * gpb * c_cnt, 2 * p_cnt)

    qr = cr[None] * pr[1:, :, None, :] - ci[None] * pi[1:, :, None, :]
    qi = -(cr[None] * pi[1:, :, None, :] + ci[None] * pr[1:, :, None, :])
    q_c = jnp.stack([qr, qi]).reshape(2, t_len, n_lb, gpb, c_cnt, p_cnt).transpose(2, 0, 5, 1, 3, 4).reshape(
        n_lb, 2 * p_cnt, t_len * gpb * c_cnt)

    coef = jnp.stack([pr[t_len].reshape(n_lb, gpb * p_cnt),
                      pi[t_len].reshape(n_lb, gpb * p_cnt)], axis=1)
    return lag_t.astype(BF16), p_c.astype(BF16), q_c.astype(BF16), coef


def _iota(shape, dim):
    return lax.broadcasted_iota(jnp.int32, shape, dim)


def _s5_ssm_kernel(u_ref, lag_ref, pc_ref, qc_ref, coef_ref, y_ref,
                   toep_scr, p_scr, q_scr, h_scr, s_scr, xp_scr, *, n_chunks):
    t_len = S5_CHUNK
    gpb = S5_LANE_GROUPS
    st = S5_STATE
    cc = S5_GROUP_CH
    half = gpb * st
    flat_w = t_len * LANES
    nt_dims = (((1,), (1,)), ((), ()))

    @pl.when(pl.program_id(1) == 0)
    def _expand_operators():
        def group_diag(x, row_unit, col_unit):
            same = ((_iota(x.shape, 0) // row_unit) % gpb) == ((_iota(x.shape, 1) // col_unit) % gpb)
            return jnp.where(same, x, 0.0).astype(BF16)

        shape = (2 * st, 2 * half)
        rep = ((_iota(shape, 0) // st == _iota(shape, 1) // half)
               & (_iota(shape, 0) % st == _iota(shape, 1) % st)).astype(BF16)
        for r in range(flat_w // MXU_TILE):
            rows = slice(r * MXU_TILE, (r + 1) * MXU_TILE)
            p_scr[rows, :] = group_diag(jnp.dot(pc_ref[0, rows, :], rep, preferred_element_type=F32), cc, st)
        shape = (2 * half, 2 * st)
        rep = ((_iota(shape, 0) // half == _iota(shape, 1) // st)
               & (_iota(shape, 0) % st == _iota(shape, 1) % st)).astype(BF16)
        for c in range(flat_w // MXU_TILE):
            cols = slice(c * MXU_TILE, (c + 1) * MXU_TILE)
            q_scr[:, cols] = group_diag(jnp.dot(rep, qc_ref[0, :, cols], preferred_element_type=F32), st, cc)
        shape = (MXU_TILE, S5_STEPS_PER_TILE * cc)
        rep = ((_iota(shape, 0) // LANES == _iota(shape, 1) // cc)
               & (_iota(shape, 0) % cc == _iota(shape, 1) % cc)).astype(BF16)
        for l in range(S5_LAG_TILES):
            toep_scr[l] = group_diag(jnp.dot(rep, lag_ref[0, l], preferred_element_type=F32), cc, cc)

    for j in range(t_len):
        h_scr[:, j * LANES:(j + 1) * LANES] = u_ref[pl.ds(j, n_chunks, stride=t_len), :].astype(BF16)
    s_scr[...] = jnp.dot(h_scr[...], p_scr[...], preferred_element_type=F32)
    cr = coef_ref[0, 0:1, :]
    ci = coef_ref[0, 1:2, :]

    def step(n, carry):
        xr, xi = carry
        xp_scr[pl.ds(n, 1), 0:half] = xr
        xp_scr[pl.ds(n, 1), half:2 * half] = xi
        sr = s_scr[pl.ds(n, 1), 0:half]
        si = s_scr[pl.ds(n, 1), half:2 * half]
        return cr * xr - ci * xi + sr, cr * xi + ci * xr + si

    zero = jnp.zeros((1, half), F32)
    lax.fori_loop(0, n_chunks, step, (zero, zero))
    xp = xp_scr[...].astype(BF16)
    for nt in range(S5_LAG_TILES):
        cols = slice(nt * MXU_TILE, (nt + 1) * MXU_TILE)
        acc = jnp.dot(xp, q_scr[:, cols], preferred_element_type=F32)
        for kt in range(nt + 1):
            acc = acc + lax.dot_general(h_scr[:, kt * MXU_TILE:(kt + 1) * MXU_TILE], toep_scr[nt - kt],
                                        nt_dims, preferred_element_type=F32)
        for tt in range(S5_STEPS_PER_TILE):
            t = nt * S5_STEPS_PER_TILE + tt
            y_ref[pl.ds(t, n_chunks, stride=t_len), :] = acc[:, tt * LANES:(tt + 1) * LANES]


def _s5_ssm(proj, ops, bsz, s, tok_w):
    lag_t, p_c, q_c, coef = ops
    n_lb = tok_w // LANES
    n_chunks = s // S5_CHUNK
    flat_w = p_c.shape[1]
    state_w = S5_LANE_GROUPS * p_c.shape[2]
    const3 = lambda l, b: (l, 0, 0)
    return pl.pallas_call(
        functools.partial(_s5_ssm_kernel, n_chunks=n_chunks),
        grid=(n_lb, bsz),
        in_specs=[pl.BlockSpec((s, LANES), lambda l, b: (b, l)),
                  pl.BlockSpec((1,) + lag_t.shape[1:], lambda l, b: (l, 0, 0, 0)),
                  pl.BlockSpec((1,) + p_c.shape[1:], const3),
                  pl.BlockSpec((1,) + q_c.shape[1:], const3),
                  pl.BlockSpec((1,) + coef.shape[1:], const3)],
        out_specs=pl.BlockSpec((s, LANES), lambda l, b: (b, l)),
        out_shape=jax.ShapeDtypeStruct((bsz * s, tok_w), F32),
        scratch_shapes=[pltpu.VMEM((S5_LAG_TILES, MXU_TILE, MXU_TILE), BF16),
                        pltpu.VMEM((flat_w, state_w), BF16),
                        pltpu.VMEM((state_w, flat_w), BF16),
                        pltpu.VMEM((n_chunks, flat_w), BF16),
                        pltpu.VMEM((n_chunks, state_w), F32),
                        pltpu.VMEM((n_chunks, state_w), F32)],
        compiler_params=_params(2),
        name="s5_ssm",
    )(proj, lag_t, p_c, q_c, coef)


MOBA_Q_BLOCKS = 2
MOBA_VT_ROWS = HEAD_DIM + 16
LOG2E = math.log2(math.e)
MOBA_AUG_HI = 8


def _moba_kernel(slope_ref, q_ref, k_ref, v_ref, o_ref,
                 ka_ref, kb_ref, vt_ref, km_ref, mask_ref, acc_ref, sca_ref, scb_ref, scd_ref, *, n_blocks):
    blk = MOBA_BLOCK
    nq = MOBA_Q_BLOCKS
    tq = nq * blk
    hp = pl.program_id(1)
    qi = pl.program_id(2)
    slopes = (slope_ref[2 * hp] * LOG2E, slope_ref[2 * hp + 1] * LOG2E)
    k_refs = (ka_ref, kb_ref)

    @pl.when(qi == 0)
    def _prepare():
        lane = lax.broadcasted_iota(jnp.int32, (blk, LANES), 1)
        key_off = lax.broadcasted_iota(jnp.int32, (blk, LANES), 0).astype(F32)
        ones_row = (lax.broadcasted_iota(jnp.int32, (MOBA_VT_ROWS - HEAD_DIM, blk), 0) == 0).astype(BF16)
        in_heads = (lane < HEAD_DIM, lane >= HEAD_DIM)
        aug_lane = (lane - HEAD_DIM, lane)
        key_bias = []
        for e in range(2):
            a = slopes[e] * key_off
            a_hi = a.astype(BF16).astype(F32)
            key_bias.append(jnp.where(aug_lane[e] == 0, a_hi, jnp.where(aug_lane[e] == 1, a - a_hi, 0.0)))
        for j in range(n_blocks):
            kj = k_ref[j * blk:(j + 1) * blk, :]
            for e in range(2):
                one_hot = (aug_lane[e] == MOBA_AUG_HI + j) | (aug_lane[e] == MOBA_AUG_HI + n_blocks + j)
                aug = jnp.where(one_hot, 1.0, key_bias[e])
                k_refs[e][j] = jnp.where(in_heads[e], kj, aug).astype(BF16)
            v_t = v_ref[j * blk:(j + 1) * blk, :].T
            for e in range(2):
                vt_ref[j, e, 0:HEAD_DIM, :] = v_t[e * HEAD_DIM:(e + 1) * HEAD_DIM, :].astype(BF16)
                vt_ref[j, e, HEAD_DIM:MOBA_VT_ROWS, :] = ones_row
            km_ref[j:j + 1, :] = jnp.mean(kj, axis=0, keepdims=True)
        d = (lax.broadcasted_iota(jnp.int32, (blk, blk), 1)
             - lax.broadcasted_iota(jnp.int32, (blk, blk), 0))
        causal = jnp.where(d >= 0, 0.0, NEG_INF)
        free = jnp.zeros((blk, blk), F32)
        mask_ref[0] = jnp.concatenate([causal, free], axis=1)
        mask_ref[1] = jnp.concatenate([free, causal], axis=1)

    q_t = (q_ref[...] * (HEAD_DIM ** -0.5 * LOG2E)).T

    km = km_ref[...]
    lane_m = lax.broadcasted_iota(jnp.int32, km.shape, 1)
    blk_id = lax.broadcasted_iota(jnp.int32, (n_blocks, tq), 0)
    q_pos = lax.broadcasted_iota(jnp.int32, (n_blocks, tq), 1)
    q_blk = nq * qi + q_pos // blk
    q_off = (q_pos % blk).astype(F32)
    past = blk_id < q_blk
    q_bf = q_t.astype(BF16)
    ones_rows = (lax.broadcasted_iota(jnp.int32, (MOBA_AUG_HI, tq), 0) < 2).astype(F32)
    pad_rows = jnp.zeros((HEAD_DIM - MOBA_AUG_HI - 2 * n_blocks, tq), F32)
    q_aug = []
    for e in range(2):
        in_head = (lane_m < HEAD_DIM) if e == 0 else (lane_m >= HEAD_DIM)
        km_e = jnp.where(in_head, km, 0.0).astype(BF16)
        gate = jnp.dot(km_e, q_bf, preferred_element_type=F32)
        gate = jnp.where(past, gate, NEG_INF)
        rank = jnp.zeros((n_blocks, tq), jnp.int32)
        for jp in range(n_blocks):
            gj = gate[jp:jp + 1, :]
            beats = (gj > gate) | ((gj == gate) & (jp < blk_id))
            rank = rank + beats.astype(jnp.int32)
        visible = (past & (rank < MOBA_TOPK)) | (blk_id == q_blk)
        bias = -slopes[e] * (((q_blk - blk_id) * blk).astype(F32) + q_off)
        bias = jnp.where(visible, bias, NEG_INF)
        bias_hi = bias.astype(BF16).astype(F32)
        aug = jnp.concatenate([ones_rows, bias_hi, bias - bias_hi, pad_rows], axis=0)
        q_head = q_t[e * HEAD_DIM:(e + 1) * HEAD_DIM, :]
        rows = [q_head, aug] if e == 0 else [aug, q_head]
        q_aug.append(jnp.concatenate(rows, axis=0).astype(BF16))

    def score_chunk(c, sc_ref, on_diag):
        maxes = []
        for e in range(2):
            m_blk = None
            for w in range(nq):
                s_t = jnp.dot(k_refs[e][nq * c + w], q_aug[e], preferred_element_type=F32)
                if on_diag:
                    s_t = s_t + mask_ref[w]
                sc_ref[e, w] = s_t
                mx = jnp.max(s_t, axis=0, keepdims=True)
                m_blk = mx if m_blk is None else jnp.maximum(m_blk, mx)
            maxes.append(m_blk)
        return tuple(maxes)

    def accumulate_chunk(c, sc_ref, m_run, m_blk):
        m_next = []
        for e in range(2):
            m_new = jnp.maximum(m_run[e], m_blk[e])
            pv = None
            for w in range(nq):
                p = jnp.exp2(sc_ref[e, w] - m_new).astype(BF16)
                d = jnp.dot(vt_ref[nq * c + w, e], p, preferred_element_type=F32)
                pv = d if pv is None else pv + d
            acc_ref[e] = jnp.exp2(m_run[e] - m_new) * acc_ref[e] + pv
            m_next.append(m_new)
        return tuple(m_next)

    acc_ref[...] = jnp.zeros_like(acc_ref)
    m_init = jnp.full((1, tq), NEG_INF, F32)
    m_diag = score_chunk(qi, scd_ref, True)
    m_first = score_chunk(0, sca_ref, False)
    m_run = accumulate_chunk(qi, scd_ref, (m_init, m_init), m_diag)
    last = jnp.maximum(qi - 1, 0)

    def body(t, carry):
        m_run, m_blk = carry
        c = 2 * t
        m_blk_b = score_chunk(c + 1, scb_ref, False)
        m_run = accumulate_chunk(c, sca_ref, m_run, m_blk)
        m_blk_a = score_chunk(jnp.minimum(c + 2, last), sca_ref, False)
        m_run = accumulate_chunk(c + 1, scb_ref, m_run, m_blk_b)
        return m_run, m_blk_a

    m_run, m_blk = lax.fori_loop(0, qi // 2, body, (m_run, m_first))

    @pl.when(qi % 2 == 1)
    def _odd_chunk():
        accumulate_chunk(qi - 1, sca_ref, m_run, m_blk)

    out_t = jnp.concatenate([acc_ref[e, 0:HEAD_DIM, :] / acc_ref[e, HEAD_DIM:HEAD_DIM + 1, :]
                             for e in range(2)], axis=0)
    o_ref[...] = out_t.T


def _moba(qkv, slopes, bsz, s, n_heads):
    n_blocks = s // MOBA_BLOCK
    n_pairs = n_heads // 2
    blk = MOBA_BLOCK
    tq = MOBA_Q_BLOCKS * blk
    n_tiles = s // tq
    return pl.pallas_call(
        functools.partial(_moba_kernel, n_blocks=n_blocks),
        grid=(bsz, n_pairs, n_tiles),
        in_specs=[pl.BlockSpec(memory_space=pltpu.SMEM),
                  pl.BlockSpec((tq, LANES), lambda b, h, i: (b * n_tiles + i, h)),
                  pl.BlockSpec((s, LANES), lambda b, h, i: (b, n_pairs + h)),
                  pl.BlockSpec((s, LANES), lambda b, h, i: (b, 2 * n_pairs + h))],
        out_specs=pl.BlockSpec((tq, LANES), lambda b, h, i: (b * n_tiles + i, h)),
        out_shape=jax.ShapeDtypeStruct((bsz * s, n_heads * HEAD_DIM), F32),
        scratch_shapes=[pltpu.VMEM((n_blocks, blk, LANES), BF16),
                        pltpu.VMEM((n_blocks, blk, LANES), BF16),
                        pltpu.VMEM((n_blocks, 2, MOBA_VT_ROWS, blk), BF16),
                        pltpu.VMEM((n_blocks, LANES), F32),
                        pltpu.VMEM((MOBA_Q_BLOCKS, blk, tq), F32),
                        pltpu.VMEM((2, MOBA_VT_ROWS, tq), F32),
                        pltpu.VMEM((2, MOBA_Q_BLOCKS, blk, tq), F32),
                        pltpu.VMEM((2, MOBA_Q_BLOCKS, blk, tq), F32),
                        pltpu.VMEM((2, MOBA_Q_BLOCKS, blk, tq), F32)],
        compiler_params=_params(3),
        name="moba",
    )(slopes, qkv, qkv, qkv)


def _moba_schedule(n_tiles):
    rows = []
    for ti in range(n_tiles):
        seq = [ti] + list(range(ti))
        for n, c in enumerate(seq):
            rows.append((ti, c, int(n == 0), int(n == len(seq) - 1)))
    assert len(rows) % 2 == 0, "the pipelined loop retires two work items per step"
    rows += [(rows[-1][0], rows[-1][1], 0, 0)] * 2
    return np.asarray(rows, np.int32).T.copy()


def _moba_pipe_kernel(slope_ref, sched_ref, q_ref, k_ref, v_ref, o_ref,
                      ka_ref, kb_ref, vt_ref, km_ref, tri_ref, qa_ref, acc_ref,
                      s0_ref, s1_ref, p0_ref, p1_ref, *, n_blocks, n_items):
    blk = MOBA_BLOCK
    nq = MOBA_Q_BLOCKS
    tq = nq * blk
    n_tiles = n_blocks // nq
    hp = pl.program_id(1)
    slopes = (slope_ref[2 * hp] * LOG2E, slope_ref[2 * hp + 1] * LOG2E)
    k_refs = (ka_ref, kb_ref)

    lane = _iota((blk, LANES), 1)
    key_off = _iota((blk, LANES), 0).astype(F32)
    ones_row = (_iota((MOBA_VT_ROWS - HEAD_DIM, blk), 0) == 0).astype(BF16)
    in_heads = (lane < HEAD_DIM, lane >= HEAD_DIM)
    aug_lane = (lane - HEAD_DIM, lane)
    key_bias = []
    for e in range(2):
        a = slopes[e] * key_off
        a_hi = a.astype(BF16).astype(F32)
        key_bias.append(jnp.where(aug_lane[e] == 0, a_hi, jnp.where(aug_lane[e] == 1, a - a_hi, 0.0)))
    for j in range(n_blocks):
        kj = k_ref[j * blk:(j + 1) * blk, :]
        for e in range(2):
            one_hot = (aug_lane[e] == MOBA_AUG_HI + j) | (aug_lane[e] == MOBA_AUG_HI + n_blocks + j)
            aug = jnp.where(one_hot, 1.0, key_bias[e])
            k_refs[e][j] = jnp.where(in_heads[e], kj, aug).astype(BF16)
        v_t = v_ref[j * blk:(j + 1) * blk, :].T
        for e in range(2):
            vt_ref[j, e, 0:HEAD_DIM, :] = v_t[e * HEAD_DIM:(e + 1) * HEAD_DIM, :].astype(BF16)
            vt_ref[j, e, HEAD_DIM:MOBA_VT_ROWS, :] = ones_row
        km_ref[j:j + 1, :] = jnp.mean(kj, axis=0, keepdims=True)
    d = _iota((blk, blk), 1) - _iota((blk, blk), 0)
    tri_ref[0] = jnp.zeros((blk, blk), F32)
    tri_ref[1] = jnp.where(d >= 0, 0.0, NEG_INF)

    km = km_ref[...]
    lane_m = _iota(km.shape, 1)
    blk_id = _iota((n_blocks, tq), 0)
    q_pos = _iota((n_blocks, tq), 1)
    q_off = (q_pos % blk).astype(F32)
    ones_rows = (_iota((MOBA_AUG_HI, tq), 0) < 2).astype(F32)
    pad_rows = jnp.zeros((HEAD_DIM - MOBA_AUG_HI - 2 * n_blocks, tq), F32)

    def gate_tile(ti, carry):
        rows = pl.ds(pl.multiple_of(ti * tq, tq), tq)
        q_t = (q_ref[rows, :] * (HEAD_DIM ** -0.5 * LOG2E)).T
        q_bf = q_t.astype(BF16)
        q_blk = nq * ti + q_pos // blk
        past = blk_id < q_blk
        for e in range(2):
            in_head = (lane_m < HEAD_DIM) if e == 0 else (lane_m >= HEAD_DIM)
            km_e = jnp.where(in_head, km, 0.0).astype(BF16)
            gate = jnp.dot(km_e, q_bf, preferred_element_type=F32)
            gate = jnp.where(past, gate, NEG_INF)
            rank = jnp.zeros((n_blocks, tq), jnp.int32)
            for jp in range(n_blocks):
                gj = gate[jp:jp + 1, :]
                beats = (gj > gate) | ((gj == gate) & (jp < blk_id))
                rank = rank + beats.astype(jnp.int32)
            visible = (past & (rank < MOBA_TOPK)) | (blk_id == q_blk)
            bias = -slopes[e] * (((q_blk - blk_id) * blk).astype(F32) + q_off)
            bias = jnp.where(visible, bias, NEG_INF)
            bias_hi = bias.astype(BF16).astype(F32)
            aug = jnp.concatenate([ones_rows, bias_hi, bias - bias_hi, pad_rows], axis=0)
            q_head = q_t[e * HEAD_DIM:(e + 1) * HEAD_DIM, :]
            parts = [q_head, aug] if e == 0 else [aug, q_head]
            qa_ref[ti, e] = jnp.concatenate(parts, axis=0).astype(BF16)
        return carry

    lax.fori_loop(0, n_tiles, gate_tile, 0)

    def score(w, s_ref):
        ti = sched_ref[0, w]
        c = sched_ref[1, w]
        on_diag = sched_ref[2, w]
        maxes = []
        for e in range(2):
            q_aug = qa_ref[ti, e]
            m_blk = None
            for slab in range(nq):
                s_t = jnp.dot(k_refs[e][nq * c + slab], q_aug, preferred_element_type=F32)
                parts = [s_t[:, h * blk:(h + 1) * blk] + tri_ref[on_diag] if h == slab
                         else s_t[:, h * blk:(h + 1) * blk] for h in range(nq)]
                s_t = jnp.concatenate(parts, axis=1)
                s_ref[e, slab] = s_t
                mx = jnp.max(s_t, axis=0, keepdims=True)
                m_blk = mx if m_blk is None else jnp.maximum(m_blk, mx)
            maxes.append(m_blk)
        return tuple(maxes)

    def exponentiate(s_ref, p_ref, m):
        for e in range(2):
            for slab in range(nq):
                p_ref[e, slab] = jnp.exp2(s_ref[e, slab] - m[e]).astype(BF16)

    def accumulate(w, p_ref, alpha):
        c = sched_ref[1, w]
        par = sched_ref[0, w] % 2
        for e in range(2):
            pv = None
            for slab in range(nq):
                dd = jnp.dot(vt_ref[nq * c + slab, e], p_ref[e, slab], preferred_element_type=F32)
                pv = dd if pv is None else pv + dd
            acc_ref[par, e] = alpha[e] * acc_ref[par, e] + pv

    def retire(w, carry, s_next, p_next, s_free, p_cur):
        m_prev, m_cur, m_blk_next = carry
        starts_tile = sched_ref[2, w] == 1
        next_starts_tile = sched_ref[2, w + 1] == 1
        m_next = tuple(jnp.where(next_starts_tile, m_blk_next[e], jnp.maximum(m_cur[e], m_blk_next[e]))
                       for e in range(2))
        exponentiate(s_next, p_next, m_next)
        alpha = tuple(jnp.exp2(jnp.where(starts_tile, NEG_INF, m_prev[e]) - m_cur[e]) for e in range(2))
        accumulate(w, p_cur, alpha)
        return m_cur, m_next, score(w + 2, s_free)

    def write_tile_if_done(w):
        @pl.when(sched_ref[3, w] == 1)
        def _write_tile():
            ti = sched_ref[0, w]
            par = ti % 2
            rows = pl.ds(pl.multiple_of(ti * tq, tq), tq)
            out_t = jnp.concatenate([acc_ref[par, e, 0:HEAD_DIM, :] / acc_ref[par, e, HEAD_DIM:HEAD_DIM + 1, :]
                                     for e in range(2)], axis=0)
            o_ref[rows, :] = out_t.T

    acc_ref[...] = jnp.zeros_like(acc_ref)
    m_first = score(0, s0_ref)
    exponentiate(s0_ref, p0_ref, m_first)
    carry = (m_first, m_first, score(1, s1_ref))

    def body(t, carry):
        carry = retire(2 * t, carry, s1_ref, p1_ref, s0_ref, p0_ref)
        carry = retire(2 * t + 1, carry, s0_ref, p0_ref, s1_ref, p1_ref)
        write_tile_if_done(2 * t)
        write_tile_if_done(2 * t + 1)
        return carry

    lax.fori_loop(0, n_items // 2, body, carry)


def _moba_pipe(qkv, slopes, bsz, s, n_heads):
    n_blocks = s // MOBA_BLOCK
    n_pairs = n_heads // 2
    blk = MOBA_BLOCK
    tq = MOBA_Q_BLOCKS * blk
    n_tiles = s // tq
    sched = _moba_schedule(n_tiles)
    n_items = sched.shape[1] - 2
    logits = pltpu.VMEM((2, MOBA_Q_BLOCKS, blk, tq), F32)
    probs = pltpu.VMEM((2, MOBA_Q_BLOCKS, blk, tq), BF16)
    return pl.pallas_call(
        functools.partial(_moba_pipe_kernel, n_blocks=n_blocks, n_items=n_items),
        grid=(bsz, n_pairs),
        in_specs=[pl.BlockSpec(memory_space=pltpu.SMEM),
                  pl.BlockSpec(memory_space=pltpu.SMEM),
                  pl.BlockSpec((s, LANES), lambda b, h: (b, h)),
                  pl.BlockSpec((s, LANES), lambda b, h: (b, n_pairs + h)),
                  pl.BlockSpec((s, LANES), lambda b, h: (b, 2 * n_pairs + h))],
        out_specs=pl.BlockSpec((s, LANES), lambda b, h: (b, h)),
        out_shape=jax.ShapeDtypeStruct((bsz * s, n_heads * HEAD_DIM), F32),
        scratch_shapes=[pltpu.VMEM((n_blocks, blk, LANES), BF16),
                        pltpu.VMEM((n_blocks, blk, LANES), BF16),
                        pltpu.VMEM((n_blocks, 2, MOBA_VT_ROWS, blk), BF16),
                        pltpu.VMEM((n_blocks, LANES), F32),
                        pltpu.VMEM((2, blk, blk), F32),
                        pltpu.VMEM((n_tiles, 2, LANES, tq), BF16),
                        pltpu.VMEM((2, 2, MOBA_VT_ROWS, tq), F32),
                        logits, logits, probs, probs],
        compiler_params=_params(2),
        name="moba",
    )(slopes, jnp.asarray(sched), qkv, qkv, qkv)


def _memory_attention(q, mem_k, mem_v):
    q = q * (HEAD_DIM ** -0.5)
    lane = lax.broadcasted_iota(jnp.int32, q.shape, 1)
    out = jnp.zeros(q.shape, F32)
    for h in range(MEM_HEADS):
        in_head = (lane >= h * HEAD_DIM) & (lane < (h + 1) * HEAD_DIM)
        qh = jnp.where(in_head, q, 0.0).astype(BF16)
        s = lax.dot_general(qh, mem_k, (((1,), (1,)), ((), ())), preferred_element_type=F32)
        m = jnp.max(s, axis=-1, keepdims=True)
        p = jnp.exp(s - m)
        l = jnp.sum(p, axis=-1, keepdims=True)
        pv = jnp.dot(p.astype(BF16), mem_v, preferred_element_type=F32)
        out = jnp.where(in_head, pv / l, out)
    return out


def _project_out(tok, mem_out, wo_ref, x):
    tw = tok.shape[1]
    return (x + jnp.dot(tok.astype(BF16), wo_ref[0:tw, :], preferred_element_type=F32)
            + jnp.dot(mem_out.astype(BF16), wo_ref[tw:, :], preferred_element_type=F32))


def _s5_mix_out_kernel(y_ref, u_ref, d_ref, wglu_ref, bglu_ref, mq_ref, mk_ref, mv_ref, wo_ref, x_ref, o_ref):
    y = y_ref[...] + d_ref[...] * u_ref[...]
    y = 0.5 * y * (1.0 + lax.erf(y * math.sqrt(0.5)))
    z = jnp.dot(y.astype(BF16), wglu_ref[...], preferred_element_type=F32) + bglu_ref[...]
    tok = y * jax.nn.sigmoid(z)
    mem_out = _memory_attention(mq_ref[...], mk_ref[...], mv_ref[...])
    o_ref[...] = _project_out(tok, mem_out, wo_ref, x_ref[...])


def _moba_mix_out_kernel(tok_ref, mq_ref, mk_ref, mv_ref, wo_ref, x_ref, o_ref):
    mem_out = _memory_attention(mq_ref[...], mk_ref[...], mv_ref[...])
    o_ref[...] = _project_out(tok_ref[...], mem_out, wo_ref, x_ref[...])


def _mix_out_specs(n, s, d, tok_w, mem_len, mq_block_col):
    tm = min(ROW_TILE, s)
    per_b = s // tm
    row = lambda i: (i, 0)
    const = lambda i: (0, 0)
    mem_specs = [pl.BlockSpec((tm, MEM_WIDTH), lambda i: (i, mq_block_col)),
                 pl.BlockSpec((mem_len, MEM_WIDTH), lambda i: (i // per_b, 0)),
                 pl.BlockSpec((mem_len, MEM_WIDTH), lambda i: (i // per_b, 1)),
                 pl.BlockSpec((d, d), const),
                 pl.BlockSpec((tm, d), row)]
    return tm, row, const, mem_specs


def _s5_mix_out(y_ssm, proj, d_skip, w_glu, b_glu, mem_kv, w_out, x2d, s, mem_len):
    n, d = x2d.shape
    tok_w = y_ssm.shape[1]
    tm, row, const, mem_specs = _mix_out_specs(n, s, d, tok_w, mem_len, tok_w // MEM_WIDTH)
    return pl.pallas_call(
        _s5_mix_out_kernel,
        grid=(n // tm,),
        in_specs=[pl.BlockSpec((tm, tok_w), row),
                  pl.BlockSpec((tm, tok_w), row),
                  pl.BlockSpec((1, tok_w), const),
                  pl.BlockSpec((tok_w, tok_w), const),
                  pl.BlockSpec((1, tok_w), const)] + mem_specs,
        out_specs=pl.BlockSpec((tm, d), row),
        out_shape=jax.ShapeDtypeStruct((n, d), F32),
        compiler_params=_params(1),
        name="s5_mix_out",
    )(y_ssm, proj, d_skip.reshape(1, tok_w), w_glu, b_glu.reshape(1, tok_w), proj, mem_kv, mem_kv, w_out, x2d)


def _moba_mix_out(tok, proj, mem_kv, w_out, x2d, s, mem_len):
    n, d = x2d.shape
    tok_w = tok.shape[1]
    tm, row, const, mem_specs = _mix_out_specs(n, s, d, tok_w, mem_len, (proj.shape[1] - MEM_WIDTH) // MEM_WIDTH)
    return pl.pallas_call(
        _moba_mix_out_kernel,
        grid=(n // tm,),
        in_specs=[pl.BlockSpec((tm, tok_w), row)] + mem_specs,
        out_specs=pl.BlockSpec((tm, d), row),
        out_shape=jax.ShapeDtypeStruct((n, d), F32),
        compiler_params=_params(1),
        name="moba_mix_out",
    )(tok, proj, mem_kv, mem_kv, w_out, x2d)


FFN_CHUNK = MXU_TILE


def _ffn_kernel(x_ref, g_ref, wg_ref, wu_ref, wd_ref, gf_ref, o_ref, act_ref, *, final_norm):
    x = x_ref[...]
    h = _rms_scale(x, g_ref[...]).astype(BF16)
    hidden = wg_ref.shape[1]
    for c in range(hidden // FFN_CHUNK):
        cols = slice(c * FFN_CHUNK, (c + 1) * FFN_CHUNK)
        gate = jnp.dot(h, wg_ref[:, cols], preferred_element_type=F32)
        up = jnp.dot(h, wu_ref[:, cols], preferred_element_type=F32)
        act_ref[:, cols] = (gate * jax.nn.sigmoid(gate) * up).astype(BF16)
    y = x + jnp.dot(act_ref[...], wd_ref[...], preferred_element_type=F32)
    if final_norm:
        y = _rms_scale(y, gf_ref[...])
    o_ref[...] = y


def _ffn(x2d, g, w_gate, w_up, w_down, g_final, final_norm):
    n, d = x2d.shape
    hidden = w_gate.shape[1]
    tm = min(ROW_TILE, n)
    row = lambda i: (i, 0)
    const = lambda i: (0, 0)
    return pl.pallas_call(
        functools.partial(_ffn_kernel, final_norm=final_norm),
        grid=(n // tm,),
        in_specs=[pl.BlockSpec((tm, d), row),
                  pl.BlockSpec((1, d), const),
                  pl.BlockSpec((d, hidden), const),
                  pl.BlockSpec((d, hidden), const),
                  pl.BlockSpec((hidden, d), const),
                  pl.BlockSpec((1, d), const)],
        out_specs=pl.BlockSpec((tm, d), row),
        out_shape=jax.ShapeDtypeStruct((n, d), F32),
        scratch_shapes=[pltpu.VMEM((tm, hidden), BF16)],
        compiler_params=_params(1),
        name="ffn",
    )(x2d, g.reshape(1, d), w_gate, w_up, w_down, g_final.reshape(1, d))


def kernel(x, mem, mem_norm_g, w_mem_kv, mix_norm_g, s5_w_in, s5_lambda_re, s5_lambda_im, s5_log_dt,
           s5_b_re, s5_b_im, s5_c_re, s5_c_im, s5_d, s5_w_glu, s5_b_glu, moba_w_in, w_out, ffn_norm_g,
           w_gate, w_up, w_down, final_norm_g):
    bsz, s, d = x.shape
    mem_len = mem.shape[1]
    depth = mix_norm_g.shape[0]
    tok_w = d - MEM_WIDTH
    n_heads = tok_w // HEAD_DIM
    slopes = jnp.asarray(2.0 ** (-8.0 * (np.arange(n_heads) + 1) / n_heads), dtype=F32)

    mem_kv = _norm_matmul(mem.reshape(bsz * mem_len, d), mem_norm_g, w_mem_kv.astype(BF16)).astype(BF16)
    x2d = x.reshape(bsz * s, d)
    for i in range(depth):
        j = i // 2
        if i % 2 == 0:
            proj = _norm_matmul(x2d, mix_norm_g[i], s5_w_in[j].astype(BF16))
            ops = _s5_operators(s5_lambda_re[j], s5_lambda_im[j], s5_log_dt[j],
                                s5_b_re[j], s5_b_im[j], s5_c_re[j], s5_c_im[j])
            y_ssm = _s5_ssm(proj, ops, bsz, s, tok_w)
            x2d = _s5_mix_out(y_ssm, proj, s5_d[j], s5_w_glu[j].astype(BF16), s5_b_glu[j],
                              mem_kv, w_out[i].astype(BF16), x2d, s, mem_len)
        else:
            proj = _norm_matmul(x2d, mix_norm_g[i], moba_w_in[j].astype(BF16))
            tok = _moba_pipe(proj, slopes, bsz, s, n_heads)
            x2d = _moba_mix_out(tok, proj, mem_kv, w_out[i].astype(BF16), x2d, s, mem_len)
        x2d = _ffn(x2d, ffn_norm_g[i], w_gate[i].astype(BF16), w_up[i].astype(BF16),
                   w_down[i].astype(BF16), final_norm_g, final_norm=(i == depth - 1))
    return x2d.reshape(bsz, s, d)
```

```python
import functools
import math

import numpy as np
import jax
import jax.numpy as jnp
from jax import lax
from jax.experimental import pallas as pl
from jax.experimental.pallas import tpu as pltpu

F32 = jnp.float32
BF16 = jnp.bfloat16

HEAD_DIM = 64
MEM_HEADS = 4
MEM_WIDTH = MEM_HEADS * HEAD_DIM
S5_GROUP_CH = 16
S5_STATE = 64
S5_CHUNK = 16
MOBA_BLOCK = 256
MOBA_TOPK = 3
RMS_EPS = 1e-6
NEG_INF = -1e30

LANES = 128
MXU_TILE = 256
VMEM_LIMIT_BYTES = 56 * 1024 * 1024
ROW_TILE = 512

S5_LANE_GROUPS = LANES // S5_GROUP_CH
S5_STEPS_PER_TILE = MXU_TILE // LANES
S5_LAG_TILES = S5_CHUNK // S5_STEPS_PER_TILE


def _params(n_axes, flags=None):
    return pltpu.CompilerParams(dimension_semantics=("arbitrary",) * n_axes,
                                vmem_limit_bytes=VMEM_LIMIT_BYTES, flags=flags)


def _rms_scale(x, g):
    ms = jnp.mean(x * x, axis=-1, keepdims=True)
    return x * lax.rsqrt(ms + RMS_EPS) * g


def _norm_matmul_kernel(x_ref, g_ref, w_ref, o_ref):
    h = _rms_scale(x_ref[...], g_ref[...])
    o_ref[...] = jnp.dot(h.astype(BF16), w_ref[...], preferred_element_type=F32)


def _norm_matmul(x2d, g, w_bf16):
    n, d = x2d.shape
    f = w_bf16.shape[1]
    tm = min(ROW_TILE, n)
    return pl.pallas_call(
        _norm_matmul_kernel,
        grid=(n // tm,),
        in_specs=[pl.BlockSpec((tm, d), lambda i: (i, 0)),
                  pl.BlockSpec((1, d), lambda i: (0, 0)),
                  pl.BlockSpec((d, f), lambda i: (0, 0))],
        out_specs=pl.BlockSpec((tm, f), lambda i: (i, 0)),
        out_shape=jax.ShapeDtypeStruct((n, f), F32),
        compiler_params=_params(1),
        name="norm_matmul",
    )(x2d, g.reshape(1, d), w_bf16)


def _s5_operators(lam_re, lam_im, log_dt, b_re, b_im, c_re, c_im):
    hi = lax.Precision.HIGHEST
    t_len = S5_CHUNK
    g_cnt, p_cnt = lam_re.shape
    c_cnt = b_re.shape[-1]
    gpb = S5_LANE_GROUPS
    n_lb = g_cnt // gpb
    spt = S5_STEPS_PER_TILE
    lr = lam_re.astype(F32)
    li = lam_im.astype(F32)
    dt = jnp.exp(log_dt.astype(F32))[:, None]
    mag = jnp.exp(lr * dt)
    ar = mag * jnp.cos(li * dt)
    ai = mag * jnp.sin(li * dt)
    den = lr * lr + li * li
    nr = ar - 1.0
    fr = (nr * lr + ai * li) / den
    fi = (ai * lr - nr * li) / den
    br = jnp.swapaxes(b_re.astype(F32), 1, 2)
    bi = jnp.swapaxes(b_im.astype(F32), 1, 2)
    bbr = fr[:, None, :] * br - fi[:, None, :] * bi
    bbi = fr[:, None, :] * bi + fi[:, None, :] * br
    cr = c_re.astype(F32)
    ci = c_im.astype(F32)

    k = jnp.arange(t_len + 1, dtype=F32)[:, None, None]
    pm = jnp.exp(k * (lr * dt)[None])
    pr = pm * jnp.cos(k * (li * dt)[None])
    pi = pm * jnp.sin(k * (li * dt)[None])

    wr = pr[:, :, None, :] * bbr[None] - pi[:, :, None, :] * bbi[None]
    wi = pr[:, :, None, :] * bbi[None] + pi[:, :, None, :] * bbr[None]
    kern = (jnp.einsum('gop,kgcp->kgoc', cr, wr[:t_len], precision=hi)
            - jnp.einsum('gop,kgcp->kgoc', ci, wi[:t_len], precision=hi))
    lag = (jnp.arange(S5_LAG_TILES)[:, None, None] * spt
           + jnp.arange(spt)[None, None, :] - jnp.arange(spt)[None, :, None])
    kd = jnp.where((lag >= 0)[..., None, None, None], kern[jnp.clip(lag, 0, t_len - 1)], 0.0)
    kd = kd.reshape(S5_LAG_TILES, spt, spt, n_lb, gpb, c_cnt, c_cnt)
    lag_t = kd.transpose(3, 0, 2, 5, 1, 4, 6).reshape(n_lb, S5_LAG_TILES, spt * c_cnt, spt * gpb * c_cnt)

    w_rev = jnp.concatenate([wr[:t_len][::-1], wi[:t_len][::-1]], axis=-1)
    p_c = w_rev.reshape(t_len, n_lb, gpb, c_cnt, 2 * p_cnt).transpose(1, 0, 2, 3, 4).reshape(
        n_lb, t_len * gpb * c_cnt, 2 * p_cnt)

    qr = cr[None] * pr[1:, :, None, :] - ci[None] * pi[1:, :, None, :]
    qi = -(cr[None] * pi[1:, :, None, :] + ci[None] * pr[1:, :, None, :])
    q_c = jnp.stack([qr, qi]).reshape(2, t_len, n_lb, gpb, c_cnt, p_cnt).transpose(2, 0, 5, 1, 3, 4).reshape(
        n_lb, 2 * p_cnt, t_len * gpb * c_cnt)

    coef = jnp.stack([pr[t_len].reshape(n_lb, gpb * p_cnt),
                      pi[t_len].reshape(n_lb, gpb * p_cnt)], axis=1)
    return lag_t.astype(BF16), p_c.astype(BF16), q_c.astype(BF16), coef


def _iota(shape, dim):
    return lax.broadcasted_iota(jnp.int32, shape, dim)


def _s5_ssm_kernel(u_ref, lag_ref, pc_ref, qc_ref, coef_ref, y_ref,
                   toep_scr, p_scr, q_scr, h_scr, s_scr, xp_scr, *, n_chunks):
    t_len = S5_CHUNK
    gpb = S5_LANE_GROUPS
    st = S5_STATE
    cc = S5_GROUP_CH
    half = gpb * st
    flat_w = t_len * LANES
    nt_dims = (((1,), (1,)), ((), ()))

    @pl.when(pl.program_id(1) == 0)
    def _expand_operators():
        def group_diag(x, row_unit, col_unit):
            same = ((_iota(x.shape, 0) // row_unit) % gpb) == ((_iota(x.shape, 1) // col_unit) % gpb)
            return jnp.where(same, x, 0.0).astype(BF16)

        shape = (2 * st, 2 * half)
        rep = ((_iota(shape, 0) // st == _iota(shape, 1) // half)
               & (_iota(shape, 0) % st == _iota(shape, 1) % st)).astype(BF16)
        for r in range(flat_w // MXU_TILE):
            rows = slice(r * MXU_TILE, (r + 1) * MXU_TILE)
            p_scr[rows, :] = group_diag(jnp.dot(pc_ref[0, rows, :], rep, preferred_element_type=F32), cc, st)
        shape = (2 * half, 2 * st)
        rep = ((_iota(shape, 0) // half == _iota(shape, 1) // st)
               & (_iota(shape, 0) % st == _iota(shape, 1) % st)).astype(BF16)
        for c in range(flat_w // MXU_TILE):
            cols = slice(c * MXU_TILE, (c + 1) * MXU_TILE)
            q_scr[:, cols] = group_diag(jnp.dot(rep, qc_ref[0, :, cols], preferred_element_type=F32), st, cc)
        shape = (MXU_TILE, S5_STEPS_PER_TILE * cc)
        rep = ((_iota(shape, 0) // LANES == _iota(shape, 1) // cc)
               & (_iota(shape, 0) % cc == _iota(shape, 1) % cc)).astype(BF16)
        for l in range(S5_LAG_TILES):
            toep_scr[l] = group_diag(jnp.dot(rep, lag_ref[0, l], preferred_element_type=F32), cc, cc)

    for j in range(t_len):
        h_scr[:, j * LANES:(j + 1) * LANES] = u_ref[pl.ds(j, n_chunks, stride=t_len), :].astype(BF16)
    s_scr[...] = jnp.dot(h_scr[...], p_scr[...], preferred_element_type=F32)
    cr = coef_ref[0, 0:1, :]
    ci = coef_ref[0, 1:2, :]

    def step(n, carry):
        xr, xi = carry
        xp_scr[pl.ds(n, 1), 0:half] = xr
        xp_scr[pl.ds(n, 1), half:2 * half] = xi
        sr = s_scr[pl.ds(n, 1), 0:half]
        si = s_scr[pl.ds(n, 1), half:2 * half]
        return cr * xr - ci * xi + sr, cr * xi + ci * xr + si

    zero = jnp.zeros((1, half), F32)
    lax.fori_loop(0, n_chunks, step, (zero, zero))
    xp = xp_scr[...].astype(BF16)
    for nt in range(S5_LAG_TILES):
        cols = slice(nt * MXU_TILE, (nt + 1) * MXU_TILE)
        acc = jnp.dot(xp, q_scr[:, cols], preferred_element_type=F32)
        for kt in range(nt + 1):
            acc = acc + lax.dot_general(h_scr[:, kt * MXU_TILE:(kt + 1) * MXU_TILE], toep_scr[nt - kt],
                                        nt_dims, preferred_element_type=F32)
        for tt in range(S5_STEPS_PER_TILE):
            t = nt * S5_STEPS_PER_TILE + tt
            y_ref[pl.ds(t, n_chunks, stride=t_len), :] = acc[:, tt * LANES:(tt + 1) * LANES]


def _s5_ssm(proj, ops, bsz, s, tok_w):
    lag_t, p_c, q_c, coef = ops
    n_lb = tok_w // LANES
    n_chunks = s // S5_CHUNK
    flat_w = p_c.shape[1]
    state_w = S5_LANE_GROUPS * p_c.shape[2]
    const3 = lambda l, b: (l, 0, 0)
    return pl.pallas_call(
        functools.partial(_s5_ssm_kernel, n_chunks=n_chunks),
        grid=(n_lb, bsz),
        in_specs=[pl.BlockSpec((s, LANES), lambda l, b: (b, l)),
                  pl.BlockSpec((1,) + lag_t.shape[1:], lambda l, b: (l, 0, 0, 0)),
                  pl.BlockSpec((1,) + p_c.shape[1:], const3),
                  pl.BlockSpec((1,) + q_c.shape[1:], const3),
                  pl.BlockSpec((1,) + coef.shape[1:], const3)],
        out_specs=pl.BlockSpec((s, LANES), lambda l, b: (b, l)),
        out_shape=jax.ShapeDtypeStruct((bsz * s, tok_w), F32),
        scratch_shapes=[pltpu.VMEM((S5_LAG_TILES, MXU_TILE, MXU_TILE), BF16),
                        pltpu.VMEM((flat_w, state_w), BF16),
                        pltpu.VMEM((state_w, flat_w), BF16),
                        pltpu.VMEM((n_chunks, flat_w), BF16),
                        pltpu.VMEM((n_chunks, state_w), F32),
                        pltpu.VMEM((n_chunks, state_w), F32)],
        compiler_params=_params(2),
        name="s5_ssm",
    )(proj, lag_t, p_c, q_c, coef)


MOBA_Q_BLOCKS = 2
MOBA_VT_ROWS = HEAD_DIM + 16
LOG2E = math.log2(math.e)
MOBA_AUG_HI = 8
MOBA_ITEMS_PER_STEP = 4
MOBA_ACCS = 4


def _moba_kernel(slope_ref, q_ref, k_ref, v_ref, o_ref,
                 ka_ref, kb_ref, vt_ref, km_ref, mask_ref, acc_ref, sca_ref, scb_ref, scd_ref, *, n_blocks):
    blk = MOBA_BLOCK
    nq = MOBA_Q_BLOCKS
    tq = nq * blk
    hp = pl.program_id(1)
    qi = pl.program_id(2)
    slopes = (slope_ref[2 * hp] * LOG2E, slope_ref[2 * hp + 1] * LOG2E)
    k_refs = (ka_ref, kb_ref)

    @pl.when(qi == 0)
    def _prepare():
        lane = lax.broadcasted_iota(jnp.int32, (blk, LANES), 1)
        key_off = lax.broadcasted_iota(jnp.int32, (blk, LANES), 0).astype(F32)
        ones_row = (lax.broadcasted_iota(jnp.int32, (MOBA_VT_ROWS - HEAD_DIM, blk), 0) == 0).astype(BF16)
        in_heads = (lane < HEAD_DIM, lane >= HEAD_DIM)
        aug_lane = (lane - HEAD_DIM, lane)
        key_bias = []
        for e in range(2):
            a = slopes[e] * key_off
            a_hi = a.astype(BF16).astype(F32)
            key_bias.append(jnp.where(aug_lane[e] == 0, a_hi, jnp.where(aug_lane[e] == 1, a - a_hi, 0.0)))
        for j in range(n_blocks):
            kj = k_ref[j * blk:(j + 1) * blk, :]
            for e in range(2):
                one_hot = (aug_lane[e] == MOBA_AUG_HI + j) | (aug_lane[e] == MOBA_AUG_HI + n_blocks + j)
                aug = jnp.where(one_hot, 1.0, key_bias[e])
                k_refs[e][j] = jnp.where(in_heads[e], kj, aug).astype(BF16)
            v_t = v_ref[j * blk:(j + 1) * blk, :].T
            for e in range(2):
                vt_ref[j, e, 0:HEAD_DIM, :] = v_t[e * HEAD_DIM:(e + 1) * HEAD_DIM, :].astype(BF16)
                vt_ref[j, e, HEAD_DIM:MOBA_VT_ROWS, :] = ones_row
            km_ref[j:j + 1, :] = jnp.mean(kj, axis=0, keepdims=True)
        d = (lax.broadcasted_iota(jnp.int32, (blk, blk), 1)
             - lax.broadcasted_iota(jnp.int32, (blk, blk), 0))
        causal = jnp.where(d >= 0, 0.0, NEG_INF)
        free = jnp.zeros((blk, blk), F32)
        mask_ref[0] = jnp.concatenate([causal, free], axis=1)
        mask_ref[1] = jnp.concatenate([free, causal], axis=1)

    q_t = (q_ref[...] * (HEAD_DIM ** -0.5 * LOG2E)).T

    km = km_ref[...]
    lane_m = lax.broadcasted_iota(jnp.int32, km.shape, 1)
    blk_id = lax.broadcasted_iota(jnp.int32, (n_blocks, tq), 0)
    q_pos = lax.broadcasted_iota(jnp.int32, (n_blocks, tq), 1)
    q_blk = nq * qi + q_pos // blk
    q_off = (q_pos % blk).astype(F32)
    past = blk_id < q_blk
    q_bf = q_t.astype(BF16)
    ones_rows = (lax.broadcasted_iota(jnp.int32, (MOBA_AUG_HI, tq), 0) < 2).astype(F32)
    pad_rows = jnp.zeros((HEAD_DIM - MOBA_AUG_HI - 2 * n_blocks, tq), F32)
    q_aug = []
    for e in range(2):
        in_head = (lane_m < HEAD_DIM) if e == 0 else (lane_m >= HEAD_DIM)
        km_e = jnp.where(in_head, km, 0.0).astype(BF16)
        gate = jnp.dot(km_e, q_bf, preferred_element_type=F32)
        gate = jnp.where(past, gate, NEG_INF)
        rank = jnp.zeros((n_blocks, tq), jnp.int32)
        for jp in range(n_blocks):
            gj = gate[jp:jp + 1, :]
            beats = (gj > gate) | ((gj == gate) & (jp < blk_id))
            rank = rank + beats.astype(jnp.int32)
        visible = (past & (rank < MOBA_TOPK)) | (blk_id == q_blk)
        bias = -slopes[e] * (((q_blk - blk_id) * blk).astype(F32) + q_off)
        bias = jnp.where(visible, bias, NEG_INF)
        bias_hi = bias.astype(BF16).astype(F32)
        aug = jnp.concatenate([ones_rows, bias_hi, bias - bias_hi, pad_rows], axis=0)
        q_head = q_t[e * HEAD_DIM:(e + 1) * HEAD_DIM, :]
        rows = [q_head, aug] if e == 0 else [aug, q_head]
        q_aug.append(jnp.concatenate(rows, axis=0).astype(BF16))

    def score_chunk(c, sc_ref, on_diag):
        maxes = []
        for e in range(2):
            m_blk = None
            for w in range(nq):
                s_t = jnp.dot(k_refs[e][nq * c + w], q_aug[e], preferred_element_type=F32)
                if on_diag:
                    s_t = s_t + mask_ref[w]
                sc_ref[e, w] = s_t
                mx = jnp.max(s_t, axis=0, keepdims=True)
                m_blk = mx if m_blk is None else jnp.maximum(m_blk, mx)
            maxes.append(m_blk)
        return tuple(maxes)

    def accumulate_chunk(c, sc_ref, m_run, m_blk):
        m_next = []
        for e in range(2):
            m_new = jnp.maximum(m_run[e], m_blk[e])
            pv = None
            for w in range(nq):
                p = jnp.exp2(sc_ref[e, w] - m_new).astype(BF16)
                d = jnp.dot(vt_ref[nq * c + w, e], p, preferred_element_type=F32)
                pv = d if pv is None else pv + d
            acc_ref[e] = jnp.exp2(m_run[e] - m_new) * acc_ref[e] + pv
            m_next.append(m_new)
        return tuple(m_next)

    acc_ref[...] = jnp.zeros_like(acc_ref)
    m_init = jnp.full((1, tq), NEG_INF, F32)
    m_diag = score_chunk(qi, scd_ref, True)
    m_first = score_chunk(0, sca_ref, False)
    m_run = accumulate_chunk(qi, scd_ref, (m_init, m_init), m_diag)
    last = jnp.maximum(qi - 1, 0)

    def body(t, carry):
        m_run, m_blk = carry
        c = 2 * t
        m_blk_b = score_chunk(c + 1, scb_ref, False)
        m_run = accumulate_chunk(c, sca_ref, m_run, m_blk)
        m_blk_a = score_chunk(jnp.minimum(c + 2, last), sca_ref, False)
        m_run = accumulate_chunk(c + 1, scb_ref, m_run, m_blk_b)
        return m_run, m_blk_a

    m_run, m_blk = lax.fori_loop(0, qi // 2, body, (m_run, m_first))

    @pl.when(qi % 2 == 1)
    def _odd_chunk():
        accumulate_chunk(qi - 1, sca_ref, m_run, m_blk)

    out_t = jnp.concatenate([acc_ref[e, 0:HEAD_DIM, :] / acc_ref[e, HEAD_DIM:HEAD_DIM + 1, :]
                             for e in range(2)], axis=0)
    o_ref[...] = out_t.T


def _moba(qkv, slopes, bsz, s, n_heads):
    n_blocks = s // MOBA_BLOCK
    n_pairs = n_heads // 2
    blk = MOBA_BLOCK
    tq = MOBA_Q_BLOCKS * blk
    n_tiles = s // tq
    return pl.pallas_call(
        functools.partial(_moba_kernel, n_blocks=n_blocks),
        grid=(bsz, n_pairs, n_tiles),
        in_specs=[pl.BlockSpec(memory_space=pltpu.SMEM),
                  pl.BlockSpec((tq, LANES), lambda b, h, i: (b * n_tiles + i, h)),
                  pl.BlockSpec((s, LANES), lambda b, h, i: (b, n_pairs + h)),
                  pl.BlockSpec((s, LANES), lambda b, h, i: (b, 2 * n_pairs + h))],
        out_specs=pl.BlockSpec((tq, LANES), lambda b, h, i: (b * n_tiles + i, h)),
        out_shape=jax.ShapeDtypeStruct((bsz * s, n_heads * HEAD_DIM), F32),
        scratch_shapes=[pltpu.VMEM((n_blocks, blk, LANES), BF16),
                        pltpu.VMEM((n_blocks, blk, LANES), BF16),
                        pltpu.VMEM((n_blocks, 2, MOBA_VT_ROWS, blk), BF16),
                        pltpu.VMEM((n_blocks, LANES), F32),
                        pltpu.VMEM((MOBA_Q_BLOCKS, blk, tq), F32),
                        pltpu.VMEM((2, MOBA_VT_ROWS, tq), F32),
                        pltpu.VMEM((2, MOBA_Q_BLOCKS, blk, tq), F32),
                        pltpu.VMEM((2, MOBA_Q_BLOCKS, blk, tq), F32),
                        pltpu.VMEM((2, MOBA_Q_BLOCKS, blk, tq), F32)],
        compiler_params=_params(3),
        name="moba",
    )(slopes, qkv, qkv, qkv)


def _moba_schedule(n_tiles):
    rows = []
    for ti in range(n_tiles):
        seq = [ti] + list(range(ti))
        for n, c in enumerate(seq):
            rows.append((ti, c, int(n == 0), int(n == len(seq) - 1)))
    assert len(rows) % 2 == 0, "the pipelined loop retires two work items per step"
    rows += [(rows[-1][0], rows[-1][1], 0, 0)] * 2
    return np.asarray(rows, np.int32).T.copy()


def _moba_pipe_kernel(slope_ref, sched_ref, q_ref, k_ref, v_ref, o_ref,
                      ka_ref, kb_ref, vt_ref, km_ref, tri_ref, qa_ref, acc_ref,
                      s0_ref, s1_ref, p0_ref, p1_ref, *, n_blocks, n_items):
    blk = MOBA_BLOCK
    nq = MOBA_Q_BLOCKS
    tq = nq * blk
    n_tiles = n_blocks // nq
    hp = pl.program_id(1)
    slopes = (slope_ref[2 * hp] * LOG2E, slope_ref[2 * hp + 1] * LOG2E)
    k_refs = (ka_ref, kb_ref)

    lane = _iota((blk, LANES), 1)
    key_off = _iota((blk, LANES), 0).astype(F32)
    ones_row = (_iota((MOBA_VT_ROWS - HEAD_DIM, blk), 0) == 0).astype(BF16)
    in_heads = (lane < HEAD_DIM, lane >= HEAD_DIM)
    aug_lane = (lane - HEAD_DIM, lane)
    key_bias = []
    for e in range(2):
        a = slopes[e] * key_off
        a_hi = a.astype(BF16).astype(F32)
        key_bias.append(jnp.where(aug_lane[e] == 0, a_hi, jnp.where(aug_lane[e] == 1, a - a_hi, 0.0)))
    for j in range(n_blocks):
        kj = k_ref[j * blk:(j + 1) * blk, :]
        for e in range(2):
            one_hot = (aug_lane[e] == MOBA_AUG_HI + j) | (aug_lane[e] == MOBA_AUG_HI + n_blocks + j)
            aug = jnp.where(one_hot, 1.0, key_bias[e])
            k_refs[e][j] = jnp.where(in_heads[e], kj, aug).astype(BF16)
        v_t = v_ref[j * blk:(j + 1) * blk, :].T
        for e in range(2):
            vt_ref[j, e, 0:HEAD_DIM, :] = v_t[e * HEAD_DIM:(e + 1) * HEAD_DIM, :].astype(BF16)
            vt_ref[j, e, HEAD_DIM:MOBA_VT_ROWS, :] = ones_row
        km_ref[j:j + 1, :] = jnp.mean(kj, axis=0, keepdims=True)
    d = _iota((blk, blk), 1) - _iota((blk, blk), 0)
    tri_ref[0] = jnp.zeros((blk, blk), F32)
    tri_ref[1] = jnp.where(d >= 0, 0.0, NEG_INF)

    km = km_ref[...]
    lane_m = _iota(km.shape, 1)
    blk_id = _iota((n_blocks, tq), 0)
    q_pos = _iota((n_blocks, tq), 1)
    q_off = (q_pos % blk).astype(F32)
    ones_rows = (_iota((MOBA_AUG_HI, tq), 0) < 2).astype(F32)
    pad_rows = jnp.zeros((HEAD_DIM - MOBA_AUG_HI - 2 * n_blocks, tq), F32)

    def gate_tile(ti, carry):
        rows = pl.ds(pl.multiple_of(ti * tq, tq), tq)
        q_t = (q_ref[rows, :] * (HEAD_DIM ** -0.5 * LOG2E)).T
        q_bf = q_t.astype(BF16)
        q_blk = nq * ti + q_pos // blk
        past = blk_id < q_blk
        for e in range(2):
            in_head = (lane_m < HEAD_DIM) if e == 0 else (lane_m >= HEAD_DIM)
            km_e = jnp.where(in_head, km, 0.0).astype(BF16)
            gate = jnp.dot(km_e, q_bf, preferred_element_type=F32)
            gate = jnp.where(past, gate, NEG_INF)
            rank = jnp.zeros((n_blocks, tq), jnp.int32)
            for jp in range(n_blocks):
                gj = gate[jp:jp + 1, :]
                beats = (gj > gate) | ((gj == gate) & (jp < blk_id))
                rank = rank + beats.astype(jnp.int32)
            visible = (past & (rank < MOBA_TOPK)) | (blk_id == q_blk)
            bias = -slopes[e] * (((q_blk - blk_id) * blk).astype(F32) + q_off)
            bias = jnp.where(visible, bias, NEG_INF)
            bias_hi = bias.astype(BF16).astype(F32)
            aug = jnp.concatenate([ones_rows, bias_hi, bias - bias_hi, pad_rows], axis=0)
            q_head = q_t[e * HEAD_DIM:(e + 1) * HEAD_DIM, :]
            parts = [q_head, aug] if e == 0 else [aug, q_head]
            qa_ref[ti, e] = jnp.concatenate(parts, axis=0).astype(BF16)
        return carry

    lax.fori_loop(0, n_tiles, gate_tile, 0)

    def score(w, s_ref):
        ti = sched_ref[0, w]
        c = sched_ref[1, w]
        on_diag = sched_ref[2, w]
        maxes = []
        for e in range(2):
            q_aug = qa_ref[ti, e]
            m_blk = None
            for slab in range(nq):
                s_t = jnp.dot(k_refs[e][nq * c + slab], q_aug, preferred_element_type=F32)
                parts = [s_t[:, h * blk:(h + 1) * blk] + tri_ref[on_diag] if h == slab
                         else s_t[:, h * blk:(h + 1) * blk] for h in range(nq)]
                s_t = jnp.concatenate(parts, axis=1)
                s_ref[e, slab] = s_t
                mx = jnp.max(s_t, axis=0, keepdims=True)
                m_blk = mx if m_blk is None else jnp.maximum(m_blk, mx)
            maxes.append(m_blk)
        return tuple(maxes)

    def exponentiate(s_ref, p_ref, m):
        for e in range(2):
            for slab in range(nq):
                p_ref[e, slab] = jnp.exp2(s_ref[e, slab] - m[e]).astype(BF16)

    def accumulate(w, p_ref, alpha):
        c = sched_ref[1, w]
        par = sched_ref[0, w] % MOBA_ACCS
        for e in range(2):
            pv = None
            for slab in range(nq):
                dd = jnp.dot(vt_ref[nq * c + slab, e], p_ref[e, slab], preferred_element_type=F32)
                pv = dd if pv is None else pv + dd
            acc_ref[par, e] = alpha[e] * acc_ref[par, e] + pv

    def retire(w, carry, s_next, p_next, s_free, p_cur):
        m_prev, m_cur, m_blk_next = carry
        starts_tile = sched_ref[2, w] == 1
        next_starts_tile = sched_ref[2, w + 1] == 1
        m_next = tuple(jnp.where(next_starts_tile, m_blk_next[e], jnp.maximum(m_cur[e], m_blk_next[e]))
                       for e in range(2))
        m_blk_new = score(w + 2, s_free)
        exponentiate(s_next, p_next, m_next)
        alpha = tuple(jnp.exp2(jnp.where(starts_tile, NEG_INF, m_prev[e]) - m_cur[e]) for e in range(2))
        accumulate(w, p_cur, alpha)
        return m_cur, m_next, m_blk_new

    def write_tile_if_done(w):
        @pl.when(sched_ref[3, w] == 1)
        def _write_tile():
            ti = sched_ref[0, w]
            par = ti % MOBA_ACCS
            rows = pl.ds(pl.multiple_of(ti * tq, tq), tq)
            out_t = jnp.concatenate([acc_ref[par, e, 0:HEAD_DIM, :] / acc_ref[par, e, HEAD_DIM:HEAD_DIM + 1, :]
                                     for e in range(2)], axis=0)
            o_ref[rows, :] = out_t.T

    acc_ref[...] = jnp.zeros_like(acc_ref)
    m_first = score(0, s0_ref)
    exponentiate(s0_ref, p0_ref, m_first)
    carry = (m_first, m_first, score(1, s1_ref))

    per_step = MOBA_ITEMS_PER_STEP if n_items % MOBA_ITEMS_PER_STEP == 0 else 2

    def body(t, carry):
        for i in range(per_step):
            bufs = (s1_ref, p1_ref, s0_ref, p0_ref) if i % 2 == 0 else (s0_ref, p0_ref, s1_ref, p1_ref)
            carry = retire(per_step * t + i, carry, *bufs)
        for i in range(per_step):
            write_tile_if_done(per_step * t + i)
        return carry

    lax.fori_loop(0, n_items // per_step, body, carry)


def _moba_pipe(qkv, slopes, bsz, s, n_heads):
    n_blocks = s // MOBA_BLOCK
    n_pairs = n_heads // 2
    blk = MOBA_BLOCK
    tq = MOBA_Q_BLOCKS * blk
    n_tiles = s // tq
    sched = _moba_schedule(n_tiles)
    n_items = sched.shape[1] - 2
    logits = pltpu.VMEM((2, MOBA_Q_BLOCKS, blk, tq), F32)
    probs = pltpu.VMEM((2, MOBA_Q_BLOCKS, blk, tq), BF16)
    return pl.pallas_call(
        functools.partial(_moba_pipe_kernel, n_blocks=n_blocks, n_items=n_items),
        grid=(bsz, n_pairs),
        in_specs=[pl.BlockSpec(memory_space=pltpu.SMEM),
                  pl.BlockSpec(memory_space=pltpu.SMEM),
                  pl.BlockSpec((s, LANES), lambda b, h: (b, h)),
                  pl.BlockSpec((s, LANES), lambda b, h: (b, n_pairs + h)),
                  pl.BlockSpec((s, LANES), lambda b, h: (b, 2 * n_pairs + h))],
        out_specs=pl.BlockSpec((s, LANES), lambda b, h: (b, h)),
        out_shape=jax.ShapeDtypeStruct((bsz * s, n_heads * HEAD_DIM), F32),
        scratch_shapes=[pltpu.VMEM((n_blocks, blk, LANES), BF16),
                        pltpu.VMEM((n_blocks, blk, LANES), BF16),
                        pltpu.VMEM((n_blocks, 2, MOBA_VT_ROWS, blk), BF16),
                        pltpu.VMEM((n_blocks, LANES), F32),
                        pltpu.VMEM((2, blk, blk), F32),
                        pltpu.VMEM((n_tiles, 2, LANES, tq), BF16),
                        pltpu.VMEM((MOBA_ACCS, 2, MOBA_VT_ROWS, tq), F32),
                        logits, logits, probs, probs],
        compiler_params=_params(2),
        name="moba",
    )(slopes, jnp.asarray(sched), qkv, qkv, qkv)


def _memory_attention(q, mem_k, mem_v):
    q = q * (HEAD_DIM ** -0.5)
    lane = lax.broadcasted_iota(jnp.int32, q.shape, 1)
    out = jnp.zeros(q.shape, F32)
    for h in range(MEM_HEADS):
        in_head = (lane >= h * HEAD_DIM) & (lane < (h + 1) * HEAD_DIM)
        qh = jnp.where(in_head, q, 0.0).astype(BF16)
        s = lax.dot_general(qh, mem_k, (((1,), (1,)), ((), ())), preferred_element_type=F32)
        m = jnp.max(s, axis=-1, keepdims=True)
        p = jnp.exp(s - m)
        l = jnp.sum(p, axis=-1, keepdims=True)
        pv = jnp.dot(p.astype(BF16), mem_v, preferred_element_type=F32)
        out = jnp.where(in_head, pv / l, out)
    return out


def _project_out(tok, mem_out, wo_ref, x):
    tw = tok.shape[1]
    return (x + jnp.dot(tok.astype(BF16), wo_ref[0:tw, :], preferred_element_type=F32)
            + jnp.dot(mem_out.astype(BF16), wo_ref[tw:, :], preferred_element_type=F32))


def _s5_mix_out_kernel(y_ref, u_ref, d_ref, wglu_ref, bglu_ref, mq_ref, mk_ref, mv_ref, wo_ref, x_ref, o_ref):
    y = y_ref[...] + d_ref[...] * u_ref[...]
    y = 0.5 * y * (1.0 + lax.erf(y * math.sqrt(0.5)))
    z = jnp.dot(y.astype(BF16), wglu_ref[...], preferred_element_type=F32) + bglu_ref[...]
    tok = y * jax.nn.sigmoid(z)
    mem_out = _memory_attention(mq_ref[...], mk_ref[...], mv_ref[...])
    o_ref[...] = _project_out(tok, mem_out, wo_ref, x_ref[...])


def _moba_mix_out_kernel(tok_ref, mq_ref, mk_ref, mv_ref, wo_ref, x_ref, o_ref):
    mem_out = _memory_attention(mq_ref[...], mk_ref[...], mv_ref[...])
    o_ref[...] = _project_out(tok_ref[...], mem_out, wo_ref, x_ref[...])


def _mix_out_specs(n, s, d, tok_w, mem_len, mq_block_col):
    tm = min(ROW_TILE, s)
    per_b = s // tm
    row = lambda i: (i, 0)
    const = lambda i: (0, 0)
    mem_specs = [pl.BlockSpec((tm, MEM_WIDTH), lambda i: (i, mq_block_col)),
                 pl.BlockSpec((mem_len, MEM_WIDTH), lambda i: (i // per_b, 0)),
                 pl.BlockSpec((mem_len, MEM_WIDTH), lambda i: (i // per_b, 1)),
                 pl.BlockSpec((d, d), const),
                 pl.BlockSpec((tm, d), row)]
    return tm, row, const, mem_specs


def _s5_mix_out(y_ssm, proj, d_skip, w_glu, b_glu, mem_kv, w_out, x2d, s, mem_len):
    n, d = x2d.shape
    tok_w = y_ssm.shape[1]
    tm, row, const, mem_specs = _mix_out_specs(n, s, d, tok_w, mem_len, tok_w // MEM_WIDTH)
    return pl.pallas_call(
        _s5_mix_out_kernel,
        grid=(n // tm,),
        in_specs=[pl.BlockSpec((tm, tok_w), row),
                  pl.BlockSpec((tm, tok_w), row),
                  pl.BlockSpec((1, tok_w), const),
                  pl.BlockSpec((tok_w, tok_w), const),
                  pl.BlockSpec((1, tok_w), const)] + mem_specs,
        out_specs=pl.BlockSpec((tm, d), row),
        out_shape=jax.ShapeDtypeStruct((n, d), F32),
        compiler_params=_params(1),
        name="s5_mix_out",
    )(y_ssm, proj, d_skip.reshape(1, tok_w), w_glu, b_glu.reshape(1, tok_w), proj, mem_kv, mem_kv, w_out, x2d)


def _moba_mix_out(tok, proj, mem_kv, w_out, x2d, s, mem_len):
    n, d = x2d.shape
    tok_w = tok.shape[1]
    tm, row, const, mem_specs = _mix_out_specs(n, s, d, tok_w, mem_len, (proj.shape[1] - MEM_WIDTH) // MEM_WIDTH)
    return pl.pallas_call(
        _moba_mix_out_kernel,
        grid=(n // tm,),
        in_specs=[pl.BlockSpec((tm, tok_w), row)] + mem_specs,
        out_specs=pl.BlockSpec((tm, d), row),
        out_shape=jax.ShapeDtypeStruct((n, d), F32),
        compiler_params=_params(1),
        name="moba_mix_out",
    )(tok, proj, mem_kv, mem_kv, w_out, x2d)


FFN_CHUNK = MXU_TILE


def _ffn_kernel(x_ref, g_ref, wg_ref, wu_ref, wd_ref, gf_ref, o_ref, act_ref, *, final_norm):
    x = x_ref[...]
    h = _rms_scale(x, g_ref[...]).astype(BF16)
    hidden = wg_ref.shape[1]
    for c in range(hidden // FFN_CHUNK):
        cols = slice(c * FFN_CHUNK, (c + 1) * FFN_CHUNK)
        gate = jnp.dot(h, wg_ref[:, cols], preferred_element_type=F32)
        up = jnp.dot(h, wu_ref[:, cols], preferred_element_type=F32)
        act_ref[:, cols] = (gate * jax.nn.sigmoid(gate) * up).astype(BF16)
    y = x + jnp.dot(act_ref[...], wd_ref[...], preferred_element_type=F32)
    if final_norm:
        y = _rms_scale(y, gf_ref[...])
    o_ref[...] = y


def _ffn(x2d, g, w_gate, w_up, w_down, g_final, final_norm):
    n, d = x2d.shape
    hidden = w_gate.shape[1]
    tm = min(ROW_TILE, n)
    row = lambda i: (i, 0)
    const = lambda i: (0, 0)
    return pl.pallas_call(
        functools.partial(_ffn_kernel, final_norm=final_norm),
        grid=(n // tm,),
        in_specs=[pl.BlockSpec((tm, d), row),
                  pl.BlockSpec((1, d), const),
                  pl.BlockSpec((d, hidden), const),
                  pl.BlockSpec((d, hidden), const),
                  pl.BlockSpec((hidden, d), const),
                  pl.BlockSpec((1, d), const)],
        out_specs=pl.BlockSpec((tm, d), row),
        out_shape=jax.ShapeDtypeStruct((n, d), F32),
        scratch_shapes=[pltpu.VMEM((tm, hidden), BF16)],
        compiler_params=_params(1),
        name="ffn",
    )(x2d, g.reshape(1, d), w_gate, w_up, w_down, g_final.reshape(1, d))


def kernel(x, mem, mem_norm_g, w_mem_kv, mix_norm_g, s5_w_in, s5_lambda_re, s5_lambda_im, s5_log_dt,
           s5_b_re, s5_b_im, s5_c_re, s5_c_im, s5_d, s5_w_glu, s5_b_glu, moba_w_in, w_out, ffn_norm_g,
           w_gate, w_up, w_down, final_norm_g):
    bsz, s, d = x.shape
    mem_len = mem.shape[1]
    depth = mix_norm_g.shape[0]
    tok_w = d - MEM_WIDTH
    n_heads = tok_w // HEAD_DIM
    slopes = jnp.asarray(2.0 ** (-8.0 * (np.arange(n_heads) + 1) / n_heads), dtype=F32)

    mem_kv = _norm_matmul(mem.reshape(bsz * mem_len, d), mem_norm_g, w_mem_kv.astype(BF16)).astype(BF16)
    x2d = x.reshape(bsz * s, d)
    for i in range(depth):
        j = i // 2
        if i % 2 == 0:
            proj = _norm_matmul(x2d, mix_norm_g[i], s5_w_in[j].astype(BF16))
            ops = _s5_operators(s5_lambda_re[j], s5_lambda_im[j], s5_log_dt[j],
                                s5_b_re[j], s5_b_im[j], s5_c_re[j], s5_c_im[j])
            y_ssm = _s5_ssm(proj, ops, bsz, s, tok_w)
            x2d = _s5_mix_out(y_ssm, proj, s5_d[j], s5_w_glu[j].astype(BF16), s5_b_glu[j],
                              mem_kv, w_out[i].astype(BF16), x2d, s, mem_len)
        else:
            proj = _norm_matmul(x2d, mix_norm_g[i], moba_w_in[j].astype(BF16))
            tok = _moba_pipe(proj, slopes, bsz, s, n_heads)
            x2d = _moba_mix_out(tok, proj, mem_kv, w_out[i].astype(BF16), x2d, s, mem_len)
        x2d = _ffn(x2d, ffn_norm_g[i], w_gate[i].astype(BF16), w_up[i].astype(BF16),
                   w_down[i].astype(BF16), final_norm_g, final_norm=(i == depth - 1))
    return x2d.reshape(bsz, s, d)
```

```python
import functools
import math

import numpy as np
import jax
import jax.numpy as jnp
from jax import lax
from jax.experimental import pallas as pl
from jax.experimental.pallas import tpu as pltpu

F32 = jnp.float32
BF16 = jnp.bfloat16

HEAD_DIM = 64
MEM_HEADS = 4
MEM_WIDTH = MEM_HEADS * HEAD_DIM
S5_GROUP_CH = 16
S5_STATE = 64
S5_CHUNK = 16
MOBA_BLOCK = 256
MOBA_TOPK = 3
RMS_EPS = 1e-6
NEG_INF = -1e30

LANES = 128
MXU_TILE = 256
VMEM_LIMIT_BYTES = 56 * 1024 * 1024
ROW_TILE = 512

S5_LANE_GROUPS = LANES // S5_GROUP_CH
S5_STEPS_PER_TILE = MXU_TILE // LANES
S5_LAG_TILES = S5_CHUNK // S5_STEPS_PER_TILE


def _params(n_axes, flags=None):
    return pltpu.CompilerParams(dimension_semantics=("arbitrary",) * n_axes,
                                vmem_limit_bytes=VMEM_LIMIT_BYTES, flags=flags)


def _rms_scale(x, g):
    ms = jnp.mean(x * x, axis=-1, keepdims=True)
    return x * lax.rsqrt(ms + RMS_EPS) * g


def _norm_matmul_kernel(x_ref, g_ref, w_ref, o_ref):
    h = _rms_scale(x_ref[...], g_ref[...])
    o_ref[...] = jnp.dot(h.astype(BF16), w_ref[...], preferred_element_type=F32)


def _norm_matmul(x2d, g, w_bf16):
    n, d = x2d.shape
    f = w_bf16.shape[1]
    tm = min(ROW_TILE, n)
    return pl.pallas_call(
        _norm_matmul_kernel,
        grid=(n // tm,),
        in_specs=[pl.BlockSpec((tm, d), lambda i: (i, 0)),
                  pl.BlockSpec((1, d), lambda i: (0, 0)),
                  pl.BlockSpec((d, f), lambda i: (0, 0))],
        out_specs=pl.BlockSpec((tm, f), lambda i: (i, 0)),
        out_shape=jax.ShapeDtypeStruct((n, f), F32),
        compiler_params=_params(1),
        name="norm_matmul",
    )(x2d, g.reshape(1, d), w_bf16)


def _s5_operators(lam_re, lam_im, log_dt, b_re, b_im, c_re, c_im):
    hi = lax.Precision.HIGHEST
    t_len = S5_CHUNK
    g_cnt, p_cnt = lam_re.shape
    c_cnt = b_re.shape[-1]
    gpb = S5_LANE_GROUPS
    n_lb = g_cnt // gpb
    spt = S5_STEPS_PER_TILE
    lr = lam_re.astype(F32)
    li = lam_im.astype(F32)
    dt = jnp.exp(log_dt.astype(F32))[:, None]
    mag = jnp.exp(lr * dt)
    ar = mag * jnp.cos(li * dt)
    ai = mag * jnp.sin(li * dt)
    den = lr * lr + li * li
    nr = ar - 1.0
    fr = (nr * lr + ai * li) / den
    fi = (ai * lr - nr * li) / den
    br = jnp.swapaxes(b_re.astype(F32), 1, 2)
    bi = jnp.swapaxes(b_im.astype(F32), 1, 2)
    bbr = fr[:, None, :] * br - fi[:, None, :] * bi
    bbi = fr[:, None, :] * bi + fi[:, None, :] * br
    cr = c_re.astype(F32)
    ci = c_im.astype(F32)

    k = jnp.arange(t_len + 1, dtype=F32)[:, None, None]
    pm = jnp.exp(k * (lr * dt)[None])
    pr = pm * jnp.cos(k * (li * dt)[None])
    pi = pm * jnp.sin(k * (li * dt)[None])

    wr = pr[:, :, None, :] * bbr[None] - pi[:, :, None, :] * bbi[None]
    wi = pr[:, :, None, :] * bbi[None] + pi[:, :, None, :] * bbr[None]
    kern = (jnp.einsum('gop,kgcp->kgoc', cr, wr[:t_len], precision=hi)
            - jnp.einsum('gop,kgcp->kgoc', ci, wi[:t_len], precision=hi))
    lag = (jnp.arange(S5_LAG_TILES)[:, None, None] * spt
           + jnp.arange(spt)[None, None, :] - jnp.arange(spt)[None, :, None])
    kd = jnp.where((lag >= 0)[..., None, None, None], kern[jnp.clip(lag, 0, t_len - 1)], 0.0)
    kd = kd.reshape(S5_LAG_TILES, spt, spt, n_lb, gpb, c_cnt, c_cnt)
    lag_t = kd.transpose(3, 0, 2, 5, 1, 4, 6).reshape(n_lb, S5_LAG_TILES, spt * c_cnt, spt * gpb * c_cnt)

    w_rev = jnp.concatenate([wr[:t_len][::-1], wi[:t_len][::-1]], axis=-1)
    p_c = w_rev.reshape(t_len, n_lb, gpb, c_cnt, 2 * p_cnt).transpose(1, 0, 2, 3, 4).reshape(
        n_lb, t_len * gpb * c_cnt, 2 * p_cnt)

    qr = cr[None] * pr[1:, :, None, :] - ci[None] * pi[1:, :, None, :]
    qi = -(cr[None] * pi[1:, :, None, :] + ci[None] * pr[1:, :, None, :])
    q_c = jnp.stack([qr, qi]).reshape(2, t_len, n_lb, gpb, c_cnt, p_cnt).transpose(2, 0, 5, 1, 3, 4).reshape(
        n_lb, 2 * p_cnt, t_len * gpb * c_cnt)

    coef = jnp.stack([pr[t_len].reshape(n_lb, gpb * p_cnt),
                      pi[t_len].reshape(n_lb, gpb * p_cnt)], axis=1)
    return lag_t.astype(BF16), p_c.astype(BF16), q_c.astype(BF16), coef


def _iota(shape, dim):
    return lax.broadcasted_iota(jnp.int32, shape, dim)


def _s5_ssm_kernel(u_ref, lag_ref, pc_ref, qc_ref, coef_ref, y_ref,
                   toep_scr, p_scr, q_scr, h_scr, s_scr, xp_scr, *, n_chunks):
    t_len = S5_CHUNK
    gpb = S5_LANE_GROUPS
    st = S5_STATE
    cc = S5_GROUP_CH
    half = gpb * st
    flat_w = t_len * LANES
    nt_dims = (((1,), (1,)), ((), ()))

    @pl.when(pl.program_id(1) == 0)
    def _expand_operators():
        def group_diag(x, row_unit, col_unit):
            same = ((_iota(x.shape, 0) // row_unit) % gpb) == ((_iota(x.shape, 1) // col_unit) % gpb)
            return jnp.where(same, x, 0.0).astype(BF16)

        shape = (2 * st, 2 * half)
        rep = ((_iota(shape, 0) // st == _iota(shape, 1) // half)
               & (_iota(shape, 0) % st == _iota(shape, 1) % st)).astype(BF16)
        for r in range(flat_w // MXU_TILE):
            rows = slice(r * MXU_TILE, (r + 1) * MXU_TILE)
            p_scr[rows, :] = group_diag(jnp.dot(pc_ref[0, rows, :], rep, preferred_element_type=F32), cc, st)
        shape = (2 * half, 2 * st)
        rep = ((_iota(shape, 0) // half == _iota(shape, 1) // st)
               & (_iota(shape, 0) % st == _iota(shape, 1) % st)).astype(BF16)
        for c in range(flat_w // MXU_TILE):
            cols = slice(c * MXU_TILE, (c + 1) * MXU_TILE)
            q_scr[:, cols] = group_diag(jnp.dot(rep, qc_ref[0, :, cols], preferred_element_type=F32), st, cc)
        shape = (MXU_TILE, S5_STEPS_PER_TILE * cc)
        rep = ((_iota(shape, 0) // LANES == _iota(shape, 1) // cc)
               & (_iota(shape, 0) % cc == _iota(shape, 1) % cc)).astype(BF16)
        for l in range(S5_LAG_TILES):
            toep_scr[l] = group_diag(jnp.dot(rep, lag_ref[0, l], preferred_element_type=F32), cc, cc)

    for j in range(t_len):
        h_scr[:, j * LANES:(j + 1) * LANES] = u_ref[pl.ds(j, n_chunks, stride=t_len), :].astype(BF16)
    s_scr[...] = jnp.dot(h_scr[...], p_scr[...], preferred_element_type=F32)
    cr = coef_ref[0, 0:1, :]
    ci = coef_ref[0, 1:2, :]

    def step(n, carry):
        xr, xi = carry
        xp_scr[pl.ds(n, 1), 0:half] = xr
        xp_scr[pl.ds(n, 1), half:2 * half] = xi
        sr = s_scr[pl.ds(n, 1), 0:half]
        si = s_scr[pl.ds(n, 1), half:2 * half]
        return cr * xr - ci * xi + sr, cr * xi + ci * xr + si

    zero = jnp.zeros((1, half), F32)
    lax.fori_loop(0, n_chunks, step, (zero, zero))
    xp = xp_scr[...].astype(BF16)
    for nt in range(S5_LAG_TILES):
        cols = slice(nt * MXU_TILE, (nt + 1) * MXU_TILE)
        acc = jnp.dot(xp, q_scr[:, cols], preferred_element_type=F32)
        for kt in range(nt + 1):
            acc = acc + lax.dot_general(h_scr[:, kt * MXU_TILE:(kt + 1) * MXU_TILE], toep_scr[nt - kt],
                                        nt_dims, preferred_element_type=F32)
        for tt in range(S5_STEPS_PER_TILE):
            t = nt * S5_STEPS_PER_TILE + tt
            y_ref[pl.ds(t, n_chunks, stride=t_len), :] = acc[:, tt * LANES:(tt + 1) * LANES]


def _s5_ssm(proj, ops, bsz, s, tok_w):
    lag_t, p_c, q_c, coef = ops
    n_lb = tok_w // LANES
    n_chunks = s // S5_CHUNK
    flat_w = p_c.shape[1]
    state_w = S5_LANE_GROUPS * p_c.shape[2]
    const3 = lambda l, b: (l, 0, 0)
    return pl.pallas_call(
        functools.partial(_s5_ssm_kernel, n_chunks=n_chunks),
        grid=(n_lb, bsz),
        in_specs=[pl.BlockSpec((s, LANES), lambda l, b: (b, l)),
                  pl.BlockSpec((1,) + lag_t.shape[1:], lambda l, b: (l, 0, 0, 0)),
                  pl.BlockSpec((1,) + p_c.shape[1:], const3),
                  pl.BlockSpec((1,) + q_c.shape[1:], const3),
                  pl.BlockSpec((1,) + coef.shape[1:], const3)],
        out_specs=pl.BlockSpec((s, LANES), lambda l, b: (b, l)),
        out_shape=jax.ShapeDtypeStruct((bsz * s, tok_w), F32),
        scratch_shapes=[pltpu.VMEM((S5_LAG_TILES, MXU_TILE, MXU_TILE), BF16),
                        pltpu.VMEM((flat_w, state_w), BF16),
                        pltpu.VMEM((state_w, flat_w), BF16),
                        pltpu.VMEM((n_chunks, flat_w), BF16),
                        pltpu.VMEM((n_chunks, state_w), F32),
                        pltpu.VMEM((n_chunks, state_w), F32)],
        compiler_params=_params(2),
        name="s5_ssm",
    )(proj, lag_t, p_c, q_c, coef)


MOBA_Q_BLOCKS = 2
MOBA_VT_ROWS = HEAD_DIM + 16
LOG2E = math.log2(math.e)
MOBA_AUG_HI = 8
MOBA_ITEMS_PER_STEP = 12


def _moba_kernel(slope_ref, q_ref, k_ref, v_ref, o_ref,
                 ka_ref, kb_ref, vt_ref, km_ref, mask_ref, acc_ref, sca_ref, scb_ref, scd_ref, *, n_blocks):
    blk = MOBA_BLOCK
    nq = MOBA_Q_BLOCKS
    tq = nq * blk
    hp = pl.program_id(1)
    qi = pl.program_id(2)
    slopes = (slope_ref[2 * hp] * LOG2E, slope_ref[2 * hp + 1] * LOG2E)
    k_refs = (ka_ref, kb_ref)

    @pl.when(qi == 0)
    def _prepare():
        lane = lax.broadcasted_iota(jnp.int32, (blk, LANES), 1)
        key_off = lax.broadcasted_iota(jnp.int32, (blk, LANES), 0).astype(F32)
        ones_row = (lax.broadcasted_iota(jnp.int32, (MOBA_VT_ROWS - HEAD_DIM, blk), 0) == 0).astype(BF16)
        in_heads = (lane < HEAD_DIM, lane >= HEAD_DIM)
        aug_lane = (lane - HEAD_DIM, lane)
        key_bias = []
        for e in range(2):
            a = slopes[e] * key_off
            a_hi = a.astype(BF16).astype(F32)
            key_bias.append(jnp.where(aug_lane[e] == 0, a_hi, jnp.where(aug_lane[e] == 1, a - a_hi, 0.0)))
        for j in range(n_blocks):
            kj = k_ref[j * blk:(j + 1) * blk, :]
            for e in range(2):
                one_hot = (aug_lane[e] == MOBA_AUG_HI + j) | (aug_lane[e] == MOBA_AUG_HI + n_blocks + j)
                aug = jnp.where(one_hot, 1.0, key_bias[e])
                k_refs[e][j] = jnp.where(in_heads[e], kj, aug).astype(BF16)
            v_t = v_ref[j * blk:(j + 1) * blk, :].T
            for e in range(2):
                vt_ref[j, e, 0:HEAD_DIM, :] = v_t[e * HEAD_DIM:(e + 1) * HEAD_DIM, :].astype(BF16)
                vt_ref[j, e, HEAD_DIM:MOBA_VT_ROWS, :] = ones_row
            km_ref[j:j + 1, :] = jnp.mean(kj, axis=0, keepdims=True)
        d = (lax.broadcasted_iota(jnp.int32, (blk, blk), 1)
             - lax.broadcasted_iota(jnp.int32, (blk, blk), 0))
        causal = jnp.where(d >= 0, 0.0, NEG_INF)
        free = jnp.zeros((blk, blk), F32)
        mask_ref[0] = jnp.concatenate([causal, free], axis=1)
        mask_ref[1] = jnp.concatenate([free, causal], axis=1)

    q_t = (q_ref[...] * (HEAD_DIM ** -0.5 * LOG2E)).T

    km = km_ref[...]
    lane_m = lax.broadcasted_iota(jnp.int32, km.shape, 1)
    blk_id = lax.broadcasted_iota(jnp.int32, (n_blocks, tq), 0)
    q_pos = lax.broadcasted_iota(jnp.int32, (n_blocks, tq), 1)
    q_blk = nq * qi + q_pos // blk
    q_off = (q_pos % blk).astype(F32)
    past = blk_id < q_blk
    q_bf = q_t.astype(BF16)
    ones_rows = (lax.broadcasted_iota(jnp.int32, (MOBA_AUG_HI, tq), 0) < 2).astype(F32)
    pad_rows = jnp.zeros((HEAD_DIM - MOBA_AUG_HI - 2 * n_blocks, tq), F32)
    q_aug = []
    for e in range(2):
        in_head = (lane_m < HEAD_DIM) if e == 0 else (lane_m >= HEAD_DIM)
        km_e = jnp.where(in_head, km, 0.0).astype(BF16)
        gate = jnp.dot(km_e, q_bf, preferred_element_type=F32)
        gate = jnp.where(past, gate, NEG_INF)
        rank = jnp.zeros((n_blocks, tq), jnp.int32)
        for jp in range(n_blocks):
            gj = gate[jp:jp + 1, :]
            beats = (gj > gate) | ((gj == gate) & (jp < blk_id))
            rank = rank + beats.astype(jnp.int32)
        visible = (past & (rank < MOBA_TOPK)) | (blk_id == q_blk)
        bias = -slopes[e] * (((q_blk - blk_id) * blk).astype(F32) + q_off)
        bias = jnp.where(visible, bias, NEG_INF)
        bias_hi = bias.astype(BF16).astype(F32)
        aug = jnp.concatenate([ones_rows, bias_hi, bias - bias_hi, pad_rows], axis=0)
        q_head = q_t[e * HEAD_DIM:(e + 1) * HEAD_DIM, :]
        rows = [q_head, aug] if e == 0 else [aug, q_head]
        q_aug.append(jnp.concatenate(rows, axis=0).astype(BF16))

    def score_chunk(c, sc_ref, on_diag):
        maxes = []
        for e in range(2):
            m_blk = None
            for w in range(nq):
                s_t = jnp.dot(k_refs[e][nq * c + w], q_aug[e], preferred_element_type=F32)
                if on_diag:
                    s_t = s_t + mask_ref[w]
                sc_ref[e, w] = s_t
                mx = jnp.max(s_t, axis=0, keepdims=True)
                m_blk = mx if m_blk is None else jnp.maximum(m_blk, mx)
            maxes.append(m_blk)
        return tuple(maxes)

    def accumulate_chunk(c, sc_ref, m_run, m_blk):
        m_next = []
        for e in range(2):
            m_new = jnp.maximum(m_run[e], m_blk[e])
            pv = None
            for w in range(nq):
                p = jnp.exp2(sc_ref[e, w] - m_new).astype(BF16)
                d = jnp.dot(vt_ref[nq * c + w, e], p, preferred_element_type=F32)
                pv = d if pv is None else pv + d
            acc_ref[e] = jnp.exp2(m_run[e] - m_new) * acc_ref[e] + pv
            m_next.append(m_new)
        return tuple(m_next)

    acc_ref[...] = jnp.zeros_like(acc_ref)
    m_init = jnp.full((1, tq), NEG_INF, F32)
    m_diag = score_chunk(qi, scd_ref, True)
    m_first = score_chunk(0, sca_ref, False)
    m_run = accumulate_chunk(qi, scd_ref, (m_init, m_init), m_diag)
    last = jnp.maximum(qi - 1, 0)

    def body(t, carry):
        m_run, m_blk = carry
        c = 2 * t
        m_blk_b = score_chunk(c + 1, scb_ref, False)
        m_run = accumulate_chunk(c, sca_ref, m_run, m_blk)
        m_blk_a = score_chunk(jnp.minimum(c + 2, last), sca_ref, False)
        m_run = accumulate_chunk(c + 1, scb_ref, m_run, m_blk_b)
        return m_run, m_blk_a

    m_run, m_blk = lax.fori_loop(0, qi // 2, body, (m_run, m_first))

    @pl.when(qi % 2 == 1)
    def _odd_chunk():
        accumulate_chunk(qi - 1, sca_ref, m_run, m_blk)

    out_t = jnp.concatenate([acc_ref[e, 0:HEAD_DIM, :] / acc_ref[e, HEAD_DIM:HEAD_DIM + 1, :]
                             for e in range(2)], axis=0)
    o_ref[...] = out_t.T


def _moba(qkv, slopes, bsz, s, n_heads):
    n_blocks = s // MOBA_BLOCK
    n_pairs = n_heads // 2
    blk = MOBA_BLOCK
    tq = MOBA_Q_BLOCKS * blk
    n_tiles = s // tq
    return pl.pallas_call(
        functools.partial(_moba_kernel, n_blocks=n_blocks),
        grid=(bsz, n_pairs, n_tiles),
        in_specs=[pl.BlockSpec(memory_space=pltpu.SMEM),
                  pl.BlockSpec((tq, LANES), lambda b, h, i: (b * n_tiles + i, h)),
                  pl.BlockSpec((s, LANES), lambda b, h, i: (b, n_pairs + h)),
                  pl.BlockSpec((s, LANES), lambda b, h, i: (b, 2 * n_pairs + h))],
        out_specs=pl.BlockSpec((tq, LANES), lambda b, h, i: (b * n_tiles + i, h)),
        out_shape=jax.ShapeDtypeStruct((bsz * s, n_heads * HEAD_DIM), F32),
        scratch_shapes=[pltpu.VMEM((n_blocks, blk, LANES), BF16),
                        pltpu.VMEM((n_blocks, blk, LANES), BF16),
                        pltpu.VMEM((n_blocks, 2, MOBA_VT_ROWS, blk), BF16),
                        pltpu.VMEM((n_blocks, LANES), F32),
                        pltpu.VMEM((MOBA_Q_BLOCKS, blk, tq), F32),
                        pltpu.VMEM((2, MOBA_VT_ROWS, tq), F32),
                        pltpu.VMEM((2, MOBA_Q_BLOCKS, blk, tq), F32),
                        pltpu.VMEM((2, MOBA_Q_BLOCKS, blk, tq), F32),
                        pltpu.VMEM((2, MOBA_Q_BLOCKS, blk, tq), F32)],
        compiler_params=_params(3),
        name="moba",
    )(slopes, qkv, qkv, qkv)


def _moba_schedule(n_tiles):
    rows = []
    for ti in range(n_tiles):
        seq = [ti] + list(range(ti))
        for n, c in enumerate(seq):
            rows.append((ti, c, int(n == 0), int(n == len(seq) - 1)))
    assert len(rows) % 2 == 0, "the pipelined loop retires two work items per step"
    rows += [(rows[-1][0], rows[-1][1], 0, 0)] * 2
    return np.asarray(rows, np.int32).T.copy()


def _moba_pipe_kernel(slope_ref, sched_ref, q_ref, k_ref, v_ref, o_ref,
                      ka_ref, kb_ref, vt_ref, km_ref, tri_ref, qa_ref, acc_ref,
                      s0_ref, s1_ref, p0_ref, p1_ref, *, n_blocks, n_items):
    blk = MOBA_BLOCK
    nq = MOBA_Q_BLOCKS
    tq = nq * blk
    n_tiles = n_blocks // nq
    hp = pl.program_id(1)
    slopes = (slope_ref[2 * hp] * LOG2E, slope_ref[2 * hp + 1] * LOG2E)
    k_refs = (ka_ref, kb_ref)

    lane = _iota((blk, LANES), 1)
    key_off = _iota((blk, LANES), 0).astype(F32)
    ones_row = (_iota((MOBA_VT_ROWS - HEAD_DIM, blk), 0) == 0).astype(BF16)
    in_heads = (lane < HEAD_DIM, lane >= HEAD_DIM)
    aug_lane = (lane - HEAD_DIM, lane)
    key_bias = []
    for e in range(2):
        a = slopes[e] * key_off
        a_hi = a.astype(BF16).astype(F32)
        key_bias.append(jnp.where(aug_lane[e] == 0, a_hi, jnp.where(aug_lane[e] == 1, a - a_hi, 0.0)))
    for j in range(n_blocks):
        kj = k_ref[j * blk:(j + 1) * blk, :]
        for e in range(2):
            one_hot = (aug_lane[e] == MOBA_AUG_HI + j) | (aug_lane[e] == MOBA_AUG_HI + n_blocks + j)
            aug = jnp.where(one_hot, 1.0, key_bias[e])
            k_refs[e][j] = jnp.where(in_heads[e], kj, aug).astype(BF16)
        v_t = v_ref[j * blk:(j + 1) * blk, :].T
        for e in range(2):
            vt_ref[j, e, 0:HEAD_DIM, :] = v_t[e * HEAD_DIM:(e + 1) * HEAD_DIM, :].astype(BF16)
            vt_ref[j, e, HEAD_DIM:MOBA_VT_ROWS, :] = ones_row
        km_ref[j:j + 1, :] = jnp.mean(kj, axis=0, keepdims=True)
    d = _iota((blk, blk), 1) - _iota((blk, blk), 0)
    tri_ref[0] = jnp.zeros((blk, blk), F32)
    tri_ref[1] = jnp.where(d >= 0, 0.0, NEG_INF)

    km = km_ref[...]
    lane_m = _iota(km.shape, 1)
    blk_id = _iota((n_blocks, tq), 0)
    q_pos = _iota((n_blocks, tq), 1)
    q_off = (q_pos % blk).astype(F32)
    ones_rows = (_iota((MOBA_AUG_HI, tq), 0) < 2).astype(F32)
    pad_rows = jnp.zeros((HEAD_DIM - MOBA_AUG_HI - 2 * n_blocks, tq), F32)

    def gate_tile(ti, carry):
        rows = pl.ds(pl.multiple_of(ti * tq, tq), tq)
        q_t = (q_ref[rows, :] * (HEAD_DIM ** -0.5 * LOG2E)).T
        q_bf = q_t.astype(BF16)
        q_blk = nq * ti + q_pos // blk
        past = blk_id < q_blk
        for e in range(2):
            in_head = (lane_m < HEAD_DIM) if e == 0 else (lane_m >= HEAD_DIM)
            km_e = jnp.where(in_head, km, 0.0).astype(BF16)
            gate = jnp.dot(km_e, q_bf, preferred_element_type=F32)
            gate = jnp.where(past, gate, NEG_INF)
            rank = jnp.zeros((n_blocks, tq), jnp.int32)
            for jp in range(n_blocks):
                gj = gate[jp:jp + 1, :]
                beats = (gj > gate) | ((gj == gate) & (jp < blk_id))
                rank = rank + beats.astype(jnp.int32)
            visible = (past & (rank < MOBA_TOPK)) | (blk_id == q_blk)
            bias = -slopes[e] * (((q_blk - blk_id) * blk).astype(F32) + q_off)
            bias = jnp.where(visible, bias, NEG_INF)
            bias_hi = bias.astype(BF16).astype(F32)
            aug = jnp.concatenate([ones_rows, bias_hi, bias - bias_hi, pad_rows], axis=0)
            q_head = q_t[e * HEAD_DIM:(e + 1) * HEAD_DIM, :]
            parts = [q_head, aug] if e == 0 else [aug, q_head]
            qa_ref[ti, e] = jnp.concatenate(parts, axis=0).astype(BF16)
        return carry

    lax.fori_loop(0, n_tiles, gate_tile, 0)

    def score(w, s_ref):
        ti = sched_ref[0, w]
        c = sched_ref[1, w]
        on_diag = sched_ref[2, w]
        maxes = []
        for e in range(2):
            q_aug = qa_ref[ti, e]
            m_blk = None
            for slab in range(nq):
                s_t = jnp.dot(k_refs[e][nq * c + slab], q_aug, preferred_element_type=F32)
                parts = [s_t[:, h * blk:(h + 1) * blk] + tri_ref[on_diag] if h == slab
                         else s_t[:, h * blk:(h + 1) * blk] for h in range(nq)]
                s_t = jnp.concatenate(parts, axis=1)
                s_ref[e, slab] = s_t
                mx = jnp.max(s_t, axis=0, keepdims=True)
                m_blk = mx if m_blk is None else jnp.maximum(m_blk, mx)
            maxes.append(m_blk)
        return tuple(maxes)

    def exponentiate(s_ref, p_ref, m):
        for e in range(2):
            for slab in range(nq):
                p_ref[e, slab] = jnp.exp2(s_ref[e, slab] - m[e]).astype(BF16)

    def accumulate(w, p_ref, alpha):
        c = sched_ref[1, w]
        ti = sched_ref[0, w]
        for e in range(2):
            pv = None
            for slab in range(nq):
                dd = jnp.dot(vt_ref[nq * c + slab, e], p_ref[e, slab], preferred_element_type=F32)
                pv = dd if pv is None else pv + dd
            acc_ref[ti, e] = alpha[e] * acc_ref[ti, e] + pv

    def retire(w, carry, s_next, p_next, s_free, p_cur):
        m_prev, m_cur, m_blk_next = carry
        starts_tile = sched_ref[2, w] == 1
        next_starts_tile = sched_ref[2, w + 1] == 1
        m_next = tuple(jnp.where(next_starts_tile, m_blk_next[e], jnp.maximum(m_cur[e], m_blk_next[e]))
                       for e in range(2))
        m_blk_new = score(w + 2, s_free)
        exponentiate(s_next, p_next, m_next)
        alpha = tuple(jnp.exp2(jnp.where(starts_tile, NEG_INF, m_prev[e]) - m_cur[e]) for e in range(2))
        accumulate(w, p_cur, alpha)
        return m_cur, m_next, m_blk_new

    acc_ref[...] = jnp.zeros_like(acc_ref)
    m_first = score(0, s0_ref)
    exponentiate(s0_ref, p0_ref, m_first)
    carry = (m_first, m_first, score(1, s1_ref))

    per_step = MOBA_ITEMS_PER_STEP if n_items % MOBA_ITEMS_PER_STEP == 0 else 2

    def body(t, carry):
        for i in range(per_step):
            bufs = (s1_ref, p1_ref, s0_ref, p0_ref) if i % 2 == 0 else (s0_ref, p0_ref, s1_ref, p1_ref)
            carry = retire(per_step * t + i, carry, *bufs)
        return carry

    lax.fori_loop(0, n_items // per_step, body, carry)

    for ti in range(n_tiles):
        out_t = jnp.concatenate([acc_ref[ti, e, 0:HEAD_DIM, :] / acc_ref[ti, e, HEAD_DIM:HEAD_DIM + 1, :]
                                 for e in range(2)], axis=0)
        o_ref[ti * tq:(ti + 1) * tq, :] = out_t.T


def _moba_pipe(qkv, slopes, bsz, s, n_heads):
    n_blocks = s // MOBA_BLOCK
    n_pairs = n_heads // 2
    blk = MOBA_BLOCK
    tq = MOBA_Q_BLOCKS * blk
    n_tiles = s // tq
    sched = _moba_schedule(n_tiles)
    n_items = sched.shape[1] - 2
    logits = pltpu.VMEM((2, MOBA_Q_BLOCKS, blk, tq), F32)
    probs = pltpu.VMEM((2, MOBA_Q_BLOCKS, blk, tq), BF16)
    return pl.pallas_call(
        functools.partial(_moba_pipe_kernel, n_blocks=n_blocks, n_items=n_items),
        grid=(bsz, n_pairs),
        in_specs=[pl.BlockSpec(memory_space=pltpu.SMEM),
                  pl.BlockSpec(memory_space=pltpu.SMEM),
                  pl.BlockSpec((s, LANES), lambda b, h: (b, h)),
                  pl.BlockSpec((s, LANES), lambda b, h: (b, n_pairs + h)),
                  pl.BlockSpec((s, LANES), lambda b, h: (b, 2 * n_pairs + h))],
        out_specs=pl.BlockSpec((s, LANES), lambda b, h: (b, h)),
        out_shape=jax.ShapeDtypeStruct((bsz * s, n_heads * HEAD_DIM), F32),
        scratch_shapes=[pltpu.VMEM((n_blocks, blk, LANES), BF16),
                        pltpu.VMEM((n_blocks, blk, LANES), BF16),
                        pltpu.VMEM((n_blocks, 2, MOBA_VT_ROWS, blk), BF16),
                        pltpu.VMEM((n_blocks, LANES), F32),
                        pltpu.VMEM((2, blk, blk), F32),
                        pltpu.VMEM((n_tiles, 2, LANES, tq), BF16),
                        pltpu.VMEM((n_tiles, 2, MOBA_VT_ROWS, tq), F32),
                        logits, logits, probs, probs],
        compiler_params=_params(2),
        name="moba",
    )(slopes, jnp.asarray(sched), qkv, qkv, qkv)


def _memory_attention(q, mem_k, mem_v):
    q = q * (HEAD_DIM ** -0.5)
    lane = lax.broadcasted_iota(jnp.int32, q.shape, 1)
    out = jnp.zeros(q.shape, F32)
    for h in range(MEM_HEADS):
        in_head = (lane >= h * HEAD_DIM) & (lane < (h + 1) * HEAD_DIM)
        qh = jnp.where(in_head, q, 0.0).astype(BF16)
        s = lax.dot_general(qh, mem_k, (((1,), (1,)), ((), ())), preferred_element_type=F32)
        m = jnp.max(s, axis=-1, keepdims=True)
        p = jnp.exp(s - m)
        l = jnp.sum(p, axis=-1, keepdims=True)
        pv = jnp.dot(p.astype(BF16), mem_v, preferred_element_type=F32)
        out = jnp.where(in_head, pv / l, out)
    return out


def _project_out(tok, mem_out, wo_ref, x):
    tw = tok.shape[1]
    return (x + jnp.dot(tok.astype(BF16), wo_ref[0:tw, :], preferred_element_type=F32)
            + jnp.dot(mem_out.astype(BF16), wo_ref[tw:, :], preferred_element_type=F32))


def _s5_mix_out_kernel(y_ref, u_ref, d_ref, wglu_ref, bglu_ref, mq_ref, mk_ref, mv_ref, wo_ref, x_ref, o_ref):
    y = y_ref[...] + d_ref[...] * u_ref[...]
    y = 0.5 * y * (1.0 + lax.erf(y * math.sqrt(0.5)))
    z = jnp.dot(y.astype(BF16), wglu_ref[...], preferred_element_type=F32) + bglu_ref[...]
    tok = y * jax.nn.sigmoid(z)
    mem_out = _memory_attention(mq_ref[...], mk_ref[...], mv_ref[...])
    o_ref[...] = _project_out(tok, mem_out, wo_ref, x_ref[...])


def _moba_mix_out_kernel(tok_ref, mq_ref, mk_ref, mv_ref, wo_ref, x_ref, o_ref):
    mem_out = _memory_attention(mq_ref[...], mk_ref[...], mv_ref[...])
    o_ref[...] = _project_out(tok_ref[...], mem_out, wo_ref, x_ref[...])


def _mix_out_specs(n, s, d, tok_w, mem_len, mq_block_col):
    tm = min(ROW_TILE, s)
    per_b = s // tm
    row = lambda i: (i, 0)
    const = lambda i: (0, 0)
    mem_specs = [pl.BlockSpec((tm, MEM_WIDTH), lambda i: (i, mq_block_col)),
                 pl.BlockSpec((mem_len, MEM_WIDTH), lambda i: (i // per_b, 0)),
                 pl.BlockSpec((mem_len, MEM_WIDTH), lambda i: (i // per_b, 1)),
                 pl.BlockSpec((d, d), const),
                 pl.BlockSpec((tm, d), row)]
    return tm, row, const, mem_specs


def _s5_mix_out(y_ssm, proj, d_skip, w_glu, b_glu, mem_kv, w_out, x2d, s, mem_len):
    n, d = x2d.shape
    tok_w = y_ssm.shape[1]
    tm, row, const, mem_specs = _mix_out_specs(n, s, d, tok_w, mem_len, tok_w // MEM_WIDTH)
    return pl.pallas_call(
        _s5_mix_out_kernel,
        grid=(n // tm,),
        in_specs=[pl.BlockSpec((tm, tok_w), row),
                  pl.BlockSpec((tm, tok_w), row),
                  pl.BlockSpec((1, tok_w), const),
                  pl.BlockSpec((tok_w, tok_w), const),
                  pl.BlockSpec((1, tok_w), const)] + mem_specs,
        out_specs=pl.BlockSpec((tm, d), row),
        out_shape=jax.ShapeDtypeStruct((n, d), F32),
        compiler_params=_params(1),
        name="s5_mix_out",
    )(y_ssm, proj, d_skip.reshape(1, tok_w), w_glu, b_glu.reshape(1, tok_w), proj, mem_kv, mem_kv, w_out, x2d)


def _moba_mix_out(tok, proj, mem_kv, w_out, x2d, s, mem_len):
    n, d = x2d.shape
    tok_w = tok.shape[1]
    tm, row, const, mem_specs = _mix_out_specs(n, s, d, tok_w, mem_len, (proj.shape[1] - MEM_WIDTH) // MEM_WIDTH)
    return pl.pallas_call(
        _moba_mix_out_kernel,
        grid=(n // tm,),
        in_specs=[pl.BlockSpec((tm, tok_w), row)] + mem_specs,
        out_specs=pl.BlockSpec((tm, d), row),
        out_shape=jax.ShapeDtypeStruct((n, d), F32),
        compiler_params=_params(1),
        name="moba_mix_out",
    )(tok, proj, mem_kv, mem_kv, w_out, x2d)


FFN_CHUNK = MXU_TILE


def _ffn_kernel(x_ref, g_ref, wg_ref, wu_ref, wd_ref, gf_ref, o_ref, act_ref, *, final_norm):
    x = x_ref[...]
    h = _rms_scale(x, g_ref[...]).astype(BF16)
    hidden = wg_ref.shape[1]
    for c in range(hidden // FFN_CHUNK):
        cols = slice(c * FFN_CHUNK, (c + 1) * FFN_CHUNK)
        gate = jnp.dot(h, wg_ref[:, cols], preferred_element_type=F32)
        up = jnp.dot(h, wu_ref[:, cols], preferred_element_type=F32)
        act_ref[:, cols] = (gate * jax.nn.sigmoid(gate) * up).astype(BF16)
    y = x + jnp.dot(act_ref[...], wd_ref[...], preferred_element_type=F32)
    if final_norm:
        y = _rms_scale(y, gf_ref[...])
    o_ref[...] = y


def _ffn(x2d, g, w_gate, w_up, w_down, g_final, final_norm):
    n, d = x2d.shape
    hidden = w_gate.shape[1]
    tm = min(ROW_TILE, n)
    row = lambda i: (i, 0)
    const = lambda i: (0, 0)
    return pl.pallas_call(
        functools.partial(_ffn_kernel, final_norm=final_norm),
        grid=(n // tm,),
        in_specs=[pl.BlockSpec((tm, d), row),
                  pl.BlockSpec((1, d), const),
                  pl.BlockSpec((d, hidden), const),
                  pl.BlockSpec((d, hidden), const),
                  pl.BlockSpec((hidden, d), const),
                  pl.BlockSpec((1, d), const)],
        out_specs=pl.BlockSpec((tm, d), row),
        out_shape=jax.ShapeDtypeStruct((n, d), F32),
        scratch_shapes=[pltpu.VMEM((tm, hidden), BF16)],
        compiler_params=_params(1),
        name="ffn",
    )(x2d, g.reshape(1, d), w_gate, w_up, w_down, g_final.reshape(1, d))


def kernel(x, mem, mem_norm_g, w_mem_kv, mix_norm_g, s5_w_in, s5_lambda_re, s5_lambda_im, s5_log_dt,
           s5_b_re, s5_b_im, s5_c_re, s5_c_im, s5_d, s5_w_glu, s5_b_glu, moba_w_in, w_out, ffn_norm_g,
           w_gate, w_up, w_down, final_norm_g):
    bsz, s, d = x.shape
    mem_len = mem.shape[1]
    depth = mix_norm_g.shape[0]
    tok_w = d - MEM_WIDTH
    n_heads = tok_w // HEAD_DIM
    slopes = jnp.asarray(2.0 ** (-8.0 * (np.arange(n_heads) + 1) / n_heads), dtype=F32)

    mem_kv = _norm_matmul(mem.reshape(bsz * mem_len, d), mem_norm_g, w_mem_kv.astype(BF16)).astype(BF16)
    x2d = x.reshape(bsz * s, d)
    for i in range(depth):
        j = i // 2
        if i % 2 == 0:
            proj = _norm_matmul(x2d, mix_norm_g[i], s5_w_in[j].astype(BF16))
            ops = _s5_operators(s5_lambda_re[j], s5_lambda_im[j], s5_log_dt[j],
                                s5_b_re[j], s5_b_im[j], s5_c_re[j], s5_c_im[j])
            y_ssm = _s5_ssm(proj, ops, bsz, s, tok_w)
            x2d = _s5_mix_out(y_ssm, proj, s5_d[j], s5_w_glu[j].astype(BF16), s5_b_glu[j],
                              mem_kv, w_out[i].astype(BF16), x2d, s, mem_len)
        else:
            proj = _norm_matmul(x2d, mix_norm_g[i], moba_w_in[j].astype(BF16))
            tok = _moba_pipe(proj, slopes, bsz, s, n_heads)
            x2d = _moba_mix_out(tok, proj, mem_kv, w_out[i].astype(BF16), x2d, s, mem_len)
        x2d = _ffn(x2d, ffn_norm_g[i], w_gate[i].astype(BF16), w_up[i].astype(BF16),
                   w_down[i].astype(BF16), final_norm_g, final_norm=(i == depth - 1))
    return x2d.reshape(bsz, s, d)
```

```python
import functools
import math

import numpy as np
import jax
import jax.numpy as jnp
from jax import lax
from jax.experimental import pallas as pl
from jax.experimental.pallas import tpu as pltpu

F32 = jnp.float32
BF16 = jnp.bfloat16

HEAD_DIM = 64
MEM_HEADS = 4
MEM_WIDTH = MEM_HEADS * HEAD_DIM
S5_GROUP_CH = 16
S5_STATE = 64
S5_CHUNK = 16
MOBA_BLOCK = 256
MOBA_TOPK = 3
RMS_EPS = 1e-6
NEG_INF = -1e30

LANES = 128
MXU_TILE = 256
VMEM_LIMIT_BYTES = 56 * 1024 * 1024
ROW_TILE = 512

S5_LANE_GROUPS = LANES // S5_GROUP_CH
S5_STEPS_PER_TILE = MXU_TILE // LANES
S5_LAG_TILES = S5_CHUNK // S5_STEPS_PER_TILE


def _params(n_axes, flags=None):
    return pltpu.CompilerParams(dimension_semantics=("arbitrary",) * n_axes,
                                vmem_limit_bytes=VMEM_LIMIT_BYTES, flags=flags)


def _rms_scale(x, g):
    ms = jnp.mean(x * x, axis=-1, keepdims=True)
    return x * lax.rsqrt(ms + RMS_EPS) * g


def _norm_matmul_kernel(x_ref, g_ref, w_ref, o_ref):
    h = _rms_scale(x_ref[...], g_ref[...])
    o_ref[...] = jnp.dot(h.astype(BF16), w_ref[...], preferred_element_type=F32)


def _norm_matmul(x2d, g, w_bf16):
    n, d = x2d.shape
    f = w_bf16.shape[1]
    tm = min(ROW_TILE, n)
    return pl.pallas_call(
        _norm_matmul_kernel,
        grid=(n // tm,),
        in_specs=[pl.BlockSpec((tm, d), lambda i: (i, 0)),
                  pl.BlockSpec((1, d), lambda i: (0, 0)),
                  pl.BlockSpec((d, f), lambda i: (0, 0))],
        out_specs=pl.BlockSpec((tm, f), lambda i: (i, 0)),
        out_shape=jax.ShapeDtypeStruct((n, f), F32),
        compiler_params=_params(1),
        name="norm_matmul",
    )(x2d, g.reshape(1, d), w_bf16)


def _s5_operators(lam_re, lam_im, log_dt, b_re, b_im, c_re, c_im):
    hi = lax.Precision.HIGHEST
    t_len = S5_CHUNK
    g_cnt, p_cnt = lam_re.shape
    c_cnt = b_re.shape[-1]
    gpb = S5_LANE_GROUPS
    n_lb = g_cnt // gpb
    spt = S5_STEPS_PER_TILE
    lr = lam_re.astype(F32)
    li = lam_im.astype(F32)
    dt = jnp.exp(log_dt.astype(F32))[:, None]
    mag = jnp.exp(lr * dt)
    ar = mag * jnp.cos(li * dt)
    ai = mag * jnp.sin(li * dt)
    den = lr * lr + li * li
    nr = ar - 1.0
    fr = (nr * lr + ai * li) / den
    fi = (ai * lr - nr * li) / den
    br = jnp.swapaxes(b_re.astype(F32), 1, 2)
    bi = jnp.swapaxes(b_im.astype(F32), 1, 2)
    bbr = fr[:, None, :] * br - fi[:, None, :] * bi
    bbi = fr[:, None, :] * bi + fi[:, None, :] * br
    cr = c_re.astype(F32)
    ci = c_im.astype(F32)

    k = jnp.arange(t_len + 1, dtype=F32)[:, None, None]
    pm = jnp.exp(k * (lr * dt)[None])
    pr = pm * jnp.cos(k * (li * dt)[None])
    pi = pm * jnp.sin(k * (li * dt)[None])

    wr = pr[:, :, None, :] * bbr[None] - pi[:, :, None, :] * bbi[None]
    wi = pr[:, :, None, :] * bbi[None] + pi[:, :, None, :] * bbr[None]
    kern = (jnp.einsum('gop,kgcp->kgoc', cr, wr[:t_len], precision=hi)
            - jnp.einsum('gop,kgcp->kgoc', ci, wi[:t_len], precision=hi))
    lag = (jnp.arange(S5_LAG_TILES)[:, None, None] * spt
           + jnp.arange(spt)[None, None, :] - jnp.arange(spt)[None, :, None])
    kd = jnp.where((lag >= 0)[..., None, None, None], kern[jnp.clip(lag, 0, t_len - 1)], 0.0)
    kd = kd.reshape(S5_LAG_TILES, spt, spt, n_lb, gpb, c_cnt, c_cnt)
    lag_t = kd.transpose(3, 0, 2, 5, 1, 4, 6).reshape(n_lb, S5_LAG_TILES, spt * c_cnt, spt * gpb * c_cnt)

    w_rev = jnp.concatenate([wr[:t_len][::-1], wi[:t_len][::-1]], axis=-1)
    p_c = w_rev.reshape(t_len, n_lb, gpb, c_cnt, 2 * p_cnt).transpose(1, 0, 2, 3, 4).reshape(
        n_lb, t_len * gpb * c_cnt, 2 * p_cnt)

    qr = cr[None] * pr[1:, :, None, :] - ci[None] * pi[1:, :, None, :]
    qi = -(cr[None] * pi[1:, :, None, :] + ci[None] * pr[1:, :, None, :])
    q_c = jnp.stack([qr, qi]).reshape(2, t_len, n_lb, gpb, c_cnt, p_cnt).transpose(2, 0, 5, 1, 3, 4).reshape(
        n_lb, 2 * p_cnt, t_len * gpb * c_cnt)

    coef = jnp.stack([pr[t_len].reshape(n_lb, gpb * p_cnt),
                      pi[t_len].reshape(n_lb, gpb * p_cnt)], axis=1)
    return lag_t.astype(BF16), p_c.astype(BF16), q_c.astype(BF16), coef


def _iota(shape, dim):
    return lax.broadcasted_iota(jnp.int32, shape, dim)


def _s5_ssm_kernel(u_ref, lag_ref, pc_ref, qc_ref, coef_ref, y_ref,
                   toep_scr, p_scr, q_scr, h_scr, s_scr, xp_scr, *, n_chunks):
    t_len = S5_CHUNK
    gpb = S5_LANE_GROUPS
    st = S5_STATE
    cc = S5_GROUP_CH
    half = gpb * st
    flat_w = t_len * LANES
    nt_dims = (((1,), (1,)), ((), ()))

    @pl.when(pl.program_id(1) == 0)
    def _expand_operators():
        def group_diag(x, row_unit, col_unit):
            same = ((_iota(x.shape, 0) // row_unit) % gpb) == ((_iota(x.shape, 1) // col_unit) % gpb)
            return jnp.where(same, x, 0.0).astype(BF16)

        shape = (2 * st, 2 * half)
        rep = ((_iota(shape, 0) // st == _iota(shape, 1) // half)
               & (_iota(shape, 0) % st == _iota(shape, 1) % st)).astype(BF16)
        for r in range(flat_w // MXU_TILE):
            rows = slice(r * MXU_TILE, (r + 1) * MXU_TILE)
            p_scr[rows, :] = group_diag(jnp.dot(pc_ref[0, rows, :], rep, preferred_element_type=F32), cc, st)
        shape = (2 * half, 2 * st)
        rep = ((_iota(shape, 0) // half == _iota(shape, 1) // st)
               & (_iota(shape, 0) % st == _iota(shape, 1) % st)).astype(BF16)
        for c in range(flat_w // MXU_TILE):
            cols = slice(c * MXU_TILE, (c + 1) * MXU_TILE)
            q_scr[:, cols] = group_diag(jnp.dot(rep, qc_ref[0, :, cols], preferred_element_type=F32), st, cc)
        shape = (MXU_TILE, S5_STEPS_PER_TILE * cc)
        rep = ((_iota(shape, 0) // LANES == _iota(shape, 1) // cc)
               & (_iota(shape, 0) % cc == _iota(shape, 1) % cc)).astype(BF16)
        for l in range(S5_LAG_TILES):
            toep_scr[l] = group_diag(jnp.dot(rep, lag_ref[0, l], preferred_element_type=F32), cc, cc)

    for j in range(t_len):
        h_scr[:, j * LANES:(j + 1) * LANES] = u_ref[pl.ds(j, n_chunks, stride=t_len), :].astype(BF16)
    s_scr[...] = jnp.dot(h_scr[...], p_scr[...], preferred_element_type=F32)
    cr = coef_ref[0, 0:1, :]
    ci = coef_ref[0, 1:2, :]

    def step(n, carry):
        xr, xi = carry
        xp_scr[pl.ds(n, 1), 0:half] = xr
        xp_scr[pl.ds(n, 1), half:2 * half] = xi
        sr = s_scr[pl.ds(n, 1), 0:half]
        si = s_scr[pl.ds(n, 1), half:2 * half]
        return cr * xr - ci * xi + sr, cr * xi + ci * xr + si

    zero = jnp.zeros((1, half), F32)
    lax.fori_loop(0, n_chunks, step, (zero, zero))
    xp = xp_scr[...].astype(BF16)
    for nt in range(S5_LAG_TILES):
        cols = slice(nt * MXU_TILE, (nt + 1) * MXU_TILE)
        acc = jnp.dot(xp, q_scr[:, cols], preferred_element_type=F32)
        for kt in range(nt + 1):
            acc = acc + lax.dot_general(h_scr[:, kt * MXU_TILE:(kt + 1) * MXU_TILE], toep_scr[nt - kt],
                                        nt_dims, preferred_element_type=F32)
        for tt in range(S5_STEPS_PER_TILE):
            t = nt * S5_STEPS_PER_TILE + tt
            y_ref[pl.ds(t, n_chunks, stride=t_len), :] = acc[:, tt * LANES:(tt + 1) * LANES]


def _s5_ssm(proj, ops, bsz, s, tok_w):
    lag_t, p_c, q_c, coef = ops
    n_lb = tok_w // LANES
    n_chunks = s // S5_CHUNK
    flat_w = p_c.shape[1]
    state_w = S5_LANE_GROUPS * p_c.shape[2]
    const3 = lambda l, b: (l, 0, 0)
    return pl.pallas_call(
        functools.partial(_s5_ssm_kernel, n_chunks=n_chunks),
        grid=(n_lb, bsz),
        in_specs=[pl.BlockSpec((s, LANES), lambda l, b: (b, l)),
                  pl.BlockSpec((1,) + lag_t.shape[1:], lambda l, b: (l, 0, 0, 0)),
                  pl.BlockSpec((1,) + p_c.shape[1:], const3),
                  pl.BlockSpec((1,) + q_c.shape[1:], const3),
                  pl.BlockSpec((1,) + coef.shape[1:], const3)],
        out_specs=pl.BlockSpec((s, LANES), lambda l, b: (b, l)),
        out_shape=jax.ShapeDtypeStruct((bsz * s, tok_w), F32),
        scratch_shapes=[pltpu.VMEM((S5_LAG_TILES, MXU_TILE, MXU_TILE), BF16),
                        pltpu.VMEM((flat_w, state_w), BF16),
                        pltpu.VMEM((state_w, flat_w), BF16),
                        pltpu.VMEM((n_chunks, flat_w), BF16),
                        pltpu.VMEM((n_chunks, state_w), F32),
                        pltpu.VMEM((n_chunks, state_w), F32)],
        compiler_params=_params(2),
        name="s5_ssm",
    )(proj, lag_t, p_c, q_c, coef)


MOBA_Q_BLOCKS = 2
MOBA_VT_ROWS = HEAD_DIM + 16
LOG2E = math.log2(math.e)
MOBA_AUG_HI = 8
MOBA_ITEMS_PER_STEP = 12


def _moba_kernel(slope_ref, q_ref, k_ref, v_ref, o_ref,
                 ka_ref, kb_ref, vt_ref, km_ref, mask_ref, acc_ref, sca_ref, scb_ref, scd_ref, *, n_blocks):
    blk = MOBA_BLOCK
    nq = MOBA_Q_BLOCKS
    tq = nq * blk
    hp = pl.program_id(1)
    qi = pl.program_id(2)
    slopes = (slope_ref[2 * hp] * LOG2E, slope_ref[2 * hp + 1] * LOG2E)
    k_refs = (ka_ref, kb_ref)

    @pl.when(qi == 0)
    def _prepare():
        lane = lax.broadcasted_iota(jnp.int32, (blk, LANES), 1)
        key_off = lax.broadcasted_iota(jnp.int32, (blk, LANES), 0).astype(F32)
        ones_row = (lax.broadcasted_iota(jnp.int32, (MOBA_VT_ROWS - HEAD_DIM, blk), 0) == 0).astype(BF16)
        in_heads = (lane < HEAD_DIM, lane >= HEAD_DIM)
        aug_lane = (lane - HEAD_DIM, lane)
        key_bias = []
        for e in range(2):
            a = slopes[e] * key_off
            a_hi = a.astype(BF16).astype(F32)
            key_bias.append(jnp.where(aug_lane[e] == 0, a_hi, jnp.where(aug_lane[e] == 1, a - a_hi, 0.0)))
        for j in range(n_blocks):
            kj = k_ref[j * blk:(j + 1) * blk, :]
            for e in range(2):
                one_hot = (aug_lane[e] == MOBA_AUG_HI + j) | (aug_lane[e] == MOBA_AUG_HI + n_blocks + j)
                aug = jnp.where(one_hot, 1.0, key_bias[e])
                k_refs[e][j] = jnp.where(in_heads[e], kj, aug).astype(BF16)
            v_t = v_ref[j * blk:(j + 1) * blk, :].T
            for e in range(2):
                vt_ref[j, e, 0:HEAD_DIM, :] = v_t[e * HEAD_DIM:(e + 1) * HEAD_DIM, :].astype(BF16)
                vt_ref[j, e, HEAD_DIM:MOBA_VT_ROWS, :] = ones_row
            km_ref[j:j + 1, :] = jnp.mean(kj, axis=0, keepdims=True)
        d = (lax.broadcasted_iota(jnp.int32, (blk, blk), 1)
             - lax.broadcasted_iota(jnp.int32, (blk, blk), 0))
        causal = jnp.where(d >= 0, 0.0, NEG_INF)
        free = jnp.zeros((blk, blk), F32)
        mask_ref[0] = jnp.concatenate([causal, free], axis=1)
        mask_ref[1] = jnp.concatenate([free, causal], axis=1)

    q_t = (q_ref[...] * (HEAD_DIM ** -0.5 * LOG2E)).T

    km = km_ref[...]
    lane_m = lax.broadcasted_iota(jnp.int32, km.shape, 1)
    blk_id = lax.broadcasted_iota(jnp.int32, (n_blocks, tq), 0)
    q_pos = lax.broadcasted_iota(jnp.int32, (n_blocks, tq), 1)
    q_blk = nq * qi + q_pos // blk
    q_off = (q_pos % blk).astype(F32)
    past = blk_id < q_blk
    q_bf = q_t.astype(BF16)
    ones_rows = (lax.broadcasted_iota(jnp.int32, (MOBA_AUG_HI, tq), 0) < 2).astype(F32)
    pad_rows = jnp.zeros((HEAD_DIM - MOBA_AUG_HI - 2 * n_blocks, tq), F32)
    q_aug = []
    for e in range(2):
        in_head = (lane_m < HEAD_DIM) if e == 0 else (lane_m >= HEAD_DIM)
        km_e = jnp.where(in_head, km, 0.0).astype(BF16)
        gate = jnp.dot(km_e, q_bf, preferred_element_type=F32)
        gate = jnp.where(past, gate, NEG_INF)
        rank = jnp.zeros((n_blocks, tq), jnp.int32)
        for jp in range(n_blocks):
            gj = gate[jp:jp + 1, :]
            beats = (gj > gate) | ((gj == gate) & (jp < blk_id))
            rank = rank + beats.astype(jnp.int32)
        visible = (past & (rank < MOBA_TOPK)) | (blk_id == q_blk)
        bias = -slopes[e] * (((q_blk - blk_id) * blk).astype(F32) + q_off)
        bias = jnp.where(visible, bias, NEG_INF)
        bias_hi = bias.astype(BF16).astype(F32)
        aug = jnp.concatenate([ones_rows, bias_hi, bias - bias_hi, pad_rows], axis=0)
        q_head = q_t[e * HEAD_DIM:(e + 1) * HEAD_DIM, :]
        rows = [q_head, aug] if e == 0 else [aug, q_head]
        q_aug.append(jnp.concatenate(rows, axis=0).astype(BF16))

    def score_chunk(c, sc_ref, on_diag):
        maxes = []
        for e in range(2):
            m_blk = None
            for w in range(nq):
                s_t = jnp.dot(k_refs[e][nq * c + w], q_aug[e], preferred_element_type=F32)
                if on_diag:
                    s_t = s_t + mask_ref[w]
                sc_ref[e, w] = s_t
                mx = jnp.max(s_t, axis=0, keepdims=True)
                m_blk = mx if m_blk is None else jnp.maximum(m_blk, mx)
            maxes.append(m_blk)
        return tuple(maxes)

    def accumulate_chunk(c, sc_ref, m_run, m_blk):
        m_next = []
        for e in range(2):
            m_new = jnp.maximum(m_run[e], m_blk[e])
            pv = None
            for w in range(nq):
                p = jnp.exp2(sc_ref[e, w] - m_new).astype(BF16)
                d = jnp.dot(vt_ref[nq * c + w, e], p, preferred_element_type=F32)
                pv = d if pv is None else pv + d
            acc_ref[e] = jnp.exp2(m_run[e] - m_new) * acc_ref[e] + pv
            m_next.append(m_new)
        return tuple(m_next)

    acc_ref[...] = jnp.zeros_like(acc_ref)
    m_init = jnp.full((1, tq), NEG_INF, F32)
    m_diag = score_chunk(qi, scd_ref, True)
    m_first = score_chunk(0, sca_ref, False)
    m_run = accumulate_chunk(qi, scd_ref, (m_init, m_init), m_diag)
    last = jnp.maximum(qi - 1, 0)

    def body(t, carry):
        m_run, m_blk = carry
        c = 2 * t
        m_blk_b = score_chunk(c + 1, scb_ref, False)
        m_run = accumulate_chunk(c, sca_ref, m_run, m_blk)
        m_blk_a = score_chunk(jnp.minimum(c + 2, last), sca_ref, False)
        m_run = accumulate_chunk(c + 1, scb_ref, m_run, m_blk_b)
        return m_run, m_blk_a

    m_run, m_blk = lax.fori_loop(0, qi // 2, body, (m_run, m_first))

    @pl.when(qi % 2 == 1)
    def _odd_chunk():
        accumulate_chunk(qi - 1, sca_ref, m_run, m_blk)

    out_t = jnp.concatenate([acc_ref[e, 0:HEAD_DIM, :] / acc_ref[e, HEAD_DIM:HEAD_DIM + 1, :]
                             for e in range(2)], axis=0)
    o_ref[...] = out_t.T


def _moba(qkv, slopes, bsz, s, n_heads):
    n_blocks = s // MOBA_BLOCK
    n_pairs = n_heads // 2
    blk = MOBA_BLOCK
    tq = MOBA_Q_BLOCKS * blk
    n_tiles = s // tq
    return pl.pallas_call(
        functools.partial(_moba_kernel, n_blocks=n_blocks),
        grid=(bsz, n_pairs, n_tiles),
        in_specs=[pl.BlockSpec(memory_space=pltpu.SMEM),
                  pl.BlockSpec((tq, LANES), lambda b, h, i: (b * n_tiles + i, h)),
                  pl.BlockSpec((s, LANES), lambda b, h, i: (b, n_pairs + h)),
                  pl.BlockSpec((s, LANES), lambda b, h, i: (b, 2 * n_pairs + h))],
        out_specs=pl.BlockSpec((tq, LANES), lambda b, h, i: (b * n_tiles + i, h)),
        out_shape=jax.ShapeDtypeStruct((bsz * s, n_heads * HEAD_DIM), F32),
        scratch_shapes=[pltpu.VMEM((n_blocks, blk, LANES), BF16),
                        pltpu.VMEM((n_blocks, blk, LANES), BF16),
                        pltpu.VMEM((n_blocks, 2, MOBA_VT_ROWS, blk), BF16),
                        pltpu.VMEM((n_blocks, LANES), F32),
                        pltpu.VMEM((MOBA_Q_BLOCKS, blk, tq), F32),
                        pltpu.VMEM((2, MOBA_VT_ROWS, tq), F32),
                        pltpu.VMEM((2, MOBA_Q_BLOCKS, blk, tq), F32),
                        pltpu.VMEM((2, MOBA_Q_BLOCKS, blk, tq), F32),
                        pltpu.VMEM((2, MOBA_Q_BLOCKS, blk, tq), F32)],
        compiler_params=_params(3),
        name="moba",
    )(slopes, qkv, qkv, qkv)


def _moba_schedule(n_tiles):
    rows = []
    for ti in range(n_tiles):
        seq = [ti] + list(range(ti))
        for n, c in enumerate(seq):
            rows.append((ti, c, int(n == 0), int(n == len(seq) - 1)))
    assert len(rows) % 2 == 0, "the pipelined loop retires two work items per step"
    rows += [(rows[-1][0], rows[-1][1], 0, 0)] * 2
    return np.asarray(rows, np.int32).T.copy()


def _moba_pipe_kernel(slope_ref, sched_ref, q_ref, k_ref, v_ref, o_ref,
                      ka_ref, kb_ref, vt_ref, km_ref, tri_ref, qt_ref, qa_ref, acc_ref,
                      s0_ref, s1_ref, p0_ref, p1_ref, *, n_blocks, n_items):
    blk = MOBA_BLOCK
    nq = MOBA_Q_BLOCKS
    tq = nq * blk
    n_tiles = n_blocks // nq
    hp = pl.program_id(1)
    slopes = (slope_ref[2 * hp] * LOG2E, slope_ref[2 * hp + 1] * LOG2E)
    k_refs = (ka_ref, kb_ref)

    lane = _iota((blk, LANES), 1)
    key_off = _iota((blk, LANES), 0).astype(F32)
    ones_row = (_iota((MOBA_VT_ROWS - HEAD_DIM, blk), 0) == 0).astype(BF16)
    in_heads = (lane < HEAD_DIM, lane >= HEAD_DIM)
    aug_lane = (lane - HEAD_DIM, lane)
    key_bias = []
    for e in range(2):
        a = slopes[e] * key_off
        a_hi = a.astype(BF16).astype(F32)
        key_bias.append(jnp.where(aug_lane[e] == 0, a_hi, jnp.where(aug_lane[e] == 1, a - a_hi, 0.0)))
    for j in range(n_blocks):
        kj = k_ref[j * blk:(j + 1) * blk, :]
        for e in range(2):
            one_hot = (aug_lane[e] == MOBA_AUG_HI + j) | (aug_lane[e] == MOBA_AUG_HI + n_blocks + j)
            aug = jnp.where(one_hot, 1.0, key_bias[e])
            k_refs[e][j] = jnp.where(in_heads[e], kj, aug).astype(BF16)
        v_t = v_ref[j * blk:(j + 1) * blk, :].T
        for e in range(2):
            vt_ref[j, e, 0:HEAD_DIM, :] = v_t[e * HEAD_DIM:(e + 1) * HEAD_DIM, :].astype(BF16)
            vt_ref[j, e, HEAD_DIM:MOBA_VT_ROWS, :] = ones_row
        km_ref[j:j + 1, :] = jnp.mean(kj, axis=0, keepdims=True)
    d = _iota((blk, blk), 1) - _iota((blk, blk), 0)
    tri_ref[0] = jnp.zeros((blk, blk), F32)
    tri_ref[1] = jnp.where(d >= 0, 0.0, NEG_INF)

    seq = n_blocks * blk
    for ti in range(n_tiles):
        cols = slice(ti * tq, (ti + 1) * tq)
        qt_ref[:, cols] = (q_ref[cols, :] * (HEAD_DIM ** -0.5 * LOG2E)).T
    q_bf = qt_ref[...].astype(BF16)
    km = km_ref[...]
    lane_m = _iota(km.shape, 1)
    blk_id = _iota((n_blocks, seq), 0)
    q_pos = _iota((n_blocks, seq), 1)
    q_blk = q_pos // blk
    past = blk_id < q_blk
    distance = ((q_blk - blk_id) * blk + q_pos % blk).astype(F32)
    ones_rows = (_iota((MOBA_AUG_HI, tq), 0) < 2).astype(F32)
    pad_rows = jnp.zeros((HEAD_DIM - MOBA_AUG_HI - 2 * n_blocks, tq), F32)
    for e in range(2):
        in_head = (lane_m < HEAD_DIM) if e == 0 else (lane_m >= HEAD_DIM)
        km_e = jnp.where(in_head, km, 0.0).astype(BF16)
        gate = jnp.dot(km_e, q_bf, preferred_element_type=F32)
        gate = jnp.where(past, gate, NEG_INF)
        rank = jnp.zeros((n_blocks, seq), jnp.int32)
        for jp in range(n_blocks):
            gj = gate[jp:jp + 1, :]
            beats = (gj > gate) | ((gj == gate) & (jp < blk_id))
            rank = rank + beats.astype(jnp.int32)
        visible = (past & (rank < MOBA_TOPK)) | (blk_id == q_blk)
        bias = jnp.where(visible, -slopes[e] * distance, NEG_INF)
        bias_hi = bias.astype(BF16).astype(F32)
        bias_lo = bias - bias_hi
        for ti in range(n_tiles):
            cols = slice(ti * tq, (ti + 1) * tq)
            aug = jnp.concatenate([ones_rows, bias_hi[:, cols], bias_lo[:, cols], pad_rows], axis=0)
            q_head = qt_ref[e * HEAD_DIM:(e + 1) * HEAD_DIM, cols]
            parts = [q_head, aug] if e == 0 else [aug, q_head]
            qa_ref[ti, e] = jnp.concatenate(parts, axis=0).astype(BF16)

    def score(w, s_ref):
        ti = sched_ref[0, w]
        c = sched_ref[1, w]
        on_diag = sched_ref[2, w]
        maxes = []
        for e in range(2):
            q_aug = qa_ref[ti, e]
            m_blk = None
            for slab in range(nq):
                s_t = jnp.dot(k_refs[e][nq * c + slab], q_aug, preferred_element_type=F32)
                parts = [s_t[:, h * blk:(h + 1) * blk] + tri_ref[on_diag] if h == slab
                         else s_t[:, h * blk:(h + 1) * blk] for h in range(nq)]
                s_t = jnp.concatenate(parts, axis=1)
                s_ref[e, slab] = s_t
                mx = jnp.max(s_t, axis=0, keepdims=True)
                m_blk = mx if m_blk is None else jnp.maximum(m_blk, mx)
            maxes.append(m_blk)
        return tuple(maxes)

    def exponentiate(s_ref, p_ref, m):
        for e in range(2):
            for slab in range(nq):
                p_ref[e, slab] = jnp.exp2(s_ref[e, slab] - m[e]).astype(BF16)

    def accumulate(w, p_ref, alpha):
        c = sched_ref[1, w]
        ti = sched_ref[0, w]
        for e in range(2):
            pv = None
            for slab in range(nq):
                dd = jnp.dot(vt_ref[nq * c + slab, e], p_ref[e, slab], preferred_element_type=F32)
                pv = dd if pv is None else pv + dd
            acc_ref[ti, e] = alpha[e] * acc_ref[ti, e] + pv

    def retire(w, carry, s_next, p_next, s_free, p_cur):
        m_prev, m_cur, m_blk_next = carry
        starts_tile = sched_ref[2, w] == 1
        next_starts_tile = sched_ref[2, w + 1] == 1
        m_next = tuple(jnp.where(next_starts_tile, m_blk_next[e], jnp.maximum(m_cur[e], m_blk_next[e]))
                       for e in range(2))
        m_blk_new = score(w + 2, s_free)
        exponentiate(s_next, p_next, m_next)
        alpha = tuple(jnp.exp2(jnp.where(starts_tile, NEG_INF, m_prev[e]) - m_cur[e]) for e in range(2))
        accumulate(w, p_cur, alpha)
        return m_cur, m_next, m_blk_new

    acc_ref[...] = jnp.zeros_like(acc_ref)
    m_first = score(0, s0_ref)
    exponentiate(s0_ref, p0_ref, m_first)
    carry = (m_first, m_first, score(1, s1_ref))

    per_step = MOBA_ITEMS_PER_STEP if n_items % MOBA_ITEMS_PER_STEP == 0 else 2

    def body(t, carry):
        for i in range(per_step):
            bufs = (s1_ref, p1_ref, s0_ref, p0_ref) if i % 2 == 0 else (s0_ref, p0_ref, s1_ref, p1_ref)
            carry = retire(per_step * t + i, carry, *bufs)
        return carry

    lax.fori_loop(0, n_items // per_step, body, carry)

    for ti in range(n_tiles):
        out_t = jnp.concatenate([acc_ref[ti, e, 0:HEAD_DIM, :] / acc_ref[ti, e, HEAD_DIM:HEAD_DIM + 1, :]
                                 for e in range(2)], axis=0)
        o_ref[ti * tq:(ti + 1) * tq, :] = out_t.T


def _moba_pipe(qkv, slopes, bsz, s, n_heads):
    n_blocks = s // MOBA_BLOCK
    n_pairs = n_heads // 2
    blk = MOBA_BLOCK
    tq = MOBA_Q_BLOCKS * blk
    n_tiles = s // tq
    sched = _moba_schedule(n_tiles)
    n_items = sched.shape[1] - 2
    logits = pltpu.VMEM((2, MOBA_Q_BLOCKS, blk, tq), F32)
    probs = pltpu.VMEM((2, MOBA_Q_BLOCKS, blk, tq), BF16)
    return pl.pallas_call(
        functools.partial(_moba_pipe_kernel, n_blocks=n_blocks, n_items=n_items),
        grid=(bsz, n_pairs),
        in_specs=[pl.BlockSpec(memory_space=pltpu.SMEM),
                  pl.BlockSpec(memory_space=pltpu.SMEM),
                  pl.BlockSpec((s, LANES), lambda b, h: (b, h)),
                  pl.BlockSpec((s, LANES), lambda b, h: (b, n_pairs + h)),
                  pl.BlockSpec((s, LANES), lambda b, h: (b, 2 * n_pairs + h))],
        out_specs=pl.BlockSpec((s, LANES), lambda b, h: (b, h)),
        out_shape=jax.ShapeDtypeStruct((bsz * s, n_heads * HEAD_DIM), F32),
        scratch_shapes=[pltpu.VMEM((n_blocks, blk, LANES), BF16),
                        pltpu.VMEM((n_blocks, blk, LANES), BF16),
                        pltpu.VMEM((n_blocks, 2, MOBA_VT_ROWS, blk), BF16),
                        pltpu.VMEM((n_blocks, LANES), F32),
                        pltpu.VMEM((2, blk, blk), F32),
                        pltpu.VMEM((LANES, s), F32),
                        pltpu.VMEM((n_tiles, 2, LANES, tq), BF16),
                        pltpu.VMEM((n_tiles, 2, MOBA_VT_ROWS, tq), F32),
                        logits, logits, probs, probs],
        compiler_params=_params(2),
        name="moba",
    )(slopes, jnp.asarray(sched), qkv, qkv, qkv)


def _memory_attention(q, mem_k, mem_v):
    q = q * (HEAD_DIM ** -0.5)
    lane = lax.broadcasted_iota(jnp.int32, q.shape, 1)
    out = jnp.zeros(q.shape, F32)
    for h in range(MEM_HEADS):
        in_head = (lane >= h * HEAD_DIM) & (lane < (h + 1) * HEAD_DIM)
        qh = jnp.where(in_head, q, 0.0).astype(BF16)
        s = lax.dot_general(qh, mem_k, (((1,), (1,)), ((), ())), preferred_element_type=F32)
        m = jnp.max(s, axis=-1, keepdims=True)
        p = jnp.exp(s - m)
        l = jnp.sum(p, axis=-1, keepdims=True)
        pv = jnp.dot(p.astype(BF16), mem_v, preferred_element_type=F32)
        out = jnp.where(in_head, pv / l, out)
    return out


def _project_out(tok, mem_out, wo_ref, x):
    tw = tok.shape[1]
    return (x + jnp.dot(tok.astype(BF16), wo_ref[0:tw, :], preferred_element_type=F32)
            + jnp.dot(mem_out.astype(BF16), wo_ref[tw:, :], preferred_element_type=F32))


def _s5_mix_out_kernel(y_ref, u_ref, d_ref, wglu_ref, bglu_ref, mq_ref, mk_ref, mv_ref, wo_ref, x_ref, o_ref):
    y = y_ref[...] + d_ref[...] * u_ref[...]
    y = 0.5 * y * (1.0 + lax.erf(y * math.sqrt(0.5)))
    z = jnp.dot(y.astype(BF16), wglu_ref[...], preferred_element_type=F32) + bglu_ref[...]
    tok = y * jax.nn.sigmoid(z)
    mem_out = _memory_attention(mq_ref[...], mk_ref[...], mv_ref[...])
    o_ref[...] = _project_out(tok, mem_out, wo_ref, x_ref[...])


def _moba_mix_out_kernel(tok_ref, mq_ref, mk_ref, mv_ref, wo_ref, x_ref, o_ref):
    mem_out = _memory_attention(mq_ref[...], mk_ref[...], mv_ref[...])
    o_ref[...] = _project_out(tok_ref[...], mem_out, wo_ref, x_ref[...])


def _mix_out_specs(n, s, d, tok_w, mem_len, mq_block_col):
    tm = min(ROW_TILE, s)
    per_b = s // tm
    row = lambda i: (i, 0)
    const = lambda i: (0, 0)
    mem_specs = [pl.BlockSpec((tm, MEM_WIDTH), lambda i: (i, mq_block_col)),
                 pl.BlockSpec((mem_len, MEM_WIDTH), lambda i: (i // per_b, 0)),
                 pl.BlockSpec((mem_len, MEM_WIDTH), lambda i: (i // per_b, 1)),
                 pl.BlockSpec((d, d), const),
                 pl.BlockSpec((tm, d), row)]
    return tm, row, const, mem_specs


def _s5_mix_out(y_ssm, proj, d_skip, w_glu, b_glu, mem_kv, w_out, x2d, s, mem_len):
    n, d = x2d.shape
    tok_w = y_ssm.shape[1]
    tm, row, const, mem_specs = _mix_out_specs(n, s, d, tok_w, mem_len, tok_w // MEM_WIDTH)
    return pl.pallas_call(
        _s5_mix_out_kernel,
        grid=(n // tm,),
        in_specs=[pl.BlockSpec((tm, tok_w), row),
                  pl.BlockSpec((tm, tok_w), row),
                  pl.BlockSpec((1, tok_w), const),
                  pl.BlockSpec((tok_w, tok_w), const),
                  pl.BlockSpec((1, tok_w), const)] + mem_specs,
        out_specs=pl.BlockSpec((tm, d), row),
        out_shape=jax.ShapeDtypeStruct((n, d), F32),
        compiler_params=_params(1),
        name="s5_mix_out",
    )(y_ssm, proj, d_skip.reshape(1, tok_w), w_glu, b_glu.reshape(1, tok_w), proj, mem_kv, mem_kv, w_out, x2d)


def _moba_mix_out(tok, proj, mem_kv, w_out, x2d, s, mem_len):
    n, d = x2d.shape
    tok_w = tok.shape[1]
    tm, row, const, mem_specs = _mix_out_specs(n, s, d, tok_w, mem_len, (proj.shape[1] - MEM_WIDTH) // MEM_WIDTH)
    return pl.pallas_call(
        _moba_mix_out_kernel,
        grid=(n // tm,),
        in_specs=[pl.BlockSpec((tm, tok_w), row)] + mem_specs,
        out_specs=pl.BlockSpec((tm, d), row),
        out_shape=jax.ShapeDtypeStruct((n, d), F32),
        compiler_params=_params(1),
        name="moba_mix_out",
    )(tok, proj, mem_kv, mem_kv, w_out, x2d)


FFN_CHUNK = MXU_TILE


def _ffn_kernel(x_ref, g_ref, wg_ref, wu_ref, wd_ref, gf_ref, o_ref, act_ref, *, final_norm):
    x = x_ref[...]
    h = _rms_scale(x, g_ref[...]).astype(BF16)
    hidden = wg_ref.shape[1]
    for c in range(hidden // FFN_CHUNK):
        cols = slice(c * FFN_CHUNK, (c + 1) * FFN_CHUNK)
        gate = jnp.dot(h, wg_ref[:, cols], preferred_element_type=F32)
        up = jnp.dot(h, wu_ref[:, cols], preferred_element_type=F32)
        act_ref[:, cols] = (gate * jax.nn.sigmoid(gate) * up).astype(BF16)
    y = x + jnp.dot(act_ref[...], wd_ref[...], preferred_element_type=F32)
    if final_norm:
        y = _rms_scale(y, gf_ref[...])
    o_ref[...] = y


def _ffn(x2d, g, w_gate, w_up, w_down, g_final, final_norm):
    n, d = x2d.shape
    hidden = w_gate.shape[1]
    tm = min(ROW_TILE, n)
    row = lambda i: (i, 0)
    const = lambda i: (0, 0)
    return pl.pallas_call(
        functools.partial(_ffn_kernel, final_norm=final_norm),
        grid=(n // tm,),
        in_specs=[pl.BlockSpec((tm, d), row),
                  pl.BlockSpec((1, d), const),
                  pl.BlockSpec((d, hidden), const),
                  pl.BlockSpec((d, hidden), const),
                  pl.BlockSpec((hidden, d), const),
                  pl.BlockSpec((1, d), const)],
        out_specs=pl.BlockSpec((tm, d), row),
        out_shape=jax.ShapeDtypeStruct((n, d), F32),
        scratch_shapes=[pltpu.VMEM((tm, hidden), BF16)],
        compiler_params=_params(1),
        name="ffn",
    )(x2d, g.reshape(1, d), w_gate, w_up, w_down, g_final.reshape(1, d))


def kernel(x, mem, mem_norm_g, w_mem_kv, mix_norm_g, s5_w_in, s5_lambda_re, s5_lambda_im, s5_log_dt,
           s5_b_re, s5_b_im, s5_c_re, s5_c_im, s5_d, s5_w_glu, s5_b_glu, moba_w_in, w_out, ffn_norm_g,
           w_gate, w_up, w_down, final_norm_g):
    bsz, s, d = x.shape
    mem_len = mem.shape[1]
    depth = mix_norm_g.shape[0]
    tok_w = d - MEM_WIDTH
    n_heads = tok_w // HEAD_DIM
    slopes = jnp.asarray(2.0 ** (-8.0 * (np.arange(n_heads) + 1) / n_heads), dtype=F32)

    mem_kv = _norm_matmul(mem.reshape(bsz * mem_len, d), mem_norm_g, w_mem_kv.astype(BF16)).astype(BF16)
    x2d = x.reshape(bsz * s, d)
    for i in range(depth):
        j = i // 2
        if i % 2 == 0:
            proj = _norm_matmul(x2d, mix_norm_g[i], s5_w_in[j].astype(BF16))
            ops = _s5_operators(s5_lambda_re[j], s5_lambda_im[j], s5_log_dt[j],
                                s5_b_re[j], s5_b_im[j], s5_c_re[j], s5_c_im[j])
            y_ssm = _s5_ssm(proj, ops, bsz, s, tok_w)
            x2d = _s5_mix_out(y_ssm, proj, s5_d[j], s5_w_glu[j].astype(BF16), s5_b_glu[j],
                              mem_kv, w_out[i].astype(BF16), x2d, s, mem_len)
        else:
            proj = _norm_matmul(x2d, mix_norm_g[i], moba_w_in[j].astype(BF16))
            tok = _moba_pipe(proj, slopes, bsz, s, n_heads)
            x2d = _moba_mix_out(tok, proj, mem_kv, w_out[i].astype(BF16), x2d, s, mem_len)
        x2d = _ffn(x2d, ffn_norm_g[i], w_gate[i].astype(BF16), w_up[i].astype(BF16),
                   w_down[i].astype(BF16), final_norm_g, final_norm=(i == depth - 1))
    return x2d.reshape(bsz, s, d)
```

```python
import functools
import math

import numpy as np
import jax
import jax.numpy as jnp
from jax import lax
from jax.experimental import pallas as pl
from jax.experimental.pallas import tpu as pltpu

F32 = jnp.float32
BF16 = jnp.bfloat16

HEAD_DIM = 64
MEM_HEADS = 4
MEM_WIDTH = MEM_HEADS * HEAD_DIM
S5_GROUP_CH = 16
S5_STATE = 64
S5_CHUNK = 16
MOBA_BLOCK = 256
MOBA_TOPK = 3
RMS_EPS = 1e-6
NEG_INF = -1e30

LANES = 128
MXU_TILE = 256
VMEM_LIMIT_BYTES = 56 * 1024 * 1024
ROW_TILE = 512

S5_LANE_GROUPS = LANES // S5_GROUP_CH
S5_STEPS_PER_TILE = MXU_TILE // LANES
S5_LAG_TILES = S5_CHUNK // S5_STEPS_PER_TILE


def _params(n_axes):
    return pltpu.CompilerParams(dimension_semantics=("arbitrary",) * n_axes,
                                vmem_limit_bytes=VMEM_LIMIT_BYTES)


def _iota(shape, dim):
    return lax.broadcasted_iota(jnp.int32, shape, dim)


def _rms_scale(x, g):
    ms = jnp.mean(x * x, axis=-1, keepdims=True)
    return x * lax.rsqrt(ms + RMS_EPS) * g


def _norm_matmul_kernel(x_ref, g_ref, w_ref, o_ref):
    h = _rms_scale(x_ref[...], g_ref[...])
    o_ref[...] = jnp.dot(h.astype(BF16), w_ref[...], preferred_element_type=F32).astype(o_ref.dtype)


def _norm_matmul(x2d, g, w_bf16, out_dtype=F32, row_tile=ROW_TILE):
    n, d = x2d.shape
    f = w_bf16.shape[1]
    tm = min(row_tile, n)
    return pl.pallas_call(
        _norm_matmul_kernel,
        grid=(n // tm,),
        in_specs=[pl.BlockSpec((tm, d), lambda i: (i, 0)),
                  pl.BlockSpec((1, d), lambda i: (0, 0)),
                  pl.BlockSpec((d, f), lambda i: (0, 0))],
        out_specs=pl.BlockSpec((tm, f), lambda i: (i, 0)),
        out_shape=jax.ShapeDtypeStruct((n, f), out_dtype),
        compiler_params=_params(1),
        name="norm_matmul",
    )(x2d, g.reshape(1, d), w_bf16)


def _s5_operators(lam_re, lam_im, log_dt, b_re, b_im, c_re, c_im):
    hi = lax.Precision.HIGHEST
    t_len = S5_CHUNK
    g_cnt, p_cnt = lam_re.shape
    c_cnt = b_re.shape[-1]
    gpb = S5_LANE_GROUPS
    n_lb = g_cnt // gpb
    spt = S5_STEPS_PER_TILE
    lr = lam_re.astype(F32)
    li = lam_im.astype(F32)
    dt = jnp.exp(log_dt.astype(F32))[:, None]
    mag = jnp.exp(lr * dt)
    ar = mag * jnp.cos(li * dt)
    ai = mag * jnp.sin(li * dt)
    den = lr * lr + li * li
    nr = ar - 1.0
    fr = (nr * lr + ai * li) / den
    fi = (ai * lr - nr * li) / den
    br = jnp.swapaxes(b_re.astype(F32), 1, 2)
    bi = jnp.swapaxes(b_im.astype(F32), 1, 2)
    bbr = fr[:, None, :] * br - fi[:, None, :] * bi
    bbi = fr[:, None, :] * bi + fi[:, None, :] * br
    cr = c_re.astype(F32)
    ci = c_im.astype(F32)

    k = jnp.arange(t_len + 1, dtype=F32)[:, None, None]
    pm = jnp.exp(k * (lr * dt)[None])
    pr = pm * jnp.cos(k * (li * dt)[None])
    pi = pm * jnp.sin(k * (li * dt)[None])

    wr = pr[:, :, None, :] * bbr[None] - pi[:, :, None, :] * bbi[None]
    wi = pr[:, :, None, :] * bbi[None] + pi[:, :, None, :] * bbr[None]
    kern = (jnp.einsum('gop,kgcp->kgoc', cr, wr[:t_len], precision=hi)
            - jnp.einsum('gop,kgcp->kgoc', ci, wi[:t_len], precision=hi))
    lag = (jnp.arange(S5_LAG_TILES)[:, None, None] * spt
           + jnp.arange(spt)[None, None, :] - jnp.arange(spt)[None, :, None])
    kd = jnp.where((lag >= 0)[..., None, None, None], kern[jnp.clip(lag, 0, t_len - 1)], 0.0)
    kd = kd.reshape(S5_LAG_TILES, spt, spt, n_lb, gpb, c_cnt, c_cnt)
    lag_t = kd.transpose(3, 0, 2, 5, 1, 4, 6).reshape(n_lb, S5_LAG_TILES, spt * c_cnt, spt * gpb * c_cnt)

    def by_lane_block(x):
        return x.reshape(t_len, n_lb, gpb, c_cnt, 2 * p_cnt).transpose(1, 0, 2, 3, 4).reshape(
            n_lb, t_len * gpb * c_cnt, 2 * p_cnt)

    p_c = by_lane_block(jnp.concatenate([wr[:t_len][::-1], wi[:t_len][::-1]], axis=-1))
    qr = cr[None] * pr[1:, :, None, :] - ci[None] * pi[1:, :, None, :]
    qi = -(cr[None] * pi[1:, :, None, :] + ci[None] * pr[1:, :, None, :])
    q_c = by_lane_block(jnp.concatenate([qr, qi], axis=-1))

    coef = jnp.stack([pr[t_len].reshape(n_lb, gpb * p_cnt),
                      pi[t_len].reshape(n_lb, gpb * p_cnt)], axis=1)
    return lag_t.astype(BF16), p_c.astype(BF16), q_c.astype(BF16), coef


def _s5_ssm_kernel(u_ref, lag_ref, pc_ref, qc_ref, coef_ref, y_ref,
                   toep_scr, p_scr, q_scr, h_scr, s_scr, xp_scr, *, n_chunks):
    t_len = S5_CHUNK
    gpb = S5_LANE_GROUPS
    st = S5_STATE
    cc = S5_GROUP_CH
    half = gpb * st
    flat_w = t_len * LANES
    nt_dims = (((1,), (1,)), ((), ()))

    @pl.when(pl.program_id(1) == 0)
    def _expand_operators():
        def group_diag(x, row_unit, col_unit):
            same = ((_iota(x.shape, 0) // row_unit) % gpb) == ((_iota(x.shape, 1) // col_unit) % gpb)
            return jnp.where(same, x, 0.0).astype(BF16)

        shape = (2 * st, 2 * half)
        rep = ((_iota(shape, 0) // st == _iota(shape, 1) // half)
               & (_iota(shape, 0) % st == _iota(shape, 1) % st)).astype(BF16)
        for r in range(flat_w // MXU_TILE):
            rows = slice(r * MXU_TILE, (r + 1) * MXU_TILE)
            p_scr[rows, :] = group_diag(jnp.dot(pc_ref[0, rows, :], rep, preferred_element_type=F32), cc, st)
        shape = (2 * half, 2 * st)
        rep = ((_iota(shape, 0) // half == _iota(shape, 1) // st)
               & (_iota(shape, 0) % st == _iota(shape, 1) % st)).astype(BF16)
        for c in range(flat_w // MXU_TILE):
            cols = slice(c * MXU_TILE, (c + 1) * MXU_TILE)
            q_scr[:, cols] = group_diag(lax.dot_general(rep, qc_ref[0, cols, :], nt_dims,
                                                        preferred_element_type=F32), st, cc)
        shape = (MXU_TILE, S5_STEPS_PER_TILE * cc)
        rep = ((_iota(shape, 0) // LANES == _iota(shape, 1) // cc)
               & (_iota(shape, 0) % cc == _iota(shape, 1) % cc)).astype(BF16)
        for l in range(S5_LAG_TILES):
            toep_scr[l] = group_diag(jnp.dot(rep, lag_ref[0, l], preferred_element_type=F32), cc, cc)

    for j in range(t_len):
        h_scr[:, j * LANES:(j + 1) * LANES] = u_ref[pl.ds(j, n_chunks, stride=t_len), :].astype(BF16)
    s_scr[...] = jnp.dot(h_scr[...], p_scr[...], preferred_element_type=F32)
    cr = coef_ref[0, 0:1, :]
    ci = coef_ref[0, 1:2, :]

    xr = jnp.zeros((1, half), F32)
    xi = jnp.zeros((1, half), F32)
    for n in range(n_chunks):
        xp_scr[n:n + 1, 0:half] = xr
        xp_scr[n:n + 1, half:2 * half] = xi
        sr = s_scr[n:n + 1, 0:half]
        si = s_scr[n:n + 1, half:2 * half]
        xr, xi = cr * xr - ci * xi + sr, cr * xi + ci * xr + si
    xp = xp_scr[...].astype(BF16)
    for nt in range(S5_LAG_TILES):
        cols = slice(nt * MXU_TILE, (nt + 1) * MXU_TILE)
        acc = jnp.dot(xp, q_scr[:, cols], preferred_element_type=F32)
        for kt in range(nt + 1):
            acc = acc + lax.dot_general(h_scr[:, kt * MXU_TILE:(kt + 1) * MXU_TILE], toep_scr[nt - kt],
                                        nt_dims, preferred_element_type=F32)
        for tt in range(S5_STEPS_PER_TILE):
            t = nt * S5_STEPS_PER_TILE + tt
            y_ref[pl.ds(t, n_chunks, stride=t_len), :] = acc[:, tt * LANES:(tt + 1) * LANES]


def _s5_ssm(proj, ops, bsz, s, tok_w):
    lag_t, p_c, q_c, coef = ops
    n_lb = tok_w // LANES
    n_chunks = s // S5_CHUNK
    flat_w = p_c.shape[1]
    state_w = S5_LANE_GROUPS * p_c.shape[2]
    const3 = lambda l, b: (l, 0, 0)
    return pl.pallas_call(
        functools.partial(_s5_ssm_kernel, n_chunks=n_chunks),
        grid=(n_lb, bsz),
        in_specs=[pl.BlockSpec((s, LANES), lambda l, b: (b, l)),
                  pl.BlockSpec((1,) + lag_t.shape[1:], lambda l, b: (l, 0, 0, 0)),
                  pl.BlockSpec((1,) + p_c.shape[1:], const3),
                  pl.BlockSpec((1,) + q_c.shape[1:], const3),
                  pl.BlockSpec((1,) + coef.shape[1:], const3)],
        out_specs=pl.BlockSpec((s, LANES), lambda l, b: (b, l)),
        out_shape=jax.ShapeDtypeStruct((bsz * s, tok_w), F32),
        scratch_shapes=[pltpu.VMEM((S5_LAG_TILES, MXU_TILE, MXU_TILE), BF16),
                        pltpu.VMEM((flat_w, state_w), BF16),
                        pltpu.VMEM((state_w, flat_w), BF16),
                        pltpu.VMEM((n_chunks, flat_w), BF16),
                        pltpu.VMEM((n_chunks, state_w), F32),
                        pltpu.VMEM((n_chunks, state_w), F32)],
        compiler_params=_params(2),
        name="s5_ssm",
    )(proj, lag_t, p_c, q_c, coef)


MOBA_Q_BLOCKS = 2
MOBA_VT_ROWS = HEAD_DIM + 16
LOG2E = math.log2(math.e)
MOBA_AUG_HI = 8
MOBA_ITEMS_PER_STEP = 12


def _moba_schedule(n_tiles):
    rows = []
    for ti in range(n_tiles):
        for n, c in enumerate([ti] + list(range(ti))):
            rows.append((ti, c, int(n == 0)))
    assert len(rows) % 2 == 0, "the pipelined loop retires an even number of work items per step"
    rows += [(rows[-1][0], rows[-1][1], 0)] * 2
    return np.asarray(rows, np.int32).T.copy()


def _moba_kernel(slope_ref, sched_ref, q_ref, k_ref, v_ref, o_ref,
                 ka_ref, kb_ref, vt_ref, km_ref, tri_ref, qt_ref, qa_ref, acc_ref,
                 s0_ref, s1_ref, p0_ref, p1_ref, *, n_blocks, n_items):
    blk = MOBA_BLOCK
    nq = MOBA_Q_BLOCKS
    tq = nq * blk
    n_tiles = n_blocks // nq
    seq = n_blocks * blk
    hp = pl.program_id(1)
    slopes = (slope_ref[2 * hp] * LOG2E, slope_ref[2 * hp + 1] * LOG2E)
    k_refs = (ka_ref, kb_ref)

    lane = _iota((blk, LANES), 1)
    key_off = _iota((blk, LANES), 0).astype(F32)
    ones_row = (_iota((MOBA_VT_ROWS - HEAD_DIM, blk), 0) == 0).astype(BF16)
    in_heads = (lane < HEAD_DIM, lane >= HEAD_DIM)
    aug_lane = (lane - HEAD_DIM, lane)
    key_bias = []
    for e in range(2):
        a = slopes[e] * key_off
        a_hi = a.astype(BF16).astype(F32)
        key_bias.append(jnp.where(aug_lane[e] == 0, a_hi, jnp.where(aug_lane[e] == 1, a - a_hi, 0.0)))
    for j in range(n_blocks):
        kj = k_ref[j * blk:(j + 1) * blk, :].astype(F32)
        for e in range(2):
            one_hot = (aug_lane[e] == MOBA_AUG_HI + j) | (aug_lane[e] == MOBA_AUG_HI + n_blocks + j)
            aug = jnp.where(one_hot, 1.0, key_bias[e])
            k_refs[e][j] = jnp.where(in_heads[e], kj, aug).astype(BF16)
        v_t = v_ref[j * blk:(j + 1) * blk, :].astype(F32).T
        for e in range(2):
            vt_ref[j, e, 0:HEAD_DIM, :] = v_t[e * HEAD_DIM:(e + 1) * HEAD_DIM, :].astype(BF16)
            vt_ref[j, e, HEAD_DIM:MOBA_VT_ROWS, :] = ones_row
        km_ref[j:j + 1, :] = jnp.mean(kj, axis=0, keepdims=True)
    d = _iota((blk, blk), 1) - _iota((blk, blk), 0)
    tri_ref[0] = jnp.zeros((blk, blk), F32)
    tri_ref[1] = jnp.where(d >= 0, 0.0, NEG_INF)

    for ti in range(n_tiles):
        cols = slice(ti * tq, (ti + 1) * tq)
        q_tile = q_ref[cols, :].astype(F32) * (HEAD_DIM ** -0.5 * LOG2E)
        qt_ref[:, cols] = q_tile.T
    q_bf = qt_ref[...].astype(BF16)
    km = km_ref[...]
    lane_m = _iota(km.shape, 1)
    blk_id = _iota((n_blocks, seq), 0)
    q_pos = _iota((n_blocks, seq), 1)
    q_blk = q_pos // blk
    past = blk_id < q_blk
    distance = ((q_blk - blk_id) * blk + q_pos % blk).astype(F32)
    ones_rows = (_iota((MOBA_AUG_HI, tq), 0) < 2).astype(F32)
    pad_rows = jnp.zeros((HEAD_DIM - MOBA_AUG_HI - 2 * n_blocks, tq), F32)
    for e in range(2):
        in_head = (lane_m < HEAD_DIM) if e == 0 else (lane_m >= HEAD_DIM)
        km_e = jnp.where(in_head, km, 0.0).astype(BF16)
        gate = jnp.dot(km_e, q_bf, preferred_element_type=F32)
        gate = jnp.where(past, gate, NEG_INF)
        rank = jnp.zeros((n_blocks, seq), jnp.int32)
        for jp in range(n_blocks):
            gj = gate[jp:jp + 1, :]
            beats = (gj > gate) | ((gj == gate) & (jp < blk_id))
            rank = rank + beats.astype(jnp.int32)
        visible = (past & (rank < MOBA_TOPK)) | (blk_id == q_blk)
        bias = jnp.where(visible, -slopes[e] * distance, NEG_INF)
        bias_hi = bias.astype(BF16).astype(F32)
        bias_lo = bias - bias_hi
        for ti in range(n_tiles):
            cols = slice(ti * tq, (ti + 1) * tq)
            aug = jnp.concatenate([ones_rows, bias_hi[:, cols], bias_lo[:, cols], pad_rows], axis=0)
            q_head = qt_ref[e * HEAD_DIM:(e + 1) * HEAD_DIM, cols]
            parts = [q_head, aug] if e == 0 else [aug, q_head]
            qa_ref[ti, e] = jnp.concatenate(parts, axis=0).astype(BF16)

    def score(w, s_ref):
        ti = sched_ref[0, w]
        c = sched_ref[1, w]
        on_diag = sched_ref[2, w]
        maxes = []
        for e in range(2):
            q_aug = qa_ref[ti, e]
            m_blk = None
            for slab in range(nq):
                s_t = jnp.dot(k_refs[e][nq * c + slab], q_aug, preferred_element_type=F32)
                parts = [s_t[:, h * blk:(h + 1) * blk] + tri_ref[on_diag] if h == slab
                         else s_t[:, h * blk:(h + 1) * blk] for h in range(nq)]
                s_t = jnp.concatenate(parts, axis=1)
                s_ref[e, slab] = s_t
                mx = jnp.max(s_t, axis=0, keepdims=True)
                m_blk = mx if m_blk is None else jnp.maximum(m_blk, mx)
            maxes.append(m_blk)
        return tuple(maxes)

    def exponentiate(s_ref, p_ref, m):
        for e in range(2):
            for slab in range(nq):
                p_ref[e, slab] = jnp.exp2(s_ref[e, slab] - m[e]).astype(BF16)

    def accumulate(w, p_ref, alpha):
        ti = sched_ref[0, w]
        c = sched_ref[1, w]
        for e in range(2):
            pv = None
            for slab in range(nq):
                dd = jnp.dot(vt_ref[nq * c + slab, e], p_ref[e, slab], preferred_element_type=F32)
                pv = dd if pv is None else pv + dd
            acc_ref[ti, e] = alpha[e] * acc_ref[ti, e] + pv

    def retire(w, carry, s_next, p_next, s_free, p_cur):
        m_prev, m_cur, m_blk_next = carry
        starts_tile = sched_ref[2, w] == 1
        next_starts_tile = sched_ref[2, w + 1] == 1
        m_next = tuple(jnp.where(next_starts_tile, m_blk_next[e], jnp.maximum(m_cur[e], m_blk_next[e]))
                       for e in range(2))
        m_blk_new = score(w + 2, s_free)
        exponentiate(s_next, p_next, m_next)
        alpha = tuple(jnp.exp2(jnp.where(starts_tile, NEG_INF, m_prev[e]) - m_cur[e]) for e in range(2))
        accumulate(w, p_cur, alpha)
        return m_cur, m_next, m_blk_new

    acc_ref[...] = jnp.zeros_like(acc_ref)
    m_first = score(0, s0_ref)
    exponentiate(s0_ref, p0_ref, m_first)
    carry = (m_first, m_first, score(1, s1_ref))

    per_step = MOBA_ITEMS_PER_STEP if n_items % MOBA_ITEMS_PER_STEP == 0 else 2

    def body(t, carry):
        for i in range(per_step):
            bufs = (s1_ref, p1_ref, s0_ref, p0_ref) if i % 2 == 0 else (s0_ref, p0_ref, s1_ref, p1_ref)
            carry = retire(per_step * t + i, carry, *bufs)
        return carry

    lax.fori_loop(0, n_items // per_step, body, carry)

    for ti in range(n_tiles):
        out_t = jnp.concatenate([acc_ref[ti, e, 0:HEAD_DIM, :] / acc_ref[ti, e, HEAD_DIM:HEAD_DIM + 1, :]
                                 for e in range(2)], axis=0)
        o_ref[ti * tq:(ti + 1) * tq, :] = out_t.T.astype(o_ref.dtype)


def _moba(qkv, slopes, bsz, s, n_heads):
    n_blocks = s // MOBA_BLOCK
    n_pairs = n_heads // 2
    blk = MOBA_BLOCK
    tq = MOBA_Q_BLOCKS * blk
    n_tiles = s // tq
    sched = _moba_schedule(n_tiles)
    n_items = sched.shape[1] - 2
    logits = pltpu.VMEM((2, MOBA_Q_BLOCKS, blk, tq), F32)
    probs = pltpu.VMEM((2, MOBA_Q_BLOCKS, blk, tq), BF16)
    return pl.pallas_call(
        functools.partial(_moba_kernel, n_blocks=n_blocks, n_items=n_items),
        grid=(bsz, n_pairs),
        in_specs=[pl.BlockSpec(memory_space=pltpu.SMEM),
                  pl.BlockSpec(memory_space=pltpu.SMEM),
                  pl.BlockSpec((s, LANES), lambda b, h: (b, h)),
                  pl.BlockSpec((s, LANES), lambda b, h: (b, n_pairs + h)),
                  pl.BlockSpec((s, LANES), lambda b, h: (b, 2 * n_pairs + h))],
        out_specs=pl.BlockSpec((s, LANES), lambda b, h: (b, h)),
        out_shape=jax.ShapeDtypeStruct((bsz * s, n_heads * HEAD_DIM), BF16),
        scratch_shapes=[pltpu.VMEM((n_blocks, blk, LANES), BF16),
                        pltpu.VMEM((n_blocks, blk, LANES), BF16),
                        pltpu.VMEM((n_blocks, 2, MOBA_VT_ROWS, blk), BF16),
                        pltpu.VMEM((n_blocks, LANES), F32),
                        pltpu.VMEM((2, blk, blk), F32),
                        pltpu.VMEM((LANES, s), F32),
                        pltpu.VMEM((n_tiles, 2, LANES, tq), BF16),
                        pltpu.VMEM((n_tiles, 2, MOBA_VT_ROWS, tq), F32),
                        logits, logits, probs, probs],
        compiler_params=_params(2),
        name="moba",
    )(slopes, jnp.asarray(sched), qkv, qkv, qkv)


def _memory_attention(q, mem_k, mem_v):
    q = q * (HEAD_DIM ** -0.5)
    lane = _iota(q.shape, 1)
    out = jnp.zeros(q.shape, F32)
    for h in range(MEM_HEADS):
        in_head = (lane >= h * HEAD_DIM) & (lane < (h + 1) * HEAD_DIM)
        qh = jnp.where(in_head, q, 0.0).astype(BF16)
        s = lax.dot_general(qh, mem_k, (((1,), (1,)), ((), ())), preferred_element_type=F32)
        m = jnp.max(s, axis=-1, keepdims=True)
        p = jnp.exp(s - m)
        l = jnp.sum(p, axis=-1, keepdims=True)
        pv = jnp.dot(p.astype(BF16), mem_v, preferred_element_type=F32)
        out = jnp.where(in_head, pv / l, out)
    return out


def _project_out(tok, mem_out, wo_ref, x):
    tw = tok.shape[1]
    return (x + jnp.dot(tok.astype(BF16), wo_ref[0:tw, :], preferred_element_type=F32)
            + jnp.dot(mem_out.astype(BF16), wo_ref[tw:, :], preferred_element_type=F32))


def _s5_mix_out_kernel(y_ref, u_ref, d_ref, wglu_ref, bglu_ref, mq_ref, mk_ref, mv_ref, wo_ref, x_ref, o_ref):
    y = y_ref[...] + d_ref[...] * u_ref[...]
    y = 0.5 * y * (1.0 + lax.erf(y * math.sqrt(0.5)))
    z = jnp.dot(y.astype(BF16), wglu_ref[...], preferred_element_type=F32) + bglu_ref[...]
    tok = y * jax.nn.sigmoid(z)
    mem_out = _memory_attention(mq_ref[...], mk_ref[...], mv_ref[...])
    o_ref[...] = _project_out(tok, mem_out, wo_ref, x_ref[...])


def _moba_mix_out_kernel(tok_ref, mq_ref, mk_ref, mv_ref, wo_ref, x_ref, o_ref):
    mem_out = _memory_attention(mq_ref[...].astype(F32), mk_ref[...], mv_ref[...])
    o_ref[...] = _project_out(tok_ref[...], mem_out, wo_ref, x_ref[...])


def _mix_out_specs(n, s, d, tok_w, mem_len, mq_block_col):
    tm = min(ROW_TILE, s)
    per_b = s // tm
    row = lambda i: (i, 0)
    const = lambda i: (0, 0)
    mem_specs = [pl.BlockSpec((tm, MEM_WIDTH), lambda i: (i, mq_block_col)),
                 pl.BlockSpec((mem_len, MEM_WIDTH), lambda i: (i // per_b, 0)),
                 pl.BlockSpec((mem_len, MEM_WIDTH), lambda i: (i // per_b, 1)),
                 pl.BlockSpec((d, d), const),
                 pl.BlockSpec((tm, d), row)]
    return tm, row, const, mem_specs


def _s5_mix_out(y_ssm, proj, d_skip, w_glu, b_glu, mem_kv, w_out, x2d, s, mem_len):
    n, d = x2d.shape
    tok_w = y_ssm.shape[1]
    tm, row, const, mem_specs = _mix_out_specs(n, s, d, tok_w, mem_len, tok_w // MEM_WIDTH)
    return pl.pallas_call(
        _s5_mix_out_kernel,
        grid=(n // tm,),
        in_specs=[pl.BlockSpec((tm, tok_w), row),
                  pl.BlockSpec((tm, tok_w), row),
                  pl.BlockSpec((1, tok_w), const),
                  pl.BlockSpec((tok_w, tok_w), const),
                  pl.BlockSpec((1, tok_w), const)] + mem_specs,
        out_specs=pl.BlockSpec((tm, d), row),
        out_shape=jax.ShapeDtypeStruct((n, d), F32),
        compiler_params=_params(1),
        name="s5_mix_out",
    )(y_ssm, proj, d_skip.reshape(1, tok_w), w_glu, b_glu.reshape(1, tok_w), proj, mem_kv, mem_kv, w_out, x2d)


def _moba_mix_out(tok, proj, mem_kv, w_out, x2d, s, mem_len):
    n, d = x2d.shape
    tok_w = tok.shape[1]
    tm, row, const, mem_specs = _mix_out_specs(n, s, d, tok_w, mem_len, (proj.shape[1] - MEM_WIDTH) // MEM_WIDTH)
    return pl.pallas_call(
        _moba_mix_out_kernel,
        grid=(n // tm,),
        in_specs=[pl.BlockSpec((tm, tok_w), row)] + mem_specs,
        out_specs=pl.BlockSpec((tm, d), row),
        out_shape=jax.ShapeDtypeStruct((n, d), F32),
        compiler_params=_params(1),
        name="moba_mix_out",
    )(tok, proj, mem_kv, mem_kv, w_out, x2d)


FFN_CHUNK = MXU_TILE


def _ffn_kernel(x_ref, g_ref, wg_ref, wu_ref, wd_ref, gf_ref, o_ref, act_ref, *, final_norm):
    x = x_ref[...]
    h = _rms_scale(x, g_ref[...]).astype(BF16)
    hidden = wg_ref.shape[1]
    for c in range(hidden // FFN_CHUNK):
        cols = slice(c * FFN_CHUNK, (c + 1) * FFN_CHUNK)
        gate = jnp.dot(h, wg_ref[:, cols], preferred_element_type=F32)
        up = jnp.dot(h, wu_ref[:, cols], preferred_element_type=F32)
        act_ref[:, cols] = (gate * jax.nn.sigmoid(gate) * up).astype(BF16)
    y = x + jnp.dot(act_ref[...], wd_ref[...], preferred_element_type=F32)
    if final_norm:
        y = _rms_scale(y, gf_ref[...])
    o_ref[...] = y


def _ffn(x2d, g, w_gate, w_up, w_down, g_final, final_norm):
    n, d = x2d.shape
    hidden = w_gate.shape[1]
    tm = min(ROW_TILE, n)
    row = lambda i: (i, 0)
    const = lambda i: (0, 0)
    return pl.pallas_call(
        functools.partial(_ffn_kernel, final_norm=final_norm),
        grid=(n // tm,),
        in_specs=[pl.BlockSpec((tm, d), row),
                  pl.BlockSpec((1, d), const),
                  pl.BlockSpec((d, hidden), const),
                  pl.BlockSpec((d, hidden), const),
                  pl.BlockSpec((hidden, d), const),
                  pl.BlockSpec((1, d), const)],
        out_specs=pl.BlockSpec((tm, d), row),
        out_shape=jax.ShapeDtypeStruct((n, d), F32),
        scratch_shapes=[pltpu.VMEM((tm, hidden), BF16)],
        compiler_params=_params(1),
        name="ffn",
    )(x2d, g.reshape(1, d), w_gate, w_up, w_down, g_final.reshape(1, d))


def kernel(x, mem, mem_norm_g, w_mem_kv, mix_norm_g, s5_w_in, s5_lambda_re, s5_lambda_im, s5_log_dt,
           s5_b_re, s5_b_im, s5_c_re, s5_c_im, s5_d, s5_w_glu, s5_b_glu, moba_w_in, w_out, ffn_norm_g,
           w_gate, w_up, w_down, final_norm_g):
    bsz, s, d = x.shape
    mem_len = mem.shape[1]
    depth = mix_norm_g.shape[0]
    tok_w = d - MEM_WIDTH
    n_heads = tok_w // HEAD_DIM
    slopes = jnp.asarray(2.0 ** (-8.0 * (np.arange(n_heads) + 1) / n_heads), dtype=F32)

    mem_kv = _norm_matmul(mem.reshape(bsz * mem_len, d), mem_norm_g, w_mem_kv.astype(BF16), out_dtype=BF16)
    x2d = x.reshape(bsz * s, d)
    for i in range(depth):
        j = i // 2
        if i % 2 == 0:
            proj = _norm_matmul(x2d, mix_norm_g[i], s5_w_in[j].astype(BF16), row_tile=2 * ROW_TILE)
            ops = _s5_operators(s5_lambda_re[j], s5_lambda_im[j], s5_log_dt[j],
                                s5_b_re[j], s5_b_im[j], s5_c_re[j], s5_c_im[j])
            y_ssm = _s5_ssm(proj, ops, bsz, s, tok_w)
            x2d = _s5_mix_out(y_ssm, proj, s5_d[j], s5_w_glu[j].astype(BF16), s5_b_glu[j],
                              mem_kv, w_out[i].astype(BF16), x2d, s, mem_len)
        else:
            proj = _norm_matmul(x2d, mix_norm_g[i], moba_w_in[j].astype(BF16), out_dtype=BF16)
            tok = _moba(proj, slopes, bsz, s, n_heads)
            x2d = _moba_mix_out(tok, proj, mem_kv, w_out[i].astype(BF16), x2d, s, mem_len)
        x2d = _ffn(x2d, ffn_norm_g[i], w_gate[i].astype(BF16), w_up[i].astype(BF16),
                   w_down[i].astype(BF16), final_norm_g, final_norm=(i == depth - 1))
    return x2d.reshape(bsz, s, d)
```

```python
import functools
import math

import numpy as np
import jax
import jax.numpy as jnp
from jax import lax
from jax.experimental import pallas as pl
from jax.experimental.pallas import tpu as pltpu

F32 = jnp.float32
BF16 = jnp.bfloat16

HEAD_DIM = 64
MEM_HEADS = 4
MEM_WIDTH = MEM_HEADS * HEAD_DIM
S5_GROUP_CH = 16
S5_STATE = 64
S5_CHUNK = 16
MOBA_BLOCK = 256
MOBA_TOPK = 3
RMS_EPS = 1e-6
NEG_INF = -1e30

LANES = 128
MXU_TILE = 256
VMEM_LIMIT_BYTES = 56 * 1024 * 1024
ROW_TILE = 512

S5_LANE_GROUPS = LANES // S5_GROUP_CH
S5_STEPS_PER_TILE = MXU_TILE // LANES
S5_LAG_TILES = S5_CHUNK // S5_STEPS_PER_TILE


def _params(n_axes):
    return pltpu.CompilerParams(dimension_semantics=("arbitrary",) * n_axes,
                                vmem_limit_bytes=VMEM_LIMIT_BYTES)


def _iota(shape, dim):
    return lax.broadcasted_iota(jnp.int32, shape, dim)


def _rms_scale(x, g):
    ms = jnp.mean(x * x, axis=-1, keepdims=True)
    return x * lax.rsqrt(ms + RMS_EPS) * g


def _norm_matmul_kernel(x_ref, g_ref, w_ref, o_ref):
    h = _rms_scale(x_ref[...], g_ref[...])
    o_ref[...] = jnp.dot(h.astype(BF16), w_ref[...], preferred_element_type=F32).astype(o_ref.dtype)


def _norm_matmul(x2d, g, w_bf16, out_dtype=F32, row_tile=ROW_TILE):
    n, d = x2d.shape
    f = w_bf16.shape[1]
    tm = min(row_tile, n)
    return pl.pallas_call(
        _norm_matmul_kernel,
        grid=(n // tm,),
        in_specs=[pl.BlockSpec((tm, d), lambda i: (i, 0)),
                  pl.BlockSpec((1, d), lambda i: (0, 0)),
                  pl.BlockSpec((d, f), lambda i: (0, 0))],
        out_specs=pl.BlockSpec((tm, f), lambda i: (i, 0)),
        out_shape=jax.ShapeDtypeStruct((n, f), out_dtype),
        compiler_params=_params(1),
        name="norm_matmul",
    )(x2d, g.reshape(1, d), w_bf16)


def _s5_operators(lam_re, lam_im, log_dt, b_re, b_im, c_re, c_im):
    hi = lax.Precision.HIGHEST
    t_len = S5_CHUNK
    g_cnt, p_cnt = lam_re.shape
    c_cnt = b_re.shape[-1]
    gpb = S5_LANE_GROUPS
    n_lb = g_cnt // gpb
    spt = S5_STEPS_PER_TILE
    lr = lam_re.astype(F32)
    li = lam_im.astype(F32)
    dt = jnp.exp(log_dt.astype(F32))[:, None]
    mag = jnp.exp(lr * dt)
    ar = mag * jnp.cos(li * dt)
    ai = mag * jnp.sin(li * dt)
    den = lr * lr + li * li
    nr = ar - 1.0
    fr = (nr * lr + ai * li) / den
    fi = (ai * lr - nr * li) / den
    br = jnp.swapaxes(b_re.astype(F32), 1, 2)
    bi = jnp.swapaxes(b_im.astype(F32), 1, 2)
    bbr = fr[:, None, :] * br - fi[:, None, :] * bi
    bbi = fr[:, None, :] * bi + fi[:, None, :] * br
    cr = c_re.astype(F32)
    ci = c_im.astype(F32)

    k = jnp.arange(t_len + 1, dtype=F32)[:, None, None]
    pm = jnp.exp(k * (lr * dt)[None])
    pr = pm * jnp.cos(k * (li * dt)[None])
    pi = pm * jnp.sin(k * (li * dt)[None])

    wr = pr[:, :, None, :] * bbr[None] - pi[:, :, None, :] * bbi[None]
    wi = pr[:, :, None, :] * bbi[None] + pi[:, :, None, :] * bbr[None]
    kern = (jnp.einsum('gop,kgcp->kgoc', cr, wr[:t_len], precision=hi)
            - jnp.einsum('gop,kgcp->kgoc', ci, wi[:t_len], precision=hi))
    lag = (jnp.arange(S5_LAG_TILES)[:, None, None] * spt
           + jnp.arange(spt)[None, None, :] - jnp.arange(spt)[None, :, None])
    kd = jnp.where((lag >= 0)[..., None, None, None], kern[jnp.clip(lag, 0, t_len - 1)], 0.0)
    kd = kd.reshape(S5_LAG_TILES, spt, spt, n_lb, gpb, c_cnt, c_cnt)
    lag_t = kd.transpose(3, 0, 2, 5, 1, 4, 6).reshape(n_lb, S5_LAG_TILES, spt * c_cnt, spt * gpb * c_cnt)

    def by_lane_block(x):
        return x.reshape(t_len, n_lb, gpb, c_cnt, 2 * p_cnt).transpose(1, 0, 2, 3, 4).reshape(
            n_lb, t_len * gpb * c_cnt, 2 * p_cnt)

    p_c = by_lane_block(jnp.concatenate([wr[:t_len][::-1], wi[:t_len][::-1]], axis=-1))
    qr = cr[None] * pr[1:, :, None, :] - ci[None] * pi[1:, :, None, :]
    qi = -(cr[None] * pi[1:, :, None, :] + ci[None] * pr[1:, :, None, :])
    q_c = by_lane_block(jnp.concatenate([qr, qi], axis=-1))

    coef = jnp.stack([pr[t_len].reshape(n_lb, gpb * p_cnt),
                      pi[t_len].reshape(n_lb, gpb * p_cnt)], axis=1)
    return lag_t.astype(BF16), p_c.astype(BF16), q_c.astype(BF16), coef


def _s5_ssm_kernel(u_ref, lag_ref, pc_ref, qc_ref, coef_ref, y_ref,
                   toep_scr, p_scr, q_scr, h_scr, s_scr, xp_scr, *, n_chunks):
    t_len = S5_CHUNK
    gpb = S5_LANE_GROUPS
    st = S5_STATE
    cc = S5_GROUP_CH
    half = gpb * st
    flat_w = t_len * LANES
    nt_dims = (((1,), (1,)), ((), ()))

    @pl.when(pl.program_id(1) == 0)
    def _expand_operators():
        def group_diag(x, row_unit, col_unit):
            same = ((_iota(x.shape, 0) // row_unit) % gpb) == ((_iota(x.shape, 1) // col_unit) % gpb)
            return jnp.where(same, x, 0.0).astype(BF16)

        shape = (2 * st, 2 * half)
        rep = ((_iota(shape, 0) // st == _iota(shape, 1) // half)
               & (_iota(shape, 0) % st == _iota(shape, 1) % st)).astype(BF16)
        for r in range(flat_w // MXU_TILE):
            rows = slice(r * MXU_TILE, (r + 1) * MXU_TILE)
            p_scr[rows, :] = group_diag(jnp.dot(pc_ref[0, rows, :], rep, preferred_element_type=F32), cc, st)
        shape = (2 * half, 2 * st)
        rep = ((_iota(shape, 0) // half == _iota(shape, 1) // st)
               & (_iota(shape, 0) % st == _iota(shape, 1) % st)).astype(BF16)
        for c in range(flat_w // MXU_TILE):
            cols = slice(c * MXU_TILE, (c + 1) * MXU_TILE)
            q_scr[:, cols] = group_diag(lax.dot_general(rep, qc_ref[0, cols, :], nt_dims,
                                                        preferred_element_type=F32), st, cc)
        shape = (MXU_TILE, S5_STEPS_PER_TILE * cc)
        rep = ((_iota(shape, 0) // LANES == _iota(shape, 1) // cc)
               & (_iota(shape, 0) % cc == _iota(shape, 1) % cc)).astype(BF16)
        for l in range(S5_LAG_TILES):
            toep_scr[l] = group_diag(jnp.dot(rep, lag_ref[0, l], preferred_element_type=F32), cc, cc)

    for j in range(t_len):
        h_scr[:, j * LANES:(j + 1) * LANES] = u_ref[pl.ds(j, n_chunks, stride=t_len), :].astype(BF16)
    s_scr[...] = jnp.dot(h_scr[...], p_scr[...], preferred_element_type=F32)
    cr = coef_ref[0, 0:1, :]
    ci = coef_ref[0, 1:2, :]

    xr = jnp.zeros((1, half), F32)
    xi = jnp.zeros((1, half), F32)
    for n in range(n_chunks):
        xp_scr[n:n + 1, 0:half] = xr
        xp_scr[n:n + 1, half:2 * half] = xi
        sr = s_scr[n:n + 1, 0:half]
        si = s_scr[n:n + 1, half:2 * half]
        xr, xi = cr * xr - ci * xi + sr, cr * xi + ci * xr + si
    xp = xp_scr[...].astype(BF16)
    for nt in range(S5_LAG_TILES):
        cols = slice(nt * MXU_TILE, (nt + 1) * MXU_TILE)
        acc = jnp.dot(xp, q_scr[:, cols], preferred_element_type=F32)
        for kt in range(nt + 1):
            acc = acc + lax.dot_general(h_scr[:, kt * MXU_TILE:(kt + 1) * MXU_TILE], toep_scr[nt - kt],
                                        nt_dims, preferred_element_type=F32)
        for tt in range(S5_STEPS_PER_TILE):
            t = nt * S5_STEPS_PER_TILE + tt
            y_ref[pl.ds(t, n_chunks, stride=t_len), :] = acc[:, tt * LANES:(tt + 1) * LANES]


def _s5_ssm(proj, ops, bsz, s, tok_w):
    lag_t, p_c, q_c, coef = ops
    n_lb = tok_w // LANES
    n_chunks = s // S5_CHUNK
    flat_w = p_c.shape[1]
    state_w = S5_LANE_GROUPS * p_c.shape[2]
    const3 = lambda l, b: (l, 0, 0)
    return pl.pallas_call(
        functools.partial(_s5_ssm_kernel, n_chunks=n_chunks),
        grid=(n_lb, bsz),
        in_specs=[pl.BlockSpec((s, LANES), lambda l, b: (b, l)),
                  pl.BlockSpec((1,) + lag_t.shape[1:], lambda l, b: (l, 0, 0, 0)),
                  pl.BlockSpec((1,) + p_c.shape[1:], const3),
                  pl.BlockSpec((1,) + q_c.shape[1:], const3),
                  pl.BlockSpec((1,) + coef.shape[1:], const3)],
        out_specs=pl.BlockSpec((s, LANES), lambda l, b: (b, l)),
        out_shape=jax.ShapeDtypeStruct((bsz * s, tok_w), F32),
        scratch_shapes=[pltpu.VMEM((S5_LAG_TILES, MXU_TILE, MXU_TILE), BF16),
                        pltpu.VMEM((flat_w, state_w), BF16),
                        pltpu.VMEM((state_w, flat_w), BF16),
                        pltpu.VMEM((n_chunks, flat_w), BF16),
                        pltpu.VMEM((n_chunks, state_w), F32),
                        pltpu.VMEM((n_chunks, state_w), F32)],
        compiler_params=_params(2),
        name="s5_ssm",
    )(proj, lag_t, p_c, q_c, coef)


MOBA_Q_BLOCKS = 2
MOBA_VT_ROWS = HEAD_DIM + 16
LOG2E = math.log2(math.e)
MOBA_AUG_HI = 8
MOBA_ITEMS_PER_STEP = 12


def _moba_schedule(n_tiles):
    rows = []
    for ti in range(n_tiles):
        for n, c in enumerate([ti] + list(range(ti))):
            rows.append((ti, c, int(n == 0)))
    assert len(rows) % 2 == 0, "the pipelined loop retires an even number of work items per step"
    rows += [(rows[-1][0], rows[-1][1], 0)] * 2
    return np.asarray(rows, np.int32).T.copy()


def _moba_kernel(slope_ref, sched_ref, q_ref, k_ref, v_ref, o_ref,
                 ka_ref, kb_ref, vt_ref, km_ref, tri_ref, qt_ref, qa_ref, acc_ref,
                 s0_ref, s1_ref, p0_ref, p1_ref, *, n_blocks, n_items):
    blk = MOBA_BLOCK
    nq = MOBA_Q_BLOCKS
    tq = nq * blk
    n_tiles = n_blocks // nq
    seq = n_blocks * blk
    hp = pl.program_id(1)
    slopes = (slope_ref[2 * hp] * LOG2E, slope_ref[2 * hp + 1] * LOG2E)
    k_refs = (ka_ref, kb_ref)

    lane = _iota((blk, LANES), 1)
    key_off = _iota((blk, LANES), 0).astype(F32)
    ones_row = (_iota((MOBA_VT_ROWS - HEAD_DIM, blk), 0) == 0).astype(BF16)
    in_heads = (lane < HEAD_DIM, lane >= HEAD_DIM)
    aug_lane = (lane - HEAD_DIM, lane)
    key_bias = []
    for e in range(2):
        a = slopes[e] * key_off
        a_hi = a.astype(BF16).astype(F32)
        key_bias.append(jnp.where(aug_lane[e] == 0, a_hi, jnp.where(aug_lane[e] == 1, a - a_hi, 0.0)))
    for j in range(n_blocks):
        kj = k_ref[j * blk:(j + 1) * blk, :].astype(F32)
        for e in range(2):
            one_hot = (aug_lane[e] == MOBA_AUG_HI + j) | (aug_lane[e] == MOBA_AUG_HI + n_blocks + j)
            aug = jnp.where(one_hot, 1.0, key_bias[e])
            k_refs[e][j] = jnp.where(in_heads[e], kj, aug).astype(BF16)
        v_t = v_ref[j * blk:(j + 1) * blk, :].astype(F32).T
        for e in range(2):
            vt_ref[j, e, 0:HEAD_DIM, :] = v_t[e * HEAD_DIM:(e + 1) * HEAD_DIM, :].astype(BF16)
            vt_ref[j, e, HEAD_DIM:MOBA_VT_ROWS, :] = ones_row
        km_ref[j:j + 1, :] = jnp.mean(kj, axis=0, keepdims=True)
    d = _iota((blk, blk), 1) - _iota((blk, blk), 0)
    tri_ref[0] = jnp.zeros((blk, blk), F32)
    tri_ref[1] = jnp.where(d >= 0, 0.0, NEG_INF)

    for ti in range(n_tiles):
        cols = slice(ti * tq, (ti + 1) * tq)
        q_tile = q_ref[cols, :].astype(F32) * (HEAD_DIM ** -0.5 * LOG2E)
        qt_ref[:, cols] = q_tile.T
    q_bf = qt_ref[...].astype(BF16)
    km = km_ref[...]
    lane_m = _iota(km.shape, 1)
    sub = 8
    groups = range(0, n_blocks, sub)
    q_pos = _iota((sub, seq), 1)
    q_blk = q_pos // blk
    blk_ids = [_iota((sub, seq), 0) + g0 for g0 in groups]
    ones_rows = (_iota((MOBA_AUG_HI, tq), 0) < 2).astype(F32)
    pad_rows = jnp.zeros((HEAD_DIM - MOBA_AUG_HI - 2 * n_blocks, tq), F32)
    for e in range(2):
        in_head = (lane_m < HEAD_DIM) if e == 0 else (lane_m >= HEAD_DIM)
        km_e = jnp.where(in_head, km, 0.0).astype(BF16)
        gate = jnp.dot(km_e, q_bf, preferred_element_type=F32)
        gates = [jnp.where(blk_id < q_blk, gate[g0:g0 + sub, :], NEG_INF)
                 for g0, blk_id in zip(groups, blk_ids)]
        bias_hi, bias_lo = [], []
        for g0, blk_id, rows in zip(groups, blk_ids, gates):
            rank = jnp.zeros(rows.shape, jnp.int32)
            for jp in range(n_blocks - 1):
                gj = gates[jp // sub][jp % sub:jp % sub + 1, :]
                if jp < g0:
                    beats = gj >= rows
                elif jp >= g0 + sub:
                    beats = gj > rows
                else:
                    beats = (gj > rows) | ((gj == rows) & (blk_id > jp))
                rank = rank + beats.astype(jnp.int32)
            visible = ((blk_id < q_blk) & (rank < MOBA_TOPK)) | (blk_id == q_blk)
            distance = ((q_blk - blk_id) * blk + q_pos % blk).astype(F32)
            bias = jnp.where(visible, -slopes[e] * distance, NEG_INF)
            hi = bias.astype(BF16).astype(F32)
            bias_hi.append(hi)
            bias_lo.append(bias - hi)
        for ti in range(n_tiles):
            cols = slice(ti * tq, (ti + 1) * tq)
            aug = jnp.concatenate([ones_rows] + [b[:, cols] for b in bias_hi + bias_lo] + [pad_rows],
                                  axis=0)
            q_head = qt_ref[e * HEAD_DIM:(e + 1) * HEAD_DIM, cols]
            parts = [q_head, aug] if e == 0 else [aug, q_head]
            qa_ref[ti, e] = jnp.concatenate(parts, axis=0).astype(BF16)

    def score(w, s_ref):
        ti = sched_ref[0, w]
        c = sched_ref[1, w]
        on_diag = sched_ref[2, w]
        maxes = []
        for e in range(2):
            q_aug = qa_ref[ti, e]
            m_blk = None
            for slab in range(nq):
                s_t = jnp.dot(k_refs[e][nq * c + slab], q_aug, preferred_element_type=F32)
                parts = [s_t[:, h * blk:(h + 1) * blk] + tri_ref[on_diag] if h == slab
                         else s_t[:, h * blk:(h + 1) * blk] for h in range(nq)]
                s_t = jnp.concatenate(parts, axis=1)
                s_ref[e, slab] = s_t
                mx = jnp.max(s_t, axis=0, keepdims=True)
                m_blk = mx if m_blk is None else jnp.maximum(m_blk, mx)
            maxes.append(m_blk)
        return tuple(maxes)

    def exponentiate(s_ref, p_ref, m):
        for e in range(2):
            for slab in range(nq):
                p_ref[e, slab] = jnp.exp2(s_ref[e, slab] - m[e]).astype(BF16)

    def accumulate(w, p_ref, alpha):
        ti = sched_ref[0, w]
        c = sched_ref[1, w]
        for e in range(2):
            pv = None
            for slab in range(nq):
                dd = jnp.dot(vt_ref[nq * c + slab, e], p_ref[e, slab], preferred_element_type=F32)
                pv = dd if pv is None else pv + dd
            acc_ref[ti, e] = alpha[e] * acc_ref[ti, e] + pv

    def retire(w, carry, s_next, p_next, s_free, p_cur):
        m_prev, m_cur, m_blk_next = carry
        starts_tile = sched_ref[2, w] == 1
        next_starts_tile = sched_ref[2, w + 1] == 1
        m_next = tuple(jnp.where(next_starts_tile, m_blk_next[e], jnp.maximum(m_cur[e], m_blk_next[e]))
                       for e in range(2))
        m_blk_new = score(w + 2, s_free)
        exponentiate(s_next, p_next, m_next)
        alpha = tuple(jnp.exp2(jnp.where(starts_tile, NEG_INF, m_prev[e]) - m_cur[e]) for e in range(2))
        accumulate(w, p_cur, alpha)
        return m_cur, m_next, m_blk_new

    acc_ref[...] = jnp.zeros_like(acc_ref)
    m_first = score(0, s0_ref)
    exponentiate(s0_ref, p0_ref, m_first)
    carry = (m_first, m_first, score(1, s1_ref))

    per_step = MOBA_ITEMS_PER_STEP if n_items % MOBA_ITEMS_PER_STEP == 0 else 2

    def body(t, carry):
        for i in range(per_step):
            bufs = (s1_ref, p1_ref, s0_ref, p0_ref) if i % 2 == 0 else (s0_ref, p0_ref, s1_ref, p1_ref)
            carry = retire(per_step * t + i, carry, *bufs)
        return carry

    lax.fori_loop(0, n_items // per_step, body, carry)

    for ti in range(n_tiles):
        out_t = jnp.concatenate([acc_ref[ti, e, 0:HEAD_DIM, :] / acc_ref[ti, e, HEAD_DIM:HEAD_DIM + 1, :]
                                 for e in range(2)], axis=0)
        o_ref[ti * tq:(ti + 1) * tq, :] = out_t.T.astype(o_ref.dtype)


def _moba(qkv, slopes, bsz, s, n_heads):
    n_blocks = s // MOBA_BLOCK
    n_pairs = n_heads // 2
    blk = MOBA_BLOCK
    tq = MOBA_Q_BLOCKS * blk
    n_tiles = s // tq
    sched = _moba_schedule(n_tiles)
    n_items = sched.shape[1] - 2
    logits = pltpu.VMEM((2, MOBA_Q_BLOCKS, blk, tq), F32)
    probs = pltpu.VMEM((2, MOBA_Q_BLOCKS, blk, tq), BF16)
    return pl.pallas_call(
        functools.partial(_moba_kernel, n_blocks=n_blocks, n_items=n_items),
        grid=(bsz, n_pairs),
        in_specs=[pl.BlockSpec(memory_space=pltpu.SMEM),
                  pl.BlockSpec(memory_space=pltpu.SMEM),
                  pl.BlockSpec((s, LANES), lambda b, h: (b, h)),
                  pl.BlockSpec((s, LANES), lambda b, h: (b, n_pairs + h)),
                  pl.BlockSpec((s, LANES), lambda b, h: (b, 2 * n_pairs + h))],
        out_specs=pl.BlockSpec((s, LANES), lambda b, h: (b, h)),
        out_shape=jax.ShapeDtypeStruct((bsz * s, n_heads * HEAD_DIM), BF16),
        scratch_shapes=[pltpu.VMEM((n_blocks, blk, LANES), BF16),
                        pltpu.VMEM((n_blocks, blk, LANES), BF16),
                        pltpu.VMEM((n_blocks, 2, MOBA_VT_ROWS, blk), BF16),
                        pltpu.VMEM((n_blocks, LANES), F32),
                        pltpu.VMEM((2, blk, blk), F32),
                        pltpu.VMEM((LANES, s), F32),
                        pltpu.VMEM((n_tiles, 2, LANES, tq), BF16),
                        pltpu.VMEM((n_tiles, 2, MOBA_VT_ROWS, tq), F32),
                        logits, logits, probs, probs],
        compiler_params=_params(2),
        name="moba",
    )(slopes, jnp.asarray(sched), qkv, qkv, qkv)


def _memory_attention(q, mem_k, mem_v):
    q = q * (HEAD_DIM ** -0.5)
    lane = _iota(q.shape, 1)
    out = jnp.zeros(q.shape, F32)
    for h in range(MEM_HEADS):
        in_head = (lane >= h * HEAD_DIM) & (lane < (h + 1) * HEAD_DIM)
        qh = jnp.where(in_head, q, 0.0).astype(BF16)
        s = lax.dot_general(qh, mem_k, (((1,), (1,)), ((), ())), preferred_element_type=F32)
        m = jnp.max(s, axis=-1, keepdims=True)
        p = jnp.exp(s - m)
        l = jnp.sum(p, axis=-1, keepdims=True)
        pv = jnp.dot(p.astype(BF16), mem_v, preferred_element_type=F32)
        out = jnp.where(in_head, pv / l, out)
    return out


def _project_out(tok, mem_out, wo_ref, x):
    tw = tok.shape[1]
    return (x + jnp.dot(tok.astype(BF16), wo_ref[0:tw, :], preferred_element_type=F32)
            + jnp.dot(mem_out.astype(BF16), wo_ref[tw:, :], preferred_element_type=F32))


def _s5_mix_out_kernel(y_ref, u_ref, d_ref, wglu_ref, bglu_ref, mq_ref, mk_ref, mv_ref, wo_ref, x_ref, o_ref):
    y = y_ref[...] + d_ref[...] * u_ref[...]
    y = 0.5 * y * (1.0 + lax.erf(y * math.sqrt(0.5)))
    z = jnp.dot(y.astype(BF16), wglu_ref[...], preferred_element_type=F32) + bglu_ref[...]
    tok = y * jax.nn.sigmoid(z)
    mem_out = _memory_attention(mq_ref[...], mk_ref[...], mv_ref[...])
    o_ref[...] = _project_out(tok, mem_out, wo_ref, x_ref[...])


def _moba_mix_out_kernel(tok_ref, mq_ref, mk_ref, mv_ref, wo_ref, x_ref, o_ref):
    mem_out = _memory_attention(mq_ref[...].astype(F32), mk_ref[...], mv_ref[...])
    o_ref[...] = _project_out(tok_ref[...], mem_out, wo_ref, x_ref[...])


def _mix_out_specs(n, s, d, tok_w, mem_len, mq_block_col):
    tm = min(2 * ROW_TILE, s)
    per_b = s // tm
    row = lambda i: (i, 0)
    const = lambda i: (0, 0)
    mem_specs = [pl.BlockSpec((tm, MEM_WIDTH), lambda i: (i, mq_block_col)),
                 pl.BlockSpec((mem_len, MEM_WIDTH), lambda i: (i // per_b, 0)),
                 pl.BlockSpec((mem_len, MEM_WIDTH), lambda i: (i // per_b, 1)),
                 pl.BlockSpec((d, d), const),
                 pl.BlockSpec((tm, d), row)]
    return tm, row, const, mem_specs


def _s5_mix_out(y_ssm, proj, d_skip, w_glu, b_glu, mem_kv, w_out, x2d, s, mem_len):
    n, d = x2d.shape
    tok_w = y_ssm.shape[1]
    tm, row, const, mem_specs = _mix_out_specs(n, s, d, tok_w, mem_len, tok_w // MEM_WIDTH)
    return pl.pallas_call(
        _s5_mix_out_kernel,
        grid=(n // tm,),
        in_specs=[pl.BlockSpec((tm, tok_w), row),
                  pl.BlockSpec((tm, tok_w), row),
                  pl.BlockSpec((1, tok_w), const),
                  pl.BlockSpec((tok_w, tok_w), const),
                  pl.BlockSpec((1, tok_w), const)] + mem_specs,
        out_specs=pl.BlockSpec((tm, d), row),
        out_shape=jax.ShapeDtypeStruct((n, d), F32),
        compiler_params=_params(1),
        name="s5_mix_out",
    )(y_ssm, proj, d_skip.reshape(1, tok_w), w_glu, b_glu.reshape(1, tok_w), proj, mem_kv, mem_kv, w_out, x2d)


def _moba_mix_out(tok, proj, mem_kv, w_out, x2d, s, mem_len):
    n, d = x2d.shape
    tok_w = tok.shape[1]
    tm, row, const, mem_specs = _mix_out_specs(n, s, d, tok_w, mem_len, (proj.shape[1] - MEM_WIDTH) // MEM_WIDTH)
    return pl.pallas_call(
        _moba_mix_out_kernel,
        grid=(n // tm,),
        in_specs=[pl.BlockSpec((tm, tok_w), row)] + mem_specs,
        out_specs=pl.BlockSpec((tm, d), row),
        out_shape=jax.ShapeDtypeStruct((n, d), F32),
        compiler_params=_params(1),
        name="moba_mix_out",
    )(tok, proj, mem_kv, mem_kv, w_out, x2d)


FFN_CHUNK = MXU_TILE


def _ffn_kernel(x_ref, g_ref, wg_ref, wu_ref, wd_ref, gf_ref, o_ref, act_ref, *, final_norm):
    x = x_ref[...]
    h = _rms_scale(x, g_ref[...]).astype(BF16)
    hidden = wg_ref.shape[1]
    for c in range(hidden // FFN_CHUNK):
        cols = slice(c * FFN_CHUNK, (c + 1) * FFN_CHUNK)
        gate = jnp.dot(h, wg_ref[:, cols], preferred_element_type=F32)
        up = jnp.dot(h, wu_ref[:, cols], preferred_element_type=F32)
        act_ref[:, cols] = (gate * jax.nn.sigmoid(gate) * up).astype(BF16)
    y = x + jnp.dot(act_ref[...], wd_ref[...], preferred_element_type=F32)
    if final_norm:
        y = _rms_scale(y, gf_ref[...])
    o_ref[...] = y


def _ffn(x2d, g, w_gate, w_up, w_down, g_final, final_norm):
    n, d = x2d.shape
    hidden = w_gate.shape[1]
    tm = min(ROW_TILE, n)
    row = lambda i: (i, 0)
    const = lambda i: (0, 0)
    return pl.pallas_call(
        functools.partial(_ffn_kernel, final_norm=final_norm),
        grid=(n // tm,),
        in_specs=[pl.BlockSpec((tm, d), row),
                  pl.BlockSpec((1, d), const),
                  pl.BlockSpec((d, hidden), const),
                  pl.BlockSpec((d, hidden), const),
                  pl.BlockSpec((hidden, d), const),
                  pl.BlockSpec((1, d), const)],
        out_specs=pl.BlockSpec((tm, d), row),
        out_shape=jax.ShapeDtypeStruct((n, d), F32),
        scratch_shapes=[pltpu.VMEM((tm, hidden), BF16)],
        compiler_params=_params(1),
        name="ffn",
    )(x2d, g.reshape(1, d), w_gate, w_up, w_down, g_final.reshape(1, d))


def kernel(x, mem, mem_norm_g, w_mem_kv, mix_norm_g, s5_w_in, s5_lambda_re, s5_lambda_im, s5_log_dt,
           s5_b_re, s5_b_im, s5_c_re, s5_c_im, s5_d, s5_w_glu, s5_b_glu, moba_w_in, w_out, ffn_norm_g,
           w_gate, w_up, w_down, final_norm_g):
    bsz, s, d = x.shape
    mem_len = mem.shape[1]
    depth = mix_norm_g.shape[0]
    tok_w = d - MEM_WIDTH
    n_heads = tok_w // HEAD_DIM
    slopes = jnp.asarray(2.0 ** (-8.0 * (np.arange(n_heads) + 1) / n_heads), dtype=F32)

    mem_kv = _norm_matmul(mem.reshape(bsz * mem_len, d), mem_norm_g, w_mem_kv.astype(BF16), out_dtype=BF16)
    x2d = x.reshape(bsz * s, d)
    for i in range(depth):
        j = i // 2
        if i % 2 == 0:
            proj = _norm_matmul(x2d, mix_norm_g[i], s5_w_in[j].astype(BF16), row_tile=2 * ROW_TILE)
            ops = _s5_operators(s5_lambda_re[j], s5_lambda_im[j], s5_log_dt[j],
                                s5_b_re[j], s5_b_im[j], s5_c_re[j], s5_c_im[j])
            y_ssm = _s5_ssm(proj, ops, bsz, s, tok_w)
            x2d = _s5_mix_out(y_ssm, proj, s5_d[j], s5_w_glu[j].astype(BF16), s5_b_glu[j],
                              mem_kv, w_out[i].astype(BF16), x2d, s, mem_len)
        else:
            proj = _norm_matmul(x2d, mix_norm_g[i], moba_w_in[j].astype(BF16), out_dtype=BF16,
                                row_tile=2 * ROW_TILE)
            tok = _moba(proj, slopes, bsz, s, n_heads)
            x2d = _moba_mix_out(tok, proj, mem_kv, w_out[i].astype(BF16), x2d, s, mem_len)
        x2d = _ffn(x2d, ffn_norm_g[i], w_gate[i].astype(BF16), w_up[i].astype(BF16),
                   w_down[i].astype(BF16), final_norm_g, final_norm=(i == depth - 1))
    return x2d.reshape(bsz, s, d)
```

```python
import functools
import math

import numpy as np
import jax
import jax.numpy as jnp
from jax import lax
from jax.experimental import pallas as pl
from jax.experimental.pallas import tpu as pltpu

F32 = jnp.float32
BF16 = jnp.bfloat16

HEAD_DIM = 64
MEM_HEADS = 4
MEM_WIDTH = MEM_HEADS * HEAD_DIM
S5_GROUP_CH = 16
S5_STATE = 64
S5_CHUNK = 16
MOBA_BLOCK = 256
MOBA_TOPK = 3
RMS_EPS = 1e-6
NEG_INF = -1e30

LANES = 128
MXU_TILE = 256
VMEM_LIMIT_BYTES = 56 * 1024 * 1024
ROW_TILE = 512

S5_LANE_GROUPS = LANES // S5_GROUP_CH
S5_STEPS_PER_TILE = MXU_TILE // LANES
S5_LAG_TILES = S5_CHUNK // S5_STEPS_PER_TILE
S5_SEQS_PER_STEP = 2


def _params(n_axes):
    return pltpu.CompilerParams(dimension_semantics=("arbitrary",) * n_axes,
                                vmem_limit_bytes=VMEM_LIMIT_BYTES)


def _iota(shape, dim):
    return lax.broadcasted_iota(jnp.int32, shape, dim)


def _rms_scale(x, g):
    ms = jnp.mean(x * x, axis=-1, keepdims=True)
    return x * lax.rsqrt(ms + RMS_EPS) * g


def _norm_matmul_kernel(x_ref, g_ref, w_ref, o_ref):
    h = _rms_scale(x_ref[...], g_ref[...])
    o_ref[...] = jnp.dot(h.astype(BF16), w_ref[...], preferred_element_type=F32).astype(o_ref.dtype)


def _norm_matmul(x2d, g, w_bf16, out_dtype=F32, row_tile=ROW_TILE):
    n, d = x2d.shape
    f = w_bf16.shape[1]
    tm = min(row_tile, n)
    return pl.pallas_call(
        _norm_matmul_kernel,
        grid=(n // tm,),
        in_specs=[pl.BlockSpec((tm, d), lambda i: (i, 0)),
                  pl.BlockSpec((1, d), lambda i: (0, 0)),
                  pl.BlockSpec((d, f), lambda i: (0, 0))],
        out_specs=pl.BlockSpec((tm, f), lambda i: (i, 0)),
        out_shape=jax.ShapeDtypeStruct((n, f), out_dtype),
        compiler_params=_params(1),
        name="norm_matmul",
    )(x2d, g.reshape(1, d), w_bf16)


def _s5_operators(lam_re, lam_im, log_dt, b_re, b_im, c_re, c_im):
    hi = lax.Precision.HIGHEST
    t_len = S5_CHUNK
    g_cnt, p_cnt = lam_re.shape
    c_cnt = b_re.shape[-1]
    gpb = S5_LANE_GROUPS
    n_lb = g_cnt // gpb
    spt = S5_STEPS_PER_TILE
    lr = lam_re.astype(F32)
    li = lam_im.astype(F32)
    dt = jnp.exp(log_dt.astype(F32))[:, None]
    mag = jnp.exp(lr * dt)
    ar = mag * jnp.cos(li * dt)
    ai = mag * jnp.sin(li * dt)
    den = lr * lr + li * li
    nr = ar - 1.0
    fr = (nr * lr + ai * li) / den
    fi = (ai * lr - nr * li) / den
    br = jnp.swapaxes(b_re.astype(F32), 1, 2)
    bi = jnp.swapaxes(b_im.astype(F32), 1, 2)
    bbr = fr[:, None, :] * br - fi[:, None, :] * bi
    bbi = fr[:, None, :] * bi + fi[:, None, :] * br
    cr = c_re.astype(F32)
    ci = c_im.astype(F32)

    k = jnp.arange(t_len + 1, dtype=F32)[:, None, None]
    pm = jnp.exp(k * (lr * dt)[None])
    pr = pm * jnp.cos(k * (li * dt)[None])
    pi = pm * jnp.sin(k * (li * dt)[None])

    wr = pr[:, :, None, :] * bbr[None] - pi[:, :, None, :] * bbi[None]
    wi = pr[:, :, None, :] * bbi[None] + pi[:, :, None, :] * bbr[None]
    kern = (jnp.einsum('gop,kgcp->kgoc', cr, wr[:t_len], precision=hi)
            - jnp.einsum('gop,kgcp->kgoc', ci, wi[:t_len], precision=hi))
    lag = (jnp.arange(S5_LAG_TILES)[:, None, None] * spt
           + jnp.arange(spt)[None, None, :] - jnp.arange(spt)[None, :, None])
    kd = jnp.where((lag >= 0)[..., None, None, None], kern[jnp.clip(lag, 0, t_len - 1)], 0.0)
    kd = kd.reshape(S5_LAG_TILES, spt, spt, n_lb, gpb, c_cnt, c_cnt)
    lag_t = kd.transpose(3, 0, 2, 5, 1, 4, 6).reshape(n_lb, S5_LAG_TILES, spt * c_cnt, spt * gpb * c_cnt)

    def by_lane_block(x):
        return x.reshape(t_len, n_lb, gpb, c_cnt, 2 * p_cnt).transpose(1, 0, 2, 3, 4).reshape(
            n_lb, t_len * gpb * c_cnt, 2 * p_cnt)

    p_c = by_lane_block(jnp.concatenate([wr[:t_len][::-1], wi[:t_len][::-1]], axis=-1))
    qr = cr[None] * pr[1:, :, None, :] - ci[None] * pi[1:, :, None, :]
    qi = -(cr[None] * pi[1:, :, None, :] + ci[None] * pr[1:, :, None, :])
    q_c = by_lane_block(jnp.concatenate([qr, qi], axis=-1))

    coef = jnp.stack([pr[t_len].reshape(n_lb, gpb * p_cnt),
                      pi[t_len].reshape(n_lb, gpb * p_cnt)], axis=1)
    return lag_t.astype(BF16), p_c.astype(BF16), q_c.astype(BF16), coef


def _s5_ssm_kernel(u_ref, lag_ref, pc_ref, qc_ref, coef_ref, y_ref,
                   toep_scr, p_scr, q_scr, h_scr, s_scr, xp_scr, *, n_chunks, n_seqs):
    t_len = S5_CHUNK
    seq_len = n_chunks * t_len
    gpb = S5_LANE_GROUPS
    st = S5_STATE
    cc = S5_GROUP_CH
    half = gpb * st
    flat_w = t_len * LANES
    nt_dims = (((1,), (1,)), ((), ()))

    @pl.when(pl.program_id(1) == 0)
    def _expand_operators():
        def group_diag(x, row_unit, col_unit):
            same = ((_iota(x.shape, 0) // row_unit) % gpb) == ((_iota(x.shape, 1) // col_unit) % gpb)
            return jnp.where(same, x, 0.0).astype(BF16)

        shape = (2 * st, 2 * half)
        rep = ((_iota(shape, 0) // st == _iota(shape, 1) // half)
               & (_iota(shape, 0) % st == _iota(shape, 1) % st)).astype(BF16)
        for r in range(flat_w // MXU_TILE):
            rows = slice(r * MXU_TILE, (r + 1) * MXU_TILE)
            p_scr[rows, :] = group_diag(jnp.dot(pc_ref[0, rows, :], rep, preferred_element_type=F32), cc, st)
        shape = (2 * half, 2 * st)
        rep = ((_iota(shape, 0) // half == _iota(shape, 1) // st)
               & (_iota(shape, 0) % st == _iota(shape, 1) % st)).astype(BF16)
        for c in range(flat_w // MXU_TILE):
            cols = slice(c * MXU_TILE, (c + 1) * MXU_TILE)
            q_scr[:, cols] = group_diag(lax.dot_general(rep, qc_ref[0, cols, :], nt_dims,
                                                        preferred_element_type=F32), st, cc)
        shape = (MXU_TILE, S5_STEPS_PER_TILE * cc)
        rep = ((_iota(shape, 0) // LANES == _iota(shape, 1) // cc)
               & (_iota(shape, 0) % cc == _iota(shape, 1) % cc)).astype(BF16)
        for l in range(S5_LAG_TILES):
            toep_scr[l] = group_diag(jnp.dot(rep, lag_ref[0, l], preferred_element_type=F32), cc, cc)

    for j in range(t_len):
        for b in range(n_seqs):
            h_scr[b * n_chunks:(b + 1) * n_chunks, j * LANES:(j + 1) * LANES] = (
                u_ref[pl.ds(b * seq_len + j, n_chunks, stride=t_len), :].astype(BF16))
    s_scr[...] = jnp.dot(h_scr[...], p_scr[...], preferred_element_type=F32)
    cr = coef_ref[0, 0:1, :]
    ci = coef_ref[0, 1:2, :]

    xr = [jnp.zeros((1, half), F32)] * n_seqs
    xi = [jnp.zeros((1, half), F32)] * n_seqs
    for n in range(n_chunks):
        for b in range(n_seqs):
            row = slice(b * n_chunks + n, b * n_chunks + n + 1)
            xp_scr[row, 0:half] = xr[b]
            xp_scr[row, half:2 * half] = xi[b]
            sr = s_scr[row, 0:half]
            si = s_scr[row, half:2 * half]
            xr[b], xi[b] = cr * xr[b] - ci * xi[b] + sr, cr * xi[b] + ci * xr[b] + si
    xp = xp_scr[...].astype(BF16)
    for nt in range(S5_LAG_TILES):
        cols = slice(nt * MXU_TILE, (nt + 1) * MXU_TILE)
        acc = jnp.dot(xp, q_scr[:, cols], preferred_element_type=F32)
        for kt in range(nt + 1):
            acc = acc + lax.dot_general(h_scr[:, kt * MXU_TILE:(kt + 1) * MXU_TILE], toep_scr[nt - kt],
                                        nt_dims, preferred_element_type=F32)
        for tt in range(S5_STEPS_PER_TILE):
            t = nt * S5_STEPS_PER_TILE + tt
            for b in range(n_seqs):
                y_ref[pl.ds(b * seq_len + t, n_chunks, stride=t_len), :] = (
                    acc[b * n_chunks:(b + 1) * n_chunks, tt * LANES:(tt + 1) * LANES])


def _s5_ssm(proj, ops, bsz, s, tok_w):
    lag_t, p_c, q_c, coef = ops
    n_lb = tok_w // LANES
    n_chunks = s // S5_CHUNK
    flat_w = p_c.shape[1]
    state_w = S5_LANE_GROUPS * p_c.shape[2]
    const3 = lambda l, b: (l, 0, 0)
    n_seqs = S5_SEQS_PER_STEP if bsz % S5_SEQS_PER_STEP == 0 else 1
    rows = n_seqs * n_chunks
    return pl.pallas_call(
        functools.partial(_s5_ssm_kernel, n_chunks=n_chunks, n_seqs=n_seqs),
        grid=(n_lb, bsz // n_seqs),
        in_specs=[pl.BlockSpec((n_seqs * s, LANES), lambda l, b: (b, l)),
                  pl.BlockSpec((1,) + lag_t.shape[1:], lambda l, b: (l, 0, 0, 0)),
                  pl.BlockSpec((1,) + p_c.shape[1:], const3),
                  pl.BlockSpec((1,) + q_c.shape[1:], const3),
                  pl.BlockSpec((1,) + coef.shape[1:], const3)],
        out_specs=pl.BlockSpec((n_seqs * s, LANES), lambda l, b: (b, l)),
        out_shape=jax.ShapeDtypeStruct((bsz * s, tok_w), F32),
        scratch_shapes=[pltpu.VMEM((S5_LAG_TILES, MXU_TILE, MXU_TILE), BF16),
                        pltpu.VMEM((flat_w, state_w), BF16),
                        pltpu.VMEM((state_w, flat_w), BF16),
                        pltpu.VMEM((rows, flat_w), BF16),
                        pltpu.VMEM((rows, state_w), F32),
                        pltpu.VMEM((rows, state_w), F32)],
        compiler_params=_params(2),
        name="s5_ssm",
    )(proj, lag_t, p_c, q_c, coef)


MOBA_Q_BLOCKS = 2
MOBA_VT_ROWS = HEAD_DIM + 16
LOG2E = math.log2(math.e)
MOBA_AUG_HI = 8
MOBA_ITEMS_PER_STEP = 18


def _moba_schedule(n_tiles):
    rows = []
    for ti in range(n_tiles):
        for n, c in enumerate([ti] + list(range(ti))):
            rows.append((ti, c, int(n == 0)))
    assert len(rows) % 2 == 0, "the pipelined loop retires an even number of work items per step"
    rows += [(rows[-1][0], rows[-1][1], 0)] * 2
    return np.asarray(rows, np.int32).T.copy()


def _moba_kernel(slope_ref, sched_ref, q_ref, k_ref, v_ref, o_ref,
                 ka_ref, kb_ref, vt_ref, km_ref, tri_ref, qt_ref, qa_ref, acc_ref,
                 s0_ref, s1_ref, p0_ref, p1_ref, *, n_blocks, n_items):
    blk = MOBA_BLOCK
    nq = MOBA_Q_BLOCKS
    tq = nq * blk
    n_tiles = n_blocks // nq
    seq = n_blocks * blk
    hp = pl.program_id(1)
    slopes = (slope_ref[2 * hp] * LOG2E, slope_ref[2 * hp + 1] * LOG2E)
    k_refs = (ka_ref, kb_ref)

    lane = _iota((blk, LANES), 1)
    key_off = _iota((blk, LANES), 0).astype(F32)
    ones_row = (_iota((MOBA_VT_ROWS - HEAD_DIM, blk), 0) == 0).astype(BF16)
    in_heads = (lane < HEAD_DIM, lane >= HEAD_DIM)
    aug_lane = (lane - HEAD_DIM, lane)
    key_bias = []
    for e in range(2):
        a = slopes[e] * key_off
        a_hi = a.astype(BF16).astype(F32)
        key_bias.append(jnp.where(aug_lane[e] == 0, a_hi, jnp.where(aug_lane[e] == 1, a - a_hi, 0.0)))
    for j in range(n_blocks):
        kj = k_ref[j * blk:(j + 1) * blk, :].astype(F32)
        for e in range(2):
            one_hot = (aug_lane[e] == MOBA_AUG_HI + j) | (aug_lane[e] == MOBA_AUG_HI + n_blocks + j)
            aug = jnp.where(one_hot, 1.0, key_bias[e])
            k_refs[e][j] = jnp.where(in_heads[e], kj, aug).astype(BF16)
        v_t = v_ref[j * blk:(j + 1) * blk, :].astype(F32).T
        for e in range(2):
            vt_ref[j, e, 0:HEAD_DIM, :] = v_t[e * HEAD_DIM:(e + 1) * HEAD_DIM, :].astype(BF16)
            vt_ref[j, e, HEAD_DIM:MOBA_VT_ROWS, :] = ones_row
        km_ref[j:j + 1, :] = jnp.mean(kj, axis=0, keepdims=True)
    d = _iota((blk, blk), 1) - _iota((blk, blk), 0)
    tri_ref[0] = jnp.zeros((blk, blk), F32)
    tri_ref[1] = jnp.where(d >= 0, 0.0, NEG_INF)

    for ti in range(n_tiles):
        cols = slice(ti * tq, (ti + 1) * tq)
        q_tile = q_ref[cols, :].astype(F32) * (HEAD_DIM ** -0.5 * LOG2E)
        qt_ref[:, cols] = q_tile.T
    q_bf = qt_ref[...].astype(BF16)
    km = km_ref[...]
    lane_m = _iota(km.shape, 1)
    sub = 8
    groups = range(0, n_blocks, sub)
    q_pos = _iota((sub, seq), 1)
    q_blk = q_pos // blk
    blk_ids = [_iota((sub, seq), 0) + g0 for g0 in groups]
    ones_rows = (_iota((MOBA_AUG_HI, tq), 0) < 2).astype(F32)
    pad_rows = jnp.zeros((HEAD_DIM - MOBA_AUG_HI - 2 * n_blocks, tq), F32)
    for e in range(2):
        in_head = (lane_m < HEAD_DIM) if e == 0 else (lane_m >= HEAD_DIM)
        km_e = jnp.where(in_head, km, 0.0).astype(BF16)
        gate = jnp.dot(km_e, q_bf, preferred_element_type=F32)
        gates = [jnp.where(blk_id < q_blk, gate[g0:g0 + sub, :], NEG_INF)
                 for g0, blk_id in zip(groups, blk_ids)]
        bias_hi, bias_lo = [], []
        for g0, blk_id, rows in zip(groups, blk_ids, gates):
            rank = jnp.zeros(rows.shape, jnp.int32)
            for jp in range(n_blocks - 1):
                gj = gates[jp // sub][jp % sub:jp % sub + 1, :]
                if jp < g0:
                    beats = gj >= rows
                elif jp >= g0 + sub:
                    beats = gj > rows
                else:
                    beats = (gj > rows) | ((gj == rows) & (blk_id > jp))
                rank = rank + beats.astype(jnp.int32)
            visible = ((blk_id < q_blk) & (rank < MOBA_TOPK)) | (blk_id == q_blk)
            distance = ((q_blk - blk_id) * blk + q_pos % blk).astype(F32)
            bias = jnp.where(visible, -slopes[e] * distance, NEG_INF)
            hi = bias.astype(BF16).astype(F32)
            bias_hi.append(hi)
            bias_lo.append(bias - hi)
        for ti in range(n_tiles):
            cols = slice(ti * tq, (ti + 1) * tq)
            aug = jnp.concatenate([ones_rows] + [b[:, cols] for b in bias_hi + bias_lo] + [pad_rows],
                                  axis=0)
            q_head = qt_ref[e * HEAD_DIM:(e + 1) * HEAD_DIM, cols]
            parts = [q_head, aug] if e == 0 else [aug, q_head]
            qa_ref[ti, e] = jnp.concatenate(parts, axis=0).astype(BF16)

    def score(w, s_ref):
        ti = sched_ref[0, w]
        c = sched_ref[1, w]
        on_diag = sched_ref[2, w]
        maxes = []
        for e in range(2):
            q_aug = qa_ref[ti, e]
            m_blk = None
            for slab in range(nq):
                s_t = jnp.dot(k_refs[e][nq * c + slab], q_aug, preferred_element_type=F32)
                parts = [s_t[:, h * blk:(h + 1) * blk] + tri_ref[on_diag] if h == slab
                         else s_t[:, h * blk:(h + 1) * blk] for h in range(nq)]
                s_t = jnp.concatenate(parts, axis=1)
                s_ref[e, slab] = s_t
                mx = jnp.max(s_t, axis=0, keepdims=True)
                m_blk = mx if m_blk is None else jnp.maximum(m_blk, mx)
            maxes.append(m_blk)
        return tuple(maxes)

    def exponentiate(s_ref, p_ref, m):
        for e in range(2):
            for slab in range(nq):
                p_ref[e, slab] = jnp.exp2(s_ref[e, slab] - m[e]).astype(BF16)

    def accumulate(w, p_ref, alpha):
        ti = sched_ref[0, w]
        c = sched_ref[1, w]
        for e in range(2):
            pv = None
            for slab in range(nq):
                dd = jnp.dot(vt_ref[nq * c + slab, e], p_ref[e, slab], preferred_element_type=F32)
                pv = dd if pv is None else pv + dd
            acc_ref[ti, e] = alpha[e] * acc_ref[ti, e] + pv

    def retire(w, carry, s_next, p_next, s_free, p_cur):
        m_prev, m_cur, m_blk_next = carry
        starts_tile = sched_ref[2, w] == 1
        next_starts_tile = sched_ref[2, w + 1] == 1
        m_next = tuple(jnp.where(next_starts_tile, m_blk_next[e], jnp.maximum(m_cur[e], m_blk_next[e]))
                       for e in range(2))
        m_blk_new = score(w + 2, s_free)
        exponentiate(s_next, p_next, m_next)
        alpha = tuple(jnp.exp2(jnp.where(starts_tile, NEG_INF, m_prev[e]) - m_cur[e]) for e in range(2))
        accumulate(w, p_cur, alpha)
        return m_cur, m_next, m_blk_new

    acc_ref[...] = jnp.zeros_like(acc_ref)
    m_first = score(0, s0_ref)
    exponentiate(s0_ref, p0_ref, m_first)
    carry = (m_first, m_first, score(1, s1_ref))

    per_step = MOBA_ITEMS_PER_STEP if n_items % MOBA_ITEMS_PER_STEP == 0 else 2

    def body(t, carry):
        for i in range(per_step):
            bufs = (s1_ref, p1_ref, s0_ref, p0_ref) if i % 2 == 0 else (s0_ref, p0_ref, s1_ref, p1_ref)
            carry = retire(per_step * t + i, carry, *bufs)
        return carry

    lax.fori_loop(0, n_items // per_step, body, carry)

    for ti in range(n_tiles):
        out_t = jnp.concatenate([acc_ref[ti, e, 0:HEAD_DIM, :] / acc_ref[ti, e, HEAD_DIM:HEAD_DIM + 1, :]
                                 for e in range(2)], axis=0)
        o_ref[ti * tq:(ti + 1) * tq, :] = out_t.T.astype(o_ref.dtype)


def _moba(qkv, slopes, bsz, s, n_heads):
    n_blocks = s // MOBA_BLOCK
    n_pairs = n_heads // 2
    blk = MOBA_BLOCK
    tq = MOBA_Q_BLOCKS * blk
    n_tiles = s // tq
    sched = _moba_schedule(n_tiles)
    n_items = sched.shape[1] - 2
    logits = pltpu.VMEM((2, MOBA_Q_BLOCKS, blk, tq), F32)
    probs = pltpu.VMEM((2, MOBA_Q_BLOCKS, blk, tq), BF16)
    return pl.pallas_call(
        functools.partial(_moba_kernel, n_blocks=n_blocks, n_items=n_items),
        grid=(bsz, n_pairs),
        in_specs=[pl.BlockSpec(memory_space=pltpu.SMEM),
                  pl.BlockSpec(memory_space=pltpu.SMEM),
                  pl.BlockSpec((s, LANES), lambda b, h: (b, h)),
                  pl.BlockSpec((s, LANES), lambda b, h: (b, n_pairs + h)),
                  pl.BlockSpec((s, LANES), lambda b, h: (b, 2 * n_pairs + h))],
        out_specs=pl.BlockSpec((s, LANES), lambda b, h: (b, h)),
        out_shape=jax.ShapeDtypeStruct((bsz * s, n_heads * HEAD_DIM), BF16),
        scratch_shapes=[pltpu.VMEM((n_blocks, blk, LANES), BF16),
                        pltpu.VMEM((n_blocks, blk, LANES), BF16),
                        pltpu.VMEM((n_blocks, 2, MOBA_VT_ROWS, blk), BF16),
                        pltpu.VMEM((n_blocks, LANES), F32),
                        pltpu.VMEM((2, blk, blk), F32),
                        pltpu.VMEM((LANES, s), F32),
                        pltpu.VMEM((n_tiles, 2, LANES, tq), BF16),
                        pltpu.VMEM((n_tiles, 2, MOBA_VT_ROWS, tq), F32),
                        logits, logits, probs, probs],
        compiler_params=_params(2),
        name="moba",
    )(slopes, jnp.asarray(sched), qkv, qkv, qkv)


def _memory_attention(q, mem_k, mem_v):
    q = q * (HEAD_DIM ** -0.5)
    lane = _iota(q.shape, 1)
    out = jnp.zeros(q.shape, F32)
    for h in range(MEM_HEADS):
        in_head = (lane >= h * HEAD_DIM) & (lane < (h + 1) * HEAD_DIM)
        qh = jnp.where(in_head, q, 0.0).astype(BF16)
        s = lax.dot_general(qh, mem_k, (((1,), (1,)), ((), ())), preferred_element_type=F32)
        m = jnp.max(s, axis=-1, keepdims=True)
        p = jnp.exp(s - m)
        l = jnp.sum(p, axis=-1, keepdims=True)
        pv = jnp.dot(p.astype(BF16), mem_v, preferred_element_type=F32)
        out = jnp.where(in_head, pv / l, out)
    return out


def _project_out(tok, mem_out, wo_ref, x):
    tw = tok.shape[1]
    return (x + jnp.dot(tok.astype(BF16), wo_ref[0:tw, :], preferred_element_type=F32)
            + jnp.dot(mem_out.astype(BF16), wo_ref[tw:, :], preferred_element_type=F32))


def _s5_mix_out_kernel(y_ref, u_ref, d_ref, wglu_ref, bglu_ref, mq_ref, mk_ref, mv_ref, wo_ref, x_ref, o_ref):
    y = y_ref[...] + d_ref[...] * u_ref[...]
    y = 0.5 * y * (1.0 + lax.erf(y * math.sqrt(0.5)))
    z = jnp.dot(y.astype(BF16), wglu_ref[...], preferred_element_type=F32) + bglu_ref[...]
    tok = y * jax.nn.sigmoid(z)
    mem_out = _memory_attention(mq_ref[...], mk_ref[...], mv_ref[...])
    o_ref[...] = _project_out(tok, mem_out, wo_ref, x_ref[...])


def _moba_mix_out_kernel(tok_ref, mq_ref, mk_ref, mv_ref, wo_ref, x_ref, o_ref):
    mem_out = _memory_attention(mq_ref[...].astype(F32), mk_ref[...], mv_ref[...])
    o_ref[...] = _project_out(tok_ref[...], mem_out, wo_ref, x_ref[...])


def _mix_out_specs(n, s, d, tok_w, mem_len, mq_block_col):
    tm = min(2 * ROW_TILE, s)
    per_b = s // tm
    row = lambda i: (i, 0)
    const = lambda i: (0, 0)
    mem_specs = [pl.BlockSpec((tm, MEM_WIDTH), lambda i: (i, mq_block_col)),
                 pl.BlockSpec((mem_len, MEM_WIDTH), lambda i: (i // per_b, 0)),
                 pl.BlockSpec((mem_len, MEM_WIDTH), lambda i: (i // per_b, 1)),
                 pl.BlockSpec((d, d), const),
                 pl.BlockSpec((tm, d), row)]
    return tm, row, const, mem_specs


def _s5_mix_out(y_ssm, proj, d_skip, w_glu, b_glu, mem_kv, w_out, x2d, s, mem_len):
    n, d = x2d.shape
    tok_w = y_ssm.shape[1]
    tm, row, const, mem_specs = _mix_out_specs(n, s, d, tok_w, mem_len, tok_w // MEM_WIDTH)
    return pl.pallas_call(
        _s5_mix_out_kernel,
        grid=(n // tm,),
        in_specs=[pl.BlockSpec((tm, tok_w), row),
                  pl.BlockSpec((tm, tok_w), row),
                  pl.BlockSpec((1, tok_w), const),
                  pl.BlockSpec((tok_w, tok_w), const),
                  pl.BlockSpec((1, tok_w), const)] + mem_specs,
        out_specs=pl.BlockSpec((tm, d), row),
        out_shape=jax.ShapeDtypeStruct((n, d), F32),
        compiler_params=_params(1),
        name="s5_mix_out",
    )(y_ssm, proj, d_skip.reshape(1, tok_w), w_glu, b_glu.reshape(1, tok_w), proj, mem_kv, mem_kv, w_out, x2d)


def _moba_mix_out(tok, proj, mem_kv, w_out, x2d, s, mem_len):
    n, d = x2d.shape
    tok_w = tok.shape[1]
    tm, row, const, mem_specs = _mix_out_specs(n, s, d, tok_w, mem_len, (proj.shape[1] - MEM_WIDTH) // MEM_WIDTH)
    return pl.pallas_call(
        _moba_mix_out_kernel,
        grid=(n // tm,),
        in_specs=[pl.BlockSpec((tm, tok_w), row)] + mem_specs,
        out_specs=pl.BlockSpec((tm, d), row),
        out_shape=jax.ShapeDtypeStruct((n, d), F32),
        compiler_params=_params(1),
        name="moba_mix_out",
    )(tok, proj, mem_kv, mem_kv, w_out, x2d)


FFN_CHUNK = MXU_TILE


def _ffn_kernel(x_ref, g_ref, wg_ref, wu_ref, wd_ref, gf_ref, o_ref, act_ref, *, final_norm):
    x = x_ref[...]
    h = _rms_scale(x, g_ref[...]).astype(BF16)
    hidden = wg_ref.shape[1]
    for c in range(hidden // FFN_CHUNK):
        cols = slice(c * FFN_CHUNK, (c + 1) * FFN_CHUNK)
        gate = jnp.dot(h, wg_ref[:, cols], preferred_element_type=F32)
        up = jnp.dot(h, wu_ref[:, cols], preferred_element_type=F32)
        act_ref[:, cols] = (gate * jax.nn.sigmoid(gate) * up).astype(BF16)
    y = x + jnp.dot(act_ref[...], wd_ref[...], preferred_element_type=F32)
    if final_norm:
        y = _rms_scale(y, gf_ref[...])
    o_ref[...] = y


def _ffn(x2d, g, w_gate, w_up, w_down, g_final, final_norm):
    n, d = x2d.shape
    hidden = w_gate.shape[1]
    tm = min(2 * ROW_TILE, n)
    row = lambda i: (i, 0)
    const = lambda i: (0, 0)
    resident = pl.Buffered(1)
    return pl.pallas_call(
        functools.partial(_ffn_kernel, final_norm=final_norm),
        grid=(n // tm,),
        in_specs=[pl.BlockSpec((tm, d), row),
                  pl.BlockSpec((1, d), const),
                  pl.BlockSpec((d, hidden), const, pipeline_mode=resident),
                  pl.BlockSpec((d, hidden), const, pipeline_mode=resident),
                  pl.BlockSpec((hidden, d), const, pipeline_mode=resident),
                  pl.BlockSpec((1, d), const)],
        out_specs=pl.BlockSpec((tm, d), row),
        out_shape=jax.ShapeDtypeStruct((n, d), F32),
        scratch_shapes=[pltpu.VMEM((tm, hidden), BF16)],
        compiler_params=_params(1),
        name="ffn",
    )(x2d, g.reshape(1, d), w_gate, w_up, w_down, g_final.reshape(1, d))


def kernel(x, mem, mem_norm_g, w_mem_kv, mix_norm_g, s5_w_in, s5_lambda_re, s5_lambda_im, s5_log_dt,
           s5_b_re, s5_b_im, s5_c_re, s5_c_im, s5_d, s5_w_glu, s5_b_glu, moba_w_in, w_out, ffn_norm_g,
           w_gate, w_up, w_down, final_norm_g):
    bsz, s, d = x.shape
    mem_len = mem.shape[1]
    depth = mix_norm_g.shape[0]
    tok_w = d - MEM_WIDTH
    n_heads = tok_w // HEAD_DIM
    slopes = jnp.asarray(2.0 ** (-8.0 * (np.arange(n_heads) + 1) / n_heads), dtype=F32)

    mem_kv = _norm_matmul(mem.reshape(bsz * mem_len, d), mem_norm_g, w_mem_kv.astype(BF16), out_dtype=BF16)
    x2d = x.reshape(bsz * s, d)
    for i in range(depth):
        j = i // 2
        if i % 2 == 0:
            proj = _norm_matmul(x2d, mix_norm_g[i], s5_w_in[j].astype(BF16), row_tile=2 * ROW_TILE)
            ops = _s5_operators(s5_lambda_re[j], s5_lambda_im[j], s5_log_dt[j],
                                s5_b_re[j], s5_b_im[j], s5_c_re[j], s5_c_im[j])
            y_ssm = _s5_ssm(proj, ops, bsz, s, tok_w)
            x2d = _s5_mix_out(y_ssm, proj, s5_d[j], s5_w_glu[j].astype(BF16), s5_b_glu[j],
                              mem_kv, w_out[i].astype(BF16), x2d, s, mem_len)
        else:
            proj = _norm_matmul(x2d, mix_norm_g[i], moba_w_in[j].astype(BF16), out_dtype=BF16,
                                row_tile=2 * ROW_TILE)
            tok = _moba(proj, slopes, bsz, s, n_heads)
            x2d = _moba_mix_out(tok, proj, mem_kv, w_out[i].astype(BF16), x2d, s, mem_len)
        x2d = _ffn(x2d, ffn_norm_g[i], w_gate[i].astype(BF16), w_up[i].astype(BF16),
                   w_down[i].astype(BF16), final_norm_g, final_norm=(i == depth - 1))
    return x2d.reshape(bsz, s, d)
```

```python
import functools
import math

import numpy as np
import jax
import jax.numpy as jnp
from jax import lax
from jax.experimental import pallas as pl
from jax.experimental.pallas import tpu as pltpu

F32 = jnp.float32
BF16 = jnp.bfloat16

HEAD_DIM = 64
MEM_HEADS = 4
MEM_WIDTH = MEM_HEADS * HEAD_DIM
S5_GROUP_CH = 16
S5_STATE = 64
S5_CHUNK = 16
MOBA_BLOCK = 256
MOBA_TOPK = 3
RMS_EPS = 1e-6
NEG_INF = -1e30

LANES = 128
MXU_TILE = 256
VMEM_LIMIT_BYTES = 56 * 1024 * 1024
ROW_TILE = 512

S5_LANE_GROUPS = LANES // S5_GROUP_CH
S5_STEPS_PER_TILE = MXU_TILE // LANES
S5_LAG_TILES = S5_CHUNK // S5_STEPS_PER_TILE
S5_SEQS_PER_STEP = 2


def _params(n_axes):
    return pltpu.CompilerParams(dimension_semantics=("arbitrary",) * n_axes,
                                vmem_limit_bytes=VMEM_LIMIT_BYTES)


def _iota(shape, dim):
    return lax.broadcasted_iota(jnp.int32, shape, dim)


def _rms_scale(x, g):
    ms = jnp.mean(x * x, axis=-1, keepdims=True)
    return x * lax.rsqrt(ms + RMS_EPS) * g


def _norm_matmul_kernel(x_ref, g_ref, w_ref, o_ref):
    h = _rms_scale(x_ref[...], g_ref[...])
    o_ref[...] = jnp.dot(h.astype(BF16), w_ref[...], preferred_element_type=F32).astype(o_ref.dtype)


def _norm_matmul(x2d, g, w_bf16, out_dtype=F32, row_tile=ROW_TILE):
    n, d = x2d.shape
    f = w_bf16.shape[1]
    tm = min(row_tile, n)
    return pl.pallas_call(
        _norm_matmul_kernel,
        grid=(n // tm,),
        in_specs=[pl.BlockSpec((tm, d), lambda i: (i, 0)),
                  pl.BlockSpec((1, d), lambda i: (0, 0)),
                  pl.BlockSpec((d, f), lambda i: (0, 0))],
        out_specs=pl.BlockSpec((tm, f), lambda i: (i, 0)),
        out_shape=jax.ShapeDtypeStruct((n, f), out_dtype),
        compiler_params=_params(1),
        name="norm_matmul",
    )(x2d, g.reshape(1, d), w_bf16)


def _s5_operators(lam_re, lam_im, log_dt, b_re, b_im, c_re, c_im):
    hi = lax.Precision.HIGHEST
    t_len = S5_CHUNK
    g_cnt, p_cnt = lam_re.shape
    c_cnt = b_re.shape[-1]
    gpb = S5_LANE_GROUPS
    n_lb = g_cnt // gpb
    spt = S5_STEPS_PER_TILE
    lr = lam_re.astype(F32)
    li = lam_im.astype(F32)
    dt = jnp.exp(log_dt.astype(F32))[:, None]
    mag = jnp.exp(lr * dt)
    ar = mag * jnp.cos(li * dt)
    ai = mag * jnp.sin(li * dt)
    den = lr * lr + li * li
    nr = ar - 1.0
    fr = (nr * lr + ai * li) / den
    fi = (ai * lr - nr * li) / den
    br = jnp.swapaxes(b_re.astype(F32), 1, 2)
    bi = jnp.swapaxes(b_im.astype(F32), 1, 2)
    bbr = fr[:, None, :] * br - fi[:, None, :] * bi
    bbi = fr[:, None, :] * bi + fi[:, None, :] * br
    cr = c_re.astype(F32)
    ci = c_im.astype(F32)

    k = jnp.arange(t_len + 1, dtype=F32)[:, None, None]
    pm = jnp.exp(k * (lr * dt)[None])
    pr = pm * jnp.cos(k * (li * dt)[None])
    pi = pm * jnp.sin(k * (li * dt)[None])

    wr = pr[:, :, None, :] * bbr[None] - pi[:, :, None, :] * bbi[None]
    wi = pr[:, :, None, :] * bbi[None] + pi[:, :, None, :] * bbr[None]
    kern = (jnp.einsum('gop,kgcp->kgoc', cr, wr[:t_len], precision=hi)
            - jnp.einsum('gop,kgcp->kgoc', ci, wi[:t_len], precision=hi))
    lag = (jnp.arange(S5_LAG_TILES)[:, None, None] * spt
           + jnp.arange(spt)[None, None, :] - jnp.arange(spt)[None, :, None])
    kd = jnp.where((lag >= 0)[..., None, None, None], kern[jnp.clip(lag, 0, t_len - 1)], 0.0)
    kd = kd.reshape(S5_LAG_TILES, spt, spt, n_lb, gpb, c_cnt, c_cnt)
    lag_t = kd.transpose(3, 0, 2, 5, 1, 4, 6).reshape(n_lb, S5_LAG_TILES, spt * c_cnt, spt * gpb * c_cnt)

    def by_lane_block(x):
        return x.reshape(t_len, n_lb, gpb, c_cnt, 2 * p_cnt).transpose(1, 0, 2, 3, 4).reshape(
            n_lb, t_len * gpb * c_cnt, 2 * p_cnt)

    p_c = by_lane_block(jnp.concatenate([wr[:t_len][::-1], wi[:t_len][::-1]], axis=-1))
    qr = cr[None] * pr[1:, :, None, :] - ci[None] * pi[1:, :, None, :]
    qi = -(cr[None] * pi[1:, :, None, :] + ci[None] * pr[1:, :, None, :])
    q_c = by_lane_block(jnp.concatenate([qr, qi], axis=-1))

    coef = jnp.stack([pr[t_len].reshape(n_lb, gpb * p_cnt),
                      pi[t_len].reshape(n_lb, gpb * p_cnt)], axis=1)
    return lag_t.astype(BF16), p_c.astype(BF16), q_c.astype(BF16), coef


def _s5_ssm_kernel(u_ref, lag_ref, pc_ref, qc_ref, coef_ref, y_ref,
                   toep_scr, p_scr, q_scr, h_scr, s_scr, xp_scr, *, n_chunks, n_seqs):
    t_len = S5_CHUNK
    seq_len = n_chunks * t_len
    gpb = S5_LANE_GROUPS
    st = S5_STATE
    cc = S5_GROUP_CH
    half = gpb * st
    flat_w = t_len * LANES
    nt_dims = (((1,), (1,)), ((), ()))

    @pl.when(pl.program_id(1) == 0)
    def _expand_operators():
        def group_diag(x, row_unit, col_unit):
            same = ((_iota(x.shape, 0) // row_unit) % gpb) == ((_iota(x.shape, 1) // col_unit) % gpb)
            return jnp.where(same, x, 0.0).astype(BF16)

        shape = (2 * st, 2 * half)
        rep = ((_iota(shape, 0) // st == _iota(shape, 1) // half)
               & (_iota(shape, 0) % st == _iota(shape, 1) % st)).astype(BF16)
        for r in range(flat_w // MXU_TILE):
            rows = slice(r * MXU_TILE, (r + 1) * MXU_TILE)
            p_scr[rows, :] = group_diag(jnp.dot(pc_ref[0, rows, :], rep, preferred_element_type=F32), cc, st)
        shape = (2 * half, 2 * st)
        rep = ((_iota(shape, 0) // half == _iota(shape, 1) // st)
               & (_iota(shape, 0) % st == _iota(shape, 1) % st)).astype(BF16)
        for c in range(flat_w // MXU_TILE):
            cols = slice(c * MXU_TILE, (c + 1) * MXU_TILE)
            q_scr[:, cols] = group_diag(lax.dot_general(rep, qc_ref[0, cols, :], nt_dims,
                                                        preferred_element_type=F32), st, cc)
        shape = (MXU_TILE, S5_STEPS_PER_TILE * cc)
        rep = ((_iota(shape, 0) // LANES == _iota(shape, 1) // cc)
               & (_iota(shape, 0) % cc == _iota(shape, 1) % cc)).astype(BF16)
        for l in range(S5_LAG_TILES):
            toep_scr[l] = group_diag(jnp.dot(rep, lag_ref[0, l], preferred_element_type=F32), cc, cc)

    for j in range(t_len):
        for b in range(n_seqs):
            h_scr[b * n_chunks:(b + 1) * n_chunks, j * LANES:(j + 1) * LANES] = (
                u_ref[pl.ds(b * seq_len + j, n_chunks, stride=t_len), :].astype(BF16))
    s_scr[...] = jnp.dot(h_scr[...], p_scr[...], preferred_element_type=F32)
    cr = coef_ref[0, 0:1, :]
    ci = coef_ref[0, 1:2, :]

    xr = [jnp.zeros((1, half), F32)] * n_seqs
    xi = [jnp.zeros((1, half), F32)] * n_seqs
    for n in range(n_chunks):
        for b in range(n_seqs):
            row = slice(b * n_chunks + n, b * n_chunks + n + 1)
            xp_scr[row, 0:half] = xr[b]
            xp_scr[row, half:2 * half] = xi[b]
            sr = s_scr[row, 0:half]
            si = s_scr[row, half:2 * half]
            xr[b], xi[b] = cr * xr[b] - ci * xi[b] + sr, cr * xi[b] + ci * xr[b] + si
    xp = xp_scr[...].astype(BF16)
    for nt in range(S5_LAG_TILES):
        cols = slice(nt * MXU_TILE, (nt + 1) * MXU_TILE)
        acc = jnp.dot(xp, q_scr[:, cols], preferred_element_type=F32)
        for kt in range(nt + 1):
            acc = acc + lax.dot_general(h_scr[:, kt * MXU_TILE:(kt + 1) * MXU_TILE], toep_scr[nt - kt],
                                        nt_dims, preferred_element_type=F32)
        for tt in range(S5_STEPS_PER_TILE):
            t = nt * S5_STEPS_PER_TILE + tt
            for b in range(n_seqs):
                y_ref[pl.ds(b * seq_len + t, n_chunks, stride=t_len), :] = (
                    acc[b * n_chunks:(b + 1) * n_chunks, tt * LANES:(tt + 1) * LANES])


def _s5_ssm(proj, ops, bsz, s, tok_w):
    lag_t, p_c, q_c, coef = ops
    n_lb = tok_w // LANES
    n_chunks = s // S5_CHUNK
    flat_w = p_c.shape[1]
    state_w = S5_LANE_GROUPS * p_c.shape[2]
    const3 = lambda l, b: (l, 0, 0)
    n_seqs = S5_SEQS_PER_STEP if bsz % S5_SEQS_PER_STEP == 0 else 1
    rows = n_seqs * n_chunks
    return pl.pallas_call(
        functools.partial(_s5_ssm_kernel, n_chunks=n_chunks, n_seqs=n_seqs),
        grid=(n_lb, bsz // n_seqs),
        in_specs=[pl.BlockSpec((n_seqs * s, LANES), lambda l, b: (b, l)),
                  pl.BlockSpec((1,) + lag_t.shape[1:], lambda l, b: (l, 0, 0, 0)),
                  pl.BlockSpec((1,) + p_c.shape[1:], const3),
                  pl.BlockSpec((1,) + q_c.shape[1:], const3),
                  pl.BlockSpec((1,) + coef.shape[1:], const3)],
        out_specs=pl.BlockSpec((n_seqs * s, LANES), lambda l, b: (b, l)),
        out_shape=jax.ShapeDtypeStruct((bsz * s, tok_w), F32),
        scratch_shapes=[pltpu.VMEM((S5_LAG_TILES, MXU_TILE, MXU_TILE), BF16),
                        pltpu.VMEM((flat_w, state_w), BF16),
                        pltpu.VMEM((state_w, flat_w), BF16),
                        pltpu.VMEM((rows, flat_w), BF16),
                        pltpu.VMEM((rows, state_w), F32),
                        pltpu.VMEM((rows, state_w), F32)],
        compiler_params=_params(2),
        name="s5_ssm",
    )(proj, lag_t, p_c, q_c, coef)


MOBA_Q_BLOCKS = 2
MOBA_VT_ROWS = HEAD_DIM + 16
LOG2E = math.log2(math.e)
MOBA_AUG_HI = 8
MOBA_ITEMS_PER_STEP = 18


def _moba_schedule(n_tiles):
    rows = []
    for ti in range(n_tiles):
        for n, c in enumerate([ti] + list(range(ti))):
            rows.append((ti, c, int(n == 0)))
    assert len(rows) % 2 == 0, "the pipelined loop retires an even number of work items per step"
    rows += [(rows[-1][0], rows[-1][1], 0)] * 2
    return np.asarray(rows, np.int32).T.copy()


def _moba_kernel(slope_ref, sched_ref, q_ref, k_ref, v_ref, o_ref,
                 ka_ref, kb_ref, vt_ref, km_ref, tri_ref, qt_ref, qa_ref, acc_ref,
                 s0_ref, s1_ref, p0_ref, p1_ref, *, n_blocks, n_items):
    blk = MOBA_BLOCK
    nq = MOBA_Q_BLOCKS
    tq = nq * blk
    n_tiles = n_blocks // nq
    seq = n_blocks * blk
    hp = pl.program_id(1)
    slopes = (slope_ref[2 * hp] * LOG2E, slope_ref[2 * hp + 1] * LOG2E)
    k_refs = (ka_ref, kb_ref)

    lane = _iota((blk, LANES), 1)
    key_off = _iota((blk, LANES), 0).astype(F32)
    ones_row = (_iota((MOBA_VT_ROWS - HEAD_DIM, blk), 0) == 0).astype(BF16)
    in_heads = (lane < HEAD_DIM, lane >= HEAD_DIM)
    aug_lane = (lane - HEAD_DIM, lane)
    key_bias = []
    for e in range(2):
        a = slopes[e] * key_off
        a_hi = a.astype(BF16).astype(F32)
        key_bias.append(jnp.where(aug_lane[e] == 0, a_hi, jnp.where(aug_lane[e] == 1, a - a_hi, 0.0)))
    for j in range(n_blocks):
        kj = k_ref[j * blk:(j + 1) * blk, :].astype(F32)
        for e in range(2):
            one_hot = (aug_lane[e] == MOBA_AUG_HI + j) | (aug_lane[e] == MOBA_AUG_HI + n_blocks + j)
            aug = jnp.where(one_hot, 1.0, key_bias[e])
            k_refs[e][j] = jnp.where(in_heads[e], kj, aug).astype(BF16)
        v_t = v_ref[j * blk:(j + 1) * blk, :].astype(F32).T
        for e in range(2):
            vt_ref[j, e, 0:HEAD_DIM, :] = v_t[e * HEAD_DIM:(e + 1) * HEAD_DIM, :].astype(BF16)
            vt_ref[j, e, HEAD_DIM:MOBA_VT_ROWS, :] = ones_row
        km_ref[j:j + 1, :] = jnp.mean(kj, axis=0, keepdims=True)
    d = _iota((blk, blk), 1) - _iota((blk, blk), 0)
    tri_ref[0] = jnp.zeros((blk, blk), F32)
    tri_ref[1] = jnp.where(d >= 0, 0.0, NEG_INF)

    for ti in range(n_tiles):
        cols = slice(ti * tq, (ti + 1) * tq)
        q_tile = q_ref[cols, :].astype(F32) * (HEAD_DIM ** -0.5 * LOG2E)
        qt_ref[:, cols] = q_tile.T
    q_bf = qt_ref[...].astype(BF16)
    km = km_ref[...]
    lane_m = _iota(km.shape, 1)
    sub = 8
    groups = range(0, n_blocks, sub)
    q_off = _iota((sub, blk), 1)
    row_ids = [_iota((sub, blk), 0) + g0 for g0 in groups]
    hidden_rows = jnp.full((sub, blk), NEG_INF, F32)
    ones_rows = (_iota((MOBA_AUG_HI, tq), 0) < 2).astype(F32)
    pad_rows = jnp.zeros((HEAD_DIM - MOBA_AUG_HI - 2 * n_blocks, tq), F32)
    for e in range(2):
        in_head = (lane_m < HEAD_DIM) if e == 0 else (lane_m >= HEAD_DIM)
        km_e = jnp.where(in_head, km, 0.0).astype(BF16)
        gate = jnp.dot(km_e, q_bf, preferred_element_type=F32)
        bias_hi = [[] for _ in groups]
        bias_lo = [[] for _ in groups]
        for qb in range(n_blocks):
            cols = slice(qb * blk, (qb + 1) * blk)
            for gi, g0 in enumerate(groups):
                if g0 > qb:
                    bias_hi[gi].append(hidden_rows)
                    bias_lo[gi].append(jnp.zeros((sub, blk), F32))
                    continue
                rows = gate[g0:g0 + sub, cols]
                row_id = row_ids[gi]
                rank = jnp.zeros((sub, blk), jnp.int32)
                for jp in range(qb):
                    gj = gate[jp:jp + 1, cols]
                    if jp < g0:
                        beats = gj >= rows
                    elif jp >= g0 + sub:
                        beats = gj > rows
                    else:
                        beats = (gj > rows) | ((gj == rows) & (row_id > jp))
                    rank = rank + beats.astype(jnp.int32)
                visible = ((row_id < qb) & (rank < MOBA_TOPK)) | (row_id == qb)
                distance = ((qb - row_id) * blk + q_off).astype(F32)
                bias = jnp.where(visible, -slopes[e] * distance, NEG_INF)
                hi = bias.astype(BF16).astype(F32)
                bias_hi[gi].append(hi)
                bias_lo[gi].append(bias - hi)
        for ti in range(n_tiles):
            cols = slice(ti * tq, (ti + 1) * tq)
            pieces = [jnp.concatenate(per_group[nq * ti:nq * (ti + 1)], axis=1) for per_group in bias_hi + bias_lo]
            aug = jnp.concatenate([ones_rows] + pieces + [pad_rows], axis=0)
            q_head = qt_ref[e * HEAD_DIM:(e + 1) * HEAD_DIM, cols]
            parts = [q_head, aug] if e == 0 else [aug, q_head]
            qa_ref[ti, e] = jnp.concatenate(parts, axis=0).astype(BF16)

    def score(w, s_ref):
        ti = sched_ref[0, w]
        c = sched_ref[1, w]
        on_diag = sched_ref[2, w]
        maxes = []
        for e in range(2):
            q_aug = qa_ref[ti, e]
            m_blk = None
            for slab in range(nq):
                s_t = jnp.dot(k_refs[e][nq * c + slab], q_aug, preferred_element_type=F32)
                parts = [s_t[:, h * blk:(h + 1) * blk] + tri_ref[on_diag] if h == slab
                         else s_t[:, h * blk:(h + 1) * blk] for h in range(nq)]
                s_t = jnp.concatenate(parts, axis=1)
                s_ref[e, slab] = s_t
                mx = jnp.max(s_t, axis=0, keepdims=True)
                m_blk = mx if m_blk is None else jnp.maximum(m_blk, mx)
            maxes.append(m_blk)
        return tuple(maxes)

    def exponentiate(s_ref, p_ref, m):
        for e in range(2):
            for slab in range(nq):
                p_ref[e, slab] = jnp.exp2(s_ref[e, slab] - m[e]).astype(BF16)

    def accumulate(w, p_ref, alpha):
        ti = sched_ref[0, w]
        c = sched_ref[1, w]
        for e in range(2):
            pv = None
            for slab in range(nq):
                dd = jnp.dot(vt_ref[nq * c + slab, e], p_ref[e, slab], preferred_element_type=F32)
                pv = dd if pv is None else pv + dd
            acc_ref[ti, e] = alpha[e] * acc_ref[ti, e] + pv

    def retire(w, carry, s_next, p_next, s_free, p_cur):
        m_prev, m_cur, m_blk_next = carry
        starts_tile = sched_ref[2, w] == 1
        next_starts_tile = sched_ref[2, w + 1] == 1
        m_next = tuple(jnp.where(next_starts_tile, m_blk_next[e], jnp.maximum(m_cur[e], m_blk_next[e]))
                       for e in range(2))
        m_blk_new = score(w + 2, s_free)
        exponentiate(s_next, p_next, m_next)
        alpha = tuple(jnp.exp2(jnp.where(starts_tile, NEG_INF, m_prev[e]) - m_cur[e]) for e in range(2))
        accumulate(w, p_cur, alpha)
        return m_cur, m_next, m_blk_new

    acc_ref[...] = jnp.zeros_like(acc_ref)
    m_first = score(0, s0_ref)
    exponentiate(s0_ref, p0_ref, m_first)
    carry = (m_first, m_first, score(1, s1_ref))

    per_step = MOBA_ITEMS_PER_STEP if n_items % MOBA_ITEMS_PER_STEP == 0 else 2

    def body(t, carry):
        for i in range(per_step):
            bufs = (s1_ref, p1_ref, s0_ref, p0_ref) if i % 2 == 0 else (s0_ref, p0_ref, s1_ref, p1_ref)
            carry = retire(per_step * t + i, carry, *bufs)
        return carry

    lax.fori_loop(0, n_items // per_step, body, carry)

    for ti in range(n_tiles):
        out_t = jnp.concatenate([acc_ref[ti, e, 0:HEAD_DIM, :] / acc_ref[ti, e, HEAD_DIM:HEAD_DIM + 1, :]
                                 for e in range(2)], axis=0)
        o_ref[ti * tq:(ti + 1) * tq, :] = out_t.T.astype(o_ref.dtype)


def _moba(qkv, slopes, bsz, s, n_heads):
    n_blocks = s // MOBA_BLOCK
    n_pairs = n_heads // 2
    blk = MOBA_BLOCK
    tq = MOBA_Q_BLOCKS * blk
    n_tiles = s // tq
    sched = _moba_schedule(n_tiles)
    n_items = sched.shape[1] - 2
    logits = pltpu.VMEM((2, MOBA_Q_BLOCKS, blk, tq), F32)
    probs = pltpu.VMEM((2, MOBA_Q_BLOCKS, blk, tq), BF16)
    return pl.pallas_call(
        functools.partial(_moba_kernel, n_blocks=n_blocks, n_items=n_items),
        grid=(bsz, n_pairs),
        in_specs=[pl.BlockSpec(memory_space=pltpu.SMEM),
                  pl.BlockSpec(memory_space=pltpu.SMEM),
                  pl.BlockSpec((s, LANES), lambda b, h: (b, h)),
                  pl.BlockSpec((s, LANES), lambda b, h: (b, n_pairs + h)),
                  pl.BlockSpec((s, LANES), lambda b, h: (b, 2 * n_pairs + h))],
        out_specs=pl.BlockSpec((s, LANES), lambda b, h: (b, h)),
        out_shape=jax.ShapeDtypeStruct((bsz * s, n_heads * HEAD_DIM), BF16),
        scratch_shapes=[pltpu.VMEM((n_blocks, blk, LANES), BF16),
                        pltpu.VMEM((n_blocks, blk, LANES), BF16),
                        pltpu.VMEM((n_blocks, 2, MOBA_VT_ROWS, blk), BF16),
                        pltpu.VMEM((n_blocks, LANES), F32),
                        pltpu.VMEM((2, blk, blk), F32),
                        pltpu.VMEM((LANES, s), F32),
                        pltpu.VMEM((n_tiles, 2, LANES, tq), BF16),
                        pltpu.VMEM((n_tiles, 2, MOBA_VT_ROWS, tq), F32),
                        logits, logits, probs, probs],
        compiler_params=_params(2),
        name="moba",
    )(slopes, jnp.asarray(sched), qkv, qkv, qkv)


def _memory_attention(q, mem_k, mem_v):
    q = q * (HEAD_DIM ** -0.5)
    lane = _iota(q.shape, 1)
    out = jnp.zeros(q.shape, F32)
    for h in range(MEM_HEADS):
        in_head = (lane >= h * HEAD_DIM) & (lane < (h + 1) * HEAD_DIM)
        qh = jnp.where(in_head, q, 0.0).astype(BF16)
        s = lax.dot_general(qh, mem_k, (((1,), (1,)), ((), ())), preferred_element_type=F32)
        m = jnp.max(s, axis=-1, keepdims=True)
        p = jnp.exp(s - m)
        l = jnp.sum(p, axis=-1, keepdims=True)
        pv = jnp.dot(p.astype(BF16), mem_v, preferred_element_type=F32)
        out = jnp.where(in_head, pv / l, out)
    return out


def _project_out(tok, mem_out, wo_ref, x):
    tw = tok.shape[1]
    return (x + jnp.dot(tok.astype(BF16), wo_ref[0:tw, :], preferred_element_type=F32)
            + jnp.dot(mem_out.astype(BF16), wo_ref[tw:, :], preferred_element_type=F32))


def _s5_mix_out_kernel(y_ref, u_ref, d_ref, wglu_ref, bglu_ref, mq_ref, mk_ref, mv_ref, wo_ref, x_ref, o_ref):
    y = y_ref[...] + d_ref[...] * u_ref[...]
    y = 0.5 * y * (1.0 + lax.erf(y * math.sqrt(0.5)))
    z = jnp.dot(y.astype(BF16), wglu_ref[...], preferred_element_type=F32) + bglu_ref[...]
    tok = y * jax.nn.sigmoid(z)
    mem_out = _memory_attention(mq_ref[...], mk_ref[...], mv_ref[...])
    o_ref[...] = _project_out(tok, mem_out, wo_ref, x_ref[...])


def _moba_mix_out_kernel(tok_ref, mq_ref, mk_ref, mv_ref, wo_ref, x_ref, o_ref):
    mem_out = _memory_attention(mq_ref[...].astype(F32), mk_ref[...], mv_ref[...])
    o_ref[...] = _project_out(tok_ref[...], mem_out, wo_ref, x_ref[...])


def _mix_out_specs(n, s, d, tok_w, mem_len, mq_block_col):
    tm = min(2 * ROW_TILE, s)
    per_b = s // tm
    row = lambda i: (i, 0)
    const = lambda i: (0, 0)
    mem_specs = [pl.BlockSpec((tm, MEM_WIDTH), lambda i: (i, mq_block_col)),
                 pl.BlockSpec((mem_len, MEM_WIDTH), lambda i: (i // per_b, 0)),
                 pl.BlockSpec((mem_len, MEM_WIDTH), lambda i: (i // per_b, 1)),
                 pl.BlockSpec((d, d), const),
                 pl.BlockSpec((tm, d), row)]
    return tm, row, const, mem_specs


def _s5_mix_out(y_ssm, proj, d_skip, w_glu, b_glu, mem_kv, w_out, x2d, s, mem_len):
    n, d = x2d.shape
    tok_w = y_ssm.shape[1]
    tm, row, const, mem_specs = _mix_out_specs(n, s, d, tok_w, mem_len, tok_w // MEM_WIDTH)
    return pl.pallas_call(
        _s5_mix_out_kernel,
        grid=(n // tm,),
        in_specs=[pl.BlockSpec((tm, tok_w), row),
                  pl.BlockSpec((tm, tok_w), row),
                  pl.BlockSpec((1, tok_w), const),
                  pl.BlockSpec((tok_w, tok_w), const),
                  pl.BlockSpec((1, tok_w), const)] + mem_specs,
        out_specs=pl.BlockSpec((tm, d), row),
        out_shape=jax.ShapeDtypeStruct((n, d), F32),
        compiler_params=_params(1),
        name="s5_mix_out",
    )(y_ssm, proj, d_skip.reshape(1, tok_w), w_glu, b_glu.reshape(1, tok_w), proj, mem_kv, mem_kv, w_out, x2d)


def _moba_mix_out(tok, proj, mem_kv, w_out, x2d, s, mem_len):
    n, d = x2d.shape
    tok_w = tok.shape[1]
    tm, row, const, mem_specs = _mix_out_specs(n, s, d, tok_w, mem_len, (proj.shape[1] - MEM_WIDTH) // MEM_WIDTH)
    return pl.pallas_call(
        _moba_mix_out_kernel,
        grid=(n // tm,),
        in_specs=[pl.BlockSpec((tm, tok_w), row)] + mem_specs,
        out_specs=pl.BlockSpec((tm, d), row),
        out_shape=jax.ShapeDtypeStruct((n, d), F32),
        compiler_params=_params(1),
        name="moba_mix_out",
    )(tok, proj, mem_kv, mem_kv, w_out, x2d)


FFN_CHUNK = MXU_TILE


def _ffn_kernel(x_ref, g_ref, wg_ref, wu_ref, wd_ref, gf_ref, o_ref, act_ref, *, final_norm):
    x = x_ref[...]
    h = _rms_scale(x, g_ref[...]).astype(BF16)
    hidden = wg_ref.shape[1]
    for c in range(hidden // FFN_CHUNK):
        cols = slice(c * FFN_CHUNK, (c + 1) * FFN_CHUNK)
        gate = jnp.dot(h, wg_ref[:, cols], preferred_element_type=F32)
        up = jnp.dot(h, wu_ref[:, cols], preferred_element_type=F32)
        act_ref[:, cols] = (gate * jax.nn.sigmoid(gate) * up).astype(BF16)
    y = x + jnp.dot(act_ref[...], wd_ref[...], preferred_element_type=F32)
    if final_norm:
        y = _rms_scale(y, gf_ref[...])
    o_ref[...] = y


def _ffn(x2d, g, w_gate, w_up, w_down, g_final, final_norm):
    n, d = x2d.shape
    hidden = w_gate.shape[1]
    tm = min(2 * ROW_TILE, n)
    row = lambda i: (i, 0)
    const = lambda i: (0, 0)
    resident = pl.Buffered(1)
    return pl.pallas_call(
        functools.partial(_ffn_kernel, final_norm=final_norm),
        grid=(n // tm,),
        in_specs=[pl.BlockSpec((tm, d), row),
                  pl.BlockSpec((1, d), const),
                  pl.BlockSpec((d, hidden), const, pipeline_mode=resident),
                  pl.BlockSpec((d, hidden), const, pipeline_mode=resident),
                  pl.BlockSpec((hidden, d), const, pipeline_mode=resident),
                  pl.BlockSpec((1, d), const)],
        out_specs=pl.BlockSpec((tm, d), row),
        out_shape=jax.ShapeDtypeStruct((n, d), F32),
        scratch_shapes=[pltpu.VMEM((tm, hidden), BF16)],
        compiler_params=_params(1),
        name="ffn",
    )(x2d, g.reshape(1, d), w_gate, w_up, w_down, g_final.reshape(1, d))


def kernel(x, mem, mem_norm_g, w_mem_kv, mix_norm_g, s5_w_in, s5_lambda_re, s5_lambda_im, s5_log_dt,
           s5_b_re, s5_b_im, s5_c_re, s5_c_im, s5_d, s5_w_glu, s5_b_glu, moba_w_in, w_out, ffn_norm_g,
           w_gate, w_up, w_down, final_norm_g):
    bsz, s, d = x.shape
    mem_len = mem.shape[1]
    depth = mix_norm_g.shape[0]
    tok_w = d - MEM_WIDTH
    n_heads = tok_w // HEAD_DIM
    slopes = jnp.asarray(2.0 ** (-8.0 * (np.arange(n_heads) + 1) / n_heads), dtype=F32)

    mem_kv = _norm_matmul(mem.reshape(bsz * mem_len, d), mem_norm_g, w_mem_kv.astype(BF16), out_dtype=BF16)
    x2d = x.reshape(bsz * s, d)
    for i in range(depth):
        j = i // 2
        if i % 2 == 0:
            proj = _norm_matmul(x2d, mix_norm_g[i], s5_w_in[j].astype(BF16), row_tile=2 * ROW_TILE)
            ops = _s5_operators(s5_lambda_re[j], s5_lambda_im[j], s5_log_dt[j],
                                s5_b_re[j], s5_b_im[j], s5_c_re[j], s5_c_im[j])
            y_ssm = _s5_ssm(proj, ops, bsz, s, tok_w)
            x2d = _s5_mix_out(y_ssm, proj, s5_d[j], s5_w_glu[j].astype(BF16), s5_b_glu[j],
                              mem_kv, w_out[i].astype(BF16), x2d, s, mem_len)
        else:
            proj = _norm_matmul(x2d, mix_norm_g[i], moba_w_in[j].astype(BF16), out_dtype=BF16,
                                row_tile=2 * ROW_TILE)
            tok = _moba(proj, slopes, bsz, s, n_heads)
            x2d = _moba_mix_out(tok, proj, mem_kv, w_out[i].astype(BF16), x2d, s, mem_len)
        x2d = _ffn(x2d, ffn_norm_g[i], w_gate[i].astype(BF16), w_up[i].astype(BF16),
                   w_down[i].astype(BF16), final_norm_g, final_norm=(i == depth - 1))
    return x2d.reshape(bsz, s, d)
```

```python
import functools
import math

import numpy as np
import jax
import jax.numpy as jnp
from jax import lax
from jax.experimental import pallas as pl
from jax.experimental.pallas import tpu as pltpu

F32 = jnp.float32
BF16 = jnp.bfloat16

HEAD_DIM = 64
MEM_HEADS = 4
MEM_WIDTH = MEM_HEADS * HEAD_DIM
S5_GROUP_CH = 16
S5_STATE = 64
S5_CHUNK = 8
MOBA_BLOCK = 256
MOBA_TOPK = 3
RMS_EPS = 1e-6
NEG_INF = -1e30

LANES = 128
MXU_TILE = 256
VMEM_LIMIT_BYTES = 56 * 1024 * 1024
ROW_TILE = 512

S5_LANE_GROUPS = LANES // S5_GROUP_CH
S5_STEPS_PER_TILE = MXU_TILE // LANES
S5_LAG_TILES = S5_CHUNK // S5_STEPS_PER_TILE
S5_SEQS_PER_STEP = 2


def _params(n_axes):
    return pltpu.CompilerParams(dimension_semantics=("arbitrary",) * n_axes,
                                vmem_limit_bytes=VMEM_LIMIT_BYTES)


def _iota(shape, dim):
    return lax.broadcasted_iota(jnp.int32, shape, dim)


def _rms_scale(x, g):
    ms = jnp.mean(x * x, axis=-1, keepdims=True)
    return x * lax.rsqrt(ms + RMS_EPS) * g


def _norm_matmul_kernel(x_ref, g_ref, w_ref, o_ref):
    h = _rms_scale(x_ref[...], g_ref[...])
    o_ref[...] = jnp.dot(h.astype(BF16), w_ref[...], preferred_element_type=F32).astype(o_ref.dtype)


def _norm_matmul(x2d, g, w_bf16, out_dtype=F32, row_tile=ROW_TILE):
    n, d = x2d.shape
    f = w_bf16.shape[1]
    tm = min(row_tile, n)
    return pl.pallas_call(
        _norm_matmul_kernel,
        grid=(n // tm,),
        in_specs=[pl.BlockSpec((tm, d), lambda i: (i, 0)),
                  pl.BlockSpec((1, d), lambda i: (0, 0)),
                  pl.BlockSpec((d, f), lambda i: (0, 0))],
        out_specs=pl.BlockSpec((tm, f), lambda i: (i, 0)),
        out_shape=jax.ShapeDtypeStruct((n, f), out_dtype),
        compiler_params=_params(1),
        name="norm_matmul",
    )(x2d, g.reshape(1, d), w_bf16)


def _s5_operators(lam_re, lam_im, log_dt, b_re, b_im, c_re, c_im):
    hi = lax.Precision.HIGHEST
    t_len = S5_CHUNK
    g_cnt, p_cnt = lam_re.shape
    c_cnt = b_re.shape[-1]
    gpb = S5_LANE_GROUPS
    n_lb = g_cnt // gpb
    spt = S5_STEPS_PER_TILE
    lr = lam_re.astype(F32)
    li = lam_im.astype(F32)
    dt = jnp.exp(log_dt.astype(F32))[:, None]
    mag = jnp.exp(lr * dt)
    ar = mag * jnp.cos(li * dt)
    ai = mag * jnp.sin(li * dt)
    den = lr * lr + li * li
    nr = ar - 1.0
    fr = (nr * lr + ai * li) / den
    fi = (ai * lr - nr * li) / den
    br = jnp.swapaxes(b_re.astype(F32), 1, 2)
    bi = jnp.swapaxes(b_im.astype(F32), 1, 2)
    bbr = fr[:, None, :] * br - fi[:, None, :] * bi
    bbi = fr[:, None, :] * bi + fi[:, None, :] * br
    cr = c_re.astype(F32)
    ci = c_im.astype(F32)

    k = jnp.arange(t_len + 1, dtype=F32)[:, None, None]
    pm = jnp.exp(k * (lr * dt)[None])
    pr = pm * jnp.cos(k * (li * dt)[None])
    pi = pm * jnp.sin(k * (li * dt)[None])

    wr = pr[:, :, None, :] * bbr[None] - pi[:, :, None, :] * bbi[None]
    wi = pr[:, :, None, :] * bbi[None] + pi[:, :, None, :] * bbr[None]
    kern = (jnp.einsum('gop,kgcp->kgoc', cr, wr[:t_len], precision=hi)
            - jnp.einsum('gop,kgcp->kgoc', ci, wi[:t_len], precision=hi))
    lag = (jnp.arange(S5_LAG_TILES)[:, None, None] * spt
           + jnp.arange(spt)[None, None, :] - jnp.arange(spt)[None, :, None])
    kd = jnp.where((lag >= 0)[..., None, None, None], kern[jnp.clip(lag, 0, t_len - 1)], 0.0)
    kd = kd.reshape(S5_LAG_TILES, spt, spt, n_lb, gpb, c_cnt, c_cnt)
    lag_t = kd.transpose(3, 0, 2, 5, 1, 4, 6).reshape(n_lb, S5_LAG_TILES, spt * c_cnt, spt * gpb * c_cnt)

    def by_lane_block(x):
        return x.reshape(t_len, n_lb, gpb, c_cnt, 2 * p_cnt).transpose(1, 0, 2, 3, 4).reshape(
            n_lb, t_len * gpb * c_cnt, 2 * p_cnt)

    p_c = by_lane_block(jnp.concatenate([wr[:t_len][::-1], wi[:t_len][::-1]], axis=-1))
    qr = cr[None] * pr[1:, :, None, :] - ci[None] * pi[1:, :, None, :]
    qi = -(cr[None] * pi[1:, :, None, :] + ci[None] * pr[1:, :, None, :])
    q_c = by_lane_block(jnp.concatenate([qr, qi], axis=-1))

    coef = jnp.stack([pr[t_len].reshape(n_lb, gpb * p_cnt),
                      pi[t_len].reshape(n_lb, gpb * p_cnt)], axis=1)
    return lag_t.astype(BF16), p_c.astype(BF16), q_c.astype(BF16), coef


def _s5_ssm_kernel(u_ref, lag_ref, pc_ref, qc_ref, coef_ref, y_ref,
                   toep_scr, p_scr, q_scr, h_scr, s_scr, xp_scr, *, n_chunks, n_seqs):
    t_len = S5_CHUNK
    seq_len = n_chunks * t_len
    gpb = S5_LANE_GROUPS
    st = S5_STATE
    cc = S5_GROUP_CH
    half = gpb * st
    flat_w = t_len * LANES
    nt_dims = (((1,), (1,)), ((), ()))

    @pl.when(pl.program_id(1) == 0)
    def _expand_operators():
        def group_diag(x, row_unit, col_unit):
            same = ((_iota(x.shape, 0) // row_unit) % gpb) == ((_iota(x.shape, 1) // col_unit) % gpb)
            return jnp.where(same, x, 0.0).astype(BF16)

        shape = (2 * st, 2 * half)
        rep = ((_iota(shape, 0) // st == _iota(shape, 1) // half)
               & (_iota(shape, 0) % st == _iota(shape, 1) % st)).astype(BF16)
        for r in range(flat_w // MXU_TILE):
            rows = slice(r * MXU_TILE, (r + 1) * MXU_TILE)
            p_scr[rows, :] = group_diag(jnp.dot(pc_ref[0, rows, :], rep, preferred_element_type=F32), cc, st)
        shape = (2 * half, 2 * st)
        rep = ((_iota(shape, 0) // half == _iota(shape, 1) // st)
               & (_iota(shape, 0) % st == _iota(shape, 1) % st)).astype(BF16)
        for c in range(flat_w // MXU_TILE):
            cols = slice(c * MXU_TILE, (c + 1) * MXU_TILE)
            q_scr[:, cols] = group_diag(lax.dot_general(rep, qc_ref[0, cols, :], nt_dims,
                                                        preferred_element_type=F32), st, cc)
        shape = (MXU_TILE, S5_STEPS_PER_TILE * cc)
        rep = ((_iota(shape, 0) // LANES == _iota(shape, 1) // cc)
               & (_iota(shape, 0) % cc == _iota(shape, 1) % cc)).astype(BF16)
        for l in range(S5_LAG_TILES):
            toep_scr[l] = group_diag(jnp.dot(rep, lag_ref[0, l], preferred_element_type=F32), cc, cc)

    for j in range(t_len):
        for b in range(n_seqs):
            h_scr[b * n_chunks:(b + 1) * n_chunks, j * LANES:(j + 1) * LANES] = (
                u_ref[pl.ds(b * seq_len + j, n_chunks, stride=t_len), :].astype(BF16))
    s_scr[...] = jnp.dot(h_scr[...], p_scr[...], preferred_element_type=F32)
    cr = coef_ref[0, 0:1, :]
    ci = coef_ref[0, 1:2, :]

    xr = [jnp.zeros((1, half), F32)] * n_seqs
    xi = [jnp.zeros((1, half), F32)] * n_seqs
    for n in range(n_chunks):
        for b in range(n_seqs):
            row = slice(b * n_chunks + n, b * n_chunks + n + 1)
            xp_scr[row, 0:half] = xr[b]
            xp_scr[row, half:2 * half] = xi[b]
            sr = s_scr[row, 0:half]
            si = s_scr[row, half:2 * half]
            xr[b], xi[b] = cr * xr[b] - ci * xi[b] + sr, cr * xi[b] + ci * xr[b] + si
    xp = xp_scr[...].astype(BF16)
    for nt in range(S5_LAG_TILES):
        cols = slice(nt * MXU_TILE, (nt + 1) * MXU_TILE)
        acc = jnp.dot(xp, q_scr[:, cols], preferred_element_type=F32)
        for kt in range(nt + 1):
            acc = acc + lax.dot_general(h_scr[:, kt * MXU_TILE:(kt + 1) * MXU_TILE], toep_scr[nt - kt],
                                        nt_dims, preferred_element_type=F32)
        for tt in range(S5_STEPS_PER_TILE):
            t = nt * S5_STEPS_PER_TILE + tt
            for b in range(n_seqs):
                y_ref[pl.ds(b * seq_len + t, n_chunks, stride=t_len), :] = (
                    acc[b * n_chunks:(b + 1) * n_chunks, tt * LANES:(tt + 1) * LANES])


def _s5_ssm(proj, ops, bsz, s, tok_w):
    lag_t, p_c, q_c, coef = ops
    n_lb = tok_w // LANES
    n_chunks = s // S5_CHUNK
    flat_w = p_c.shape[1]
    state_w = S5_LANE_GROUPS * p_c.shape[2]
    const3 = lambda l, b: (l, 0, 0)
    n_seqs = S5_SEQS_PER_STEP if bsz % S5_SEQS_PER_STEP == 0 else 1
    rows = n_seqs * n_chunks
    return pl.pallas_call(
        functools.partial(_s5_ssm_kernel, n_chunks=n_chunks, n_seqs=n_seqs),
        grid=(n_lb, bsz // n_seqs),
        in_specs=[pl.BlockSpec((n_seqs * s, LANES), lambda l, b: (b, l)),
                  pl.BlockSpec((1,) + lag_t.shape[1:], lambda l, b: (l, 0, 0, 0)),
                  pl.BlockSpec((1,) + p_c.shape[1:], const3),
                  pl.BlockSpec((1,) + q_c.shape[1:], const3),
                  pl.BlockSpec((1,) + coef.shape[1:], const3)],
        out_specs=pl.BlockSpec((n_seqs * s, LANES), lambda l, b: (b, l)),
        out_shape=jax.ShapeDtypeStruct((bsz * s, tok_w), F32),
        scratch_shapes=[pltpu.VMEM((S5_LAG_TILES, MXU_TILE, MXU_TILE), BF16),
                        pltpu.VMEM((flat_w, state_w), BF16),
                        pltpu.VMEM((state_w, flat_w), BF16),
                        pltpu.VMEM((rows, flat_w), BF16),
                        pltpu.VMEM((rows, state_w), F32),
                        pltpu.VMEM((rows, state_w), F32)],
        compiler_params=_params(2),
        name="s5_ssm",
    )(proj, lag_t, p_c, q_c, coef)


MOBA_Q_BLOCKS = 2
MOBA_VT_ROWS = HEAD_DIM + 16
LOG2E = math.log2(math.e)
MOBA_AUG_HI = 8
MOBA_ITEMS_PER_STEP = 18


def _moba_schedule(n_tiles):
    rows = []
    for ti in range(n_tiles):
        for n, c in enumerate([ti] + list(range(ti))):
            rows.append((ti, c, int(n == 0)))
    assert len(rows) % 2 == 0, "the pipelined loop retires an even number of work items per step"
    rows += [(rows[-1][0], rows[-1][1], 0)] * 2
    return np.asarray(rows, np.int32).T.copy()


def _moba_kernel(slope_ref, sched_ref, q_ref, k_ref, v_ref, o_ref,
                 ka_ref, kb_ref, vt_ref, km_ref, tri_ref, qt_ref, qa_ref, acc_ref,
                 s0_ref, s1_ref, p0_ref, p1_ref, *, n_blocks, n_items):
    blk = MOBA_BLOCK
    nq = MOBA_Q_BLOCKS
    tq = nq * blk
    n_tiles = n_blocks // nq
    seq = n_blocks * blk
    hp = pl.program_id(1)
    slopes = (slope_ref[2 * hp] * LOG2E, slope_ref[2 * hp + 1] * LOG2E)
    k_refs = (ka_ref, kb_ref)

    lane = _iota((blk, LANES), 1)
    key_off = _iota((blk, LANES), 0).astype(F32)
    ones_row = (_iota((MOBA_VT_ROWS - HEAD_DIM, blk), 0) == 0).astype(BF16)
    in_heads = (lane < HEAD_DIM, lane >= HEAD_DIM)
    aug_lane = (lane - HEAD_DIM, lane)
    key_bias = []
    for e in range(2):
        a = slopes[e] * key_off
        a_hi = a.astype(BF16).astype(F32)
        key_bias.append(jnp.where(aug_lane[e] == 0, a_hi, jnp.where(aug_lane[e] == 1, a - a_hi, 0.0)))
    for j in range(n_blocks):
        kj = k_ref[j * blk:(j + 1) * blk, :].astype(F32)
        for e in range(2):
            one_hot = (aug_lane[e] == MOBA_AUG_HI + j) | (aug_lane[e] == MOBA_AUG_HI + n_blocks + j)
            aug = jnp.where(one_hot, 1.0, key_bias[e])
            k_refs[e][j] = jnp.where(in_heads[e], kj, aug).astype(BF16)
        v_t = v_ref[j * blk:(j + 1) * blk, :].astype(F32).T
        for e in range(2):
            vt_ref[j, e, 0:HEAD_DIM, :] = v_t[e * HEAD_DIM:(e + 1) * HEAD_DIM, :].astype(BF16)
            vt_ref[j, e, HEAD_DIM:MOBA_VT_ROWS, :] = ones_row
        km_ref[j:j + 1, :] = jnp.mean(kj, axis=0, keepdims=True)
    d = _iota((blk, blk), 1) - _iota((blk, blk), 0)
    tri_ref[0] = jnp.zeros((blk, blk), F32)
    tri_ref[1] = jnp.where(d >= 0, 0.0, NEG_INF)

    for ti in range(n_tiles):
        cols = slice(ti * tq, (ti + 1) * tq)
        q_tile = q_ref[cols, :].astype(F32) * (HEAD_DIM ** -0.5 * LOG2E)
        qt_ref[:, cols] = q_tile.T
    q_bf = qt_ref[...].astype(BF16)
    km = km_ref[...]
    lane_m = _iota(km.shape, 1)
    sub = 8
    groups = range(0, n_blocks, sub)
    q_off = _iota((sub, blk), 1)
    row_ids = [_iota((sub, blk), 0) + g0 for g0 in groups]
    hidden_rows = jnp.full((sub, blk), NEG_INF, F32)
    ones_rows = (_iota((MOBA_AUG_HI, tq), 0) < 2).astype(F32)
    pad_rows = jnp.zeros((HEAD_DIM - MOBA_AUG_HI - 2 * n_blocks, tq), F32)
    for e in range(2):
        in_head = (lane_m < HEAD_DIM) if e == 0 else (lane_m >= HEAD_DIM)
        km_e = jnp.where(in_head, km, 0.0).astype(BF16)
        gate = jnp.dot(km_e, q_bf, preferred_element_type=F32)
        bias_hi = [[] for _ in groups]
        bias_lo = [[] for _ in groups]
        for qb in range(n_blocks):
            cols = slice(qb * blk, (qb + 1) * blk)
            for gi, g0 in enumerate(groups):
                if g0 > qb:
                    bias_hi[gi].append(hidden_rows)
                    bias_lo[gi].append(jnp.zeros((sub, blk), F32))
                    continue
                rows = gate[g0:g0 + sub, cols]
                row_id = row_ids[gi]
                rank = jnp.zeros((sub, blk), jnp.int32)
                for jp in range(qb):
                    gj = gate[jp:jp + 1, cols]
                    if jp < g0:
                        beats = gj >= rows
                    elif jp >= g0 + sub:
                        beats = gj > rows
                    else:
                        beats = (gj > rows) | ((gj == rows) & (row_id > jp))
                    rank = rank + beats.astype(jnp.int32)
                visible = ((row_id < qb) & (rank < MOBA_TOPK)) | (row_id == qb)
                distance = ((qb - row_id) * blk + q_off).astype(F32)
                bias = jnp.where(visible, -slopes[e] * distance, NEG_INF)
                hi = bias.astype(BF16).astype(F32)
                bias_hi[gi].append(hi)
                bias_lo[gi].append(bias - hi)
        for ti in range(n_tiles):
            cols = slice(ti * tq, (ti + 1) * tq)
            pieces = [jnp.concatenate(per_group[nq * ti:nq * (ti + 1)], axis=1) for per_group in bias_hi + bias_lo]
            aug = jnp.concatenate([ones_rows] + pieces + [pad_rows], axis=0)
            q_head = qt_ref[e * HEAD_DIM:(e + 1) * HEAD_DIM, cols]
            parts = [q_head, aug] if e == 0 else [aug, q_head]
            qa_ref[ti, e] = jnp.concatenate(parts, axis=0).astype(BF16)

    def score(w, s_ref):
        ti = sched_ref[0, w]
        c = sched_ref[1, w]
        on_diag = sched_ref[2, w]
        maxes = []
        for e in range(2):
            q_aug = qa_ref[ti, e]
            m_blk = None
            for slab in range(nq):
                s_t = jnp.dot(k_refs[e][nq * c + slab], q_aug, preferred_element_type=F32)
                parts = [s_t[:, h * blk:(h + 1) * blk] + tri_ref[on_diag] if h == slab
                         else s_t[:, h * blk:(h + 1) * blk] for h in range(nq)]
                s_t = jnp.concatenate(parts, axis=1)
                s_ref[e, slab] = s_t
                mx = jnp.max(s_t, axis=0, keepdims=True)
                m_blk = mx if m_blk is None else jnp.maximum(m_blk, mx)
            maxes.append(m_blk)
        return tuple(maxes)

    def exponentiate(s_ref, p_ref, m):
        for e in range(2):
            for slab in range(nq):
                p_ref[e, slab] = jnp.exp2(s_ref[e, slab] - m[e]).astype(BF16)

    def accumulate(w, p_ref, alpha):
        ti = sched_ref[0, w]
        c = sched_ref[1, w]
        for e in range(2):
            pv = None
            for slab in range(nq):
                dd = jnp.dot(vt_ref[nq * c + slab, e], p_ref[e, slab], preferred_element_type=F32)
                pv = dd if pv is None else pv + dd
            acc_ref[ti, e] = alpha[e] * acc_ref[ti, e] + pv

    def retire(w, carry, s_next, p_next, s_free, p_cur):
        m_prev, m_cur, m_blk_next = carry
        starts_tile = sched_ref[2, w] == 1
        next_starts_tile = sched_ref[2, w + 1] == 1
        m_next = tuple(jnp.where(next_starts_tile, m_blk_next[e], jnp.maximum(m_cur[e], m_blk_next[e]))
                       for e in range(2))
        m_blk_new = score(w + 2, s_free)
        exponentiate(s_next, p_next, m_next)
        alpha = tuple(jnp.exp2(jnp.where(starts_tile, NEG_INF, m_prev[e]) - m_cur[e]) for e in range(2))
        accumulate(w, p_cur, alpha)
        return m_cur, m_next, m_blk_new

    acc_ref[...] = jnp.zeros_like(acc_ref)
    m_first = score(0, s0_ref)
    exponentiate(s0_ref, p0_ref, m_first)
    carry = (m_first, m_first, score(1, s1_ref))

    per_step = MOBA_ITEMS_PER_STEP if n_items % MOBA_ITEMS_PER_STEP == 0 else 2

    def body(t, carry):
        for i in range(per_step):
            bufs = (s1_ref, p1_ref, s0_ref, p0_ref) if i % 2 == 0 else (s0_ref, p0_ref, s1_ref, p1_ref)
            carry = retire(per_step * t + i, carry, *bufs)
        return carry

    lax.fori_loop(0, n_items // per_step, body, carry)

    for ti in range(n_tiles):
        out_t = jnp.concatenate([acc_ref[ti, e, 0:HEAD_DIM, :] / acc_ref[ti, e, HEAD_DIM:HEAD_DIM + 1, :]
                                 for e in range(2)], axis=0)
        o_ref[ti * tq:(ti + 1) * tq, :] = out_t.T.astype(o_ref.dtype)


def _moba(qkv, slopes, bsz, s, n_heads):
    n_blocks = s // MOBA_BLOCK
    n_pairs = n_heads // 2
    blk = MOBA_BLOCK
    tq = MOBA_Q_BLOCKS * blk
    n_tiles = s // tq
    sched = _moba_schedule(n_tiles)
    n_items = sched.shape[1] - 2
    logits = pltpu.VMEM((2, MOBA_Q_BLOCKS, blk, tq), F32)
    probs = pltpu.VMEM((2, MOBA_Q_BLOCKS, blk, tq), BF16)
    return pl.pallas_call(
        functools.partial(_moba_kernel, n_blocks=n_blocks, n_items=n_items),
        grid=(bsz, n_pairs),
        in_specs=[pl.BlockSpec(memory_space=pltpu.SMEM),
                  pl.BlockSpec(memory_space=pltpu.SMEM),
                  pl.BlockSpec((s, LANES), lambda b, h: (b, h)),
                  pl.BlockSpec((s, LANES), lambda b, h: (b, n_pairs + h)),
                  pl.BlockSpec((s, LANES), lambda b, h: (b, 2 * n_pairs + h))],
        out_specs=pl.BlockSpec((s, LANES), lambda b, h: (b, h)),
        out_shape=jax.ShapeDtypeStruct((bsz * s, n_heads * HEAD_DIM), BF16),
        scratch_shapes=[pltpu.VMEM((n_blocks, blk, LANES), BF16),
                        pltpu.VMEM((n_blocks, blk, LANES), BF16),
                        pltpu.VMEM((n_blocks, 2, MOBA_VT_ROWS, blk), BF16),
                        pltpu.VMEM((n_blocks, LANES), F32),
                        pltpu.VMEM((2, blk, blk), F32),
                        pltpu.VMEM((LANES, s), F32),
                        pltpu.VMEM((n_tiles, 2, LANES, tq), BF16),
                        pltpu.VMEM((n_tiles, 2, MOBA_VT_ROWS, tq), F32),
                        logits, logits, probs, probs],
        compiler_params=_params(2),
        name="moba",
    )(slopes, jnp.asarray(sched), qkv, qkv, qkv)


def _memory_attention(q, mem_k, mem_v):
    q = q * (HEAD_DIM ** -0.5)
    lane = _iota(q.shape, 1)
    out = jnp.zeros(q.shape, F32)
    for h in range(MEM_HEADS):
        in_head = (lane >= h * HEAD_DIM) & (lane < (h + 1) * HEAD_DIM)
        qh = jnp.where(in_head, q, 0.0).astype(BF16)
        s = lax.dot_general(qh, mem_k, (((1,), (1,)), ((), ())), preferred_element_type=F32)
        m = jnp.max(s, axis=-1, keepdims=True)
        p = jnp.exp(s - m)
        l = jnp.sum(p, axis=-1, keepdims=True)
        pv = jnp.dot(p.astype(BF16), mem_v, preferred_element_type=F32)
        out = jnp.where(in_head, pv / l, out)
    return out


def _project_out(tok, mem_out, wo_ref, x):
    tw = tok.shape[1]
    return (x + jnp.dot(tok.astype(BF16), wo_ref[0:tw, :], preferred_element_type=F32)
            + jnp.dot(mem_out.astype(BF16), wo_ref[tw:, :], preferred_element_type=F32))


def _s5_mix_out_kernel(y_ref, u_ref, d_ref, wglu_ref, bglu_ref, mq_ref, mk_ref, mv_ref, wo_ref, x_ref, o_ref):
    y = y_ref[...] + d_ref[...] * u_ref[...]
    y = 0.5 * y * (1.0 + lax.erf(y * math.sqrt(0.5)))
    z = jnp.dot(y.astype(BF16), wglu_ref[...], preferred_element_type=F32) + bglu_ref[...]
    tok = y * jax.nn.sigmoid(z)
    mem_out = _memory_attention(mq_ref[...], mk_ref[...], mv_ref[...])
    o_ref[...] = _project_out(tok, mem_out, wo_ref, x_ref[...])


def _moba_mix_out_kernel(tok_ref, mq_ref, mk_ref, mv_ref, wo_ref, x_ref, o_ref):
    mem_out = _memory_attention(mq_ref[...].astype(F32), mk_ref[...], mv_ref[...])
    o_ref[...] = _project_out(tok_ref[...], mem_out, wo_ref, x_ref[...])


def _mix_out_specs(n, s, d, tok_w, mem_len, mq_block_col):
    tm = min(2 * ROW_TILE, s)
    per_b = s // tm
    row = lambda i: (i, 0)
    const = lambda i: (0, 0)
    mem_specs = [pl.BlockSpec((tm, MEM_WIDTH), lambda i: (i, mq_block_col)),
                 pl.BlockSpec((mem_len, MEM_WIDTH), lambda i: (i // per_b, 0)),
                 pl.BlockSpec((mem_len, MEM_WIDTH), lambda i: (i // per_b, 1)),
                 pl.BlockSpec((d, d), const),
                 pl.BlockSpec((tm, d), row)]
    return tm, row, const, mem_specs


def _s5_mix_out(y_ssm, proj, d_skip, w_glu, b_glu, mem_kv, w_out, x2d, s, mem_len):
    n, d = x2d.shape
    tok_w = y_ssm.shape[1]
    tm, row, const, mem_specs = _mix_out_specs(n, s, d, tok_w, mem_len, tok_w // MEM_WIDTH)
    return pl.pallas_call(
        _s5_mix_out_kernel,
        grid=(n // tm,),
        in_specs=[pl.BlockSpec((tm, tok_w), row),
                  pl.BlockSpec((tm, tok_w), row),
                  pl.BlockSpec((1, tok_w), const),
                  pl.BlockSpec((tok_w, tok_w), const),
                  pl.BlockSpec((1, tok_w), const)] + mem_specs,
        out_specs=pl.BlockSpec((tm, d), row),
        out_shape=jax.ShapeDtypeStruct((n, d), F32),
        compiler_params=_params(1),
        name="s5_mix_out",
    )(y_ssm, proj, d_skip.reshape(1, tok_w), w_glu, b_glu.reshape(1, tok_w), proj, mem_kv, mem_kv, w_out, x2d)


def _moba_mix_out(tok, proj, mem_kv, w_out, x2d, s, mem_len):
    n, d = x2d.shape
    tok_w = tok.shape[1]
    tm, row, const, mem_specs = _mix_out_specs(n, s, d, tok_w, mem_len, (proj.shape[1] - MEM_WIDTH) // MEM_WIDTH)
    return pl.pallas_call(
        _moba_mix_out_kernel,
        grid=(n // tm,),
        in_specs=[pl.BlockSpec((tm, tok_w), row)] + mem_specs,
        out_specs=pl.BlockSpec((tm, d), row),
        out_shape=jax.ShapeDtypeStruct((n, d), F32),
        compiler_params=_params(1),
        name="moba_mix_out",
    )(tok, proj, mem_kv, mem_kv, w_out, x2d)


FFN_CHUNK = MXU_TILE


def _ffn_kernel(x_ref, g_ref, wg_ref, wu_ref, wd_ref, gf_ref, o_ref, act_ref, *, final_norm):
    x = x_ref[...]
    h = _rms_scale(x, g_ref[...]).astype(BF16)
    hidden = wg_ref.shape[1]
    for c in range(hidden // FFN_CHUNK):
        cols = slice(c * FFN_CHUNK, (c + 1) * FFN_CHUNK)
        gate = jnp.dot(h, wg_ref[:, cols], preferred_element_type=F32)
        up = jnp.dot(h, wu_ref[:, cols], preferred_element_type=F32)
        act_ref[:, cols] = (gate * jax.nn.sigmoid(gate) * up).astype(BF16)
    y = x + jnp.dot(act_ref[...], wd_ref[...], preferred_element_type=F32)
    if final_norm:
        y = _rms_scale(y, gf_ref[...])
    o_ref[...] = y


def _ffn(x2d, g, w_gate, w_up, w_down, g_final, final_norm):
    n, d = x2d.shape
    hidden = w_gate.shape[1]
    tm = min(2 * ROW_TILE, n)
    row = lambda i: (i, 0)
    const = lambda i: (0, 0)
    resident = pl.Buffered(1)
    return pl.pallas_call(
        functools.partial(_ffn_kernel, final_norm=final_norm),
        grid=(n // tm,),
        in_specs=[pl.BlockSpec((tm, d), row),
                  pl.BlockSpec((1, d), const),
                  pl.BlockSpec((d, hidden), const, pipeline_mode=resident),
                  pl.BlockSpec((d, hidden), const, pipeline_mode=resident),
                  pl.BlockSpec((hidden, d), const, pipeline_mode=resident),
                  pl.BlockSpec((1, d), const)],
        out_specs=pl.BlockSpec((tm, d), row),
        out_shape=jax.ShapeDtypeStruct((n, d), F32),
        scratch_shapes=[pltpu.VMEM((tm, hidden), BF16)],
        compiler_params=_params(1),
        name="ffn",
    )(x2d, g.reshape(1, d), w_gate, w_up, w_down, g_final.reshape(1, d))


def kernel(x, mem, mem_norm_g, w_mem_kv, mix_norm_g, s5_w_in, s5_lambda_re, s5_lambda_im, s5_log_dt,
           s5_b_re, s5_b_im, s5_c_re, s5_c_im, s5_d, s5_w_glu, s5_b_glu, moba_w_in, w_out, ffn_norm_g,
           w_gate, w_up, w_down, final_norm_g):
    bsz, s, d = x.shape
    mem_len = mem.shape[1]
    depth = mix_norm_g.shape[0]
    tok_w = d - MEM_WIDTH
    n_heads = tok_w // HEAD_DIM
    slopes = jnp.asarray(2.0 ** (-8.0 * (np.arange(n_heads) + 1) / n_heads), dtype=F32)

    mem_kv = _norm_matmul(mem.reshape(bsz * mem_len, d), mem_norm_g, w_mem_kv.astype(BF16), out_dtype=BF16)
    x2d = x.reshape(bsz * s, d)
    for i in range(depth):
        j = i // 2
        if i % 2 == 0:
            proj = _norm_matmul(x2d, mix_norm_g[i], s5_w_in[j].astype(BF16), row_tile=2 * ROW_TILE)
            ops = _s5_operators(s5_lambda_re[j], s5_lambda_im[j], s5_log_dt[j],
                                s5_b_re[j], s5_b_im[j], s5_c_re[j], s5_c_im[j])
            y_ssm = _s5_ssm(proj, ops, bsz, s, tok_w)
            x2d = _s5_mix_out(y_ssm, proj, s5_d[j], s5_w_glu[j].astype(BF16), s5_b_glu[j],
                              mem_kv, w_out[i].astype(BF16), x2d, s, mem_len)
        else:
            proj = _norm_matmul(x2d, mix_norm_g[i], moba_w_in[j].astype(BF16), out_dtype=BF16,
                                row_tile=2 * ROW_TILE)
            tok = _moba(proj, slopes, bsz, s, n_heads)
            x2d = _moba_mix_out(tok, proj, mem_kv, w_out[i].astype(BF16), x2d, s, mem_len)
        x2d = _ffn(x2d, ffn_norm_g[i], w_gate[i].astype(BF16), w_up[i].astype(BF16),
                   w_down[i].astype(BF16), final_norm_g, final_norm=(i == depth - 1))
    return x2d.reshape(bsz, s, d)
```

```python
import functools
import math

import numpy as np
import jax
import jax.numpy as jnp
from jax import lax
from jax.experimental import pallas as pl
from jax.experimental.pallas import tpu as pltpu

F32 = jnp.float32
BF16 = jnp.bfloat16

HEAD_DIM = 64
MEM_HEADS = 4
MEM_WIDTH = MEM_HEADS * HEAD_DIM
S5_GROUP_CH = 16
S5_STATE = 64
S5_CHUNK = 8
MOBA_BLOCK = 256
MOBA_TOPK = 3
RMS_EPS = 1e-6
NEG_INF = -1e30

LANES = 128
MXU_TILE = 256
VMEM_LIMIT_BYTES = 56 * 1024 * 1024
ROW_TILE = 512

S5_LANE_GROUPS = LANES // S5_GROUP_CH
S5_STEPS_PER_TILE = MXU_TILE // LANES
S5_LAG_TILES = S5_CHUNK // S5_STEPS_PER_TILE
S5_SEQS_PER_STEP = 2


def _params(n_axes):
    return pltpu.CompilerParams(dimension_semantics=("arbitrary",) * n_axes,
                                vmem_limit_bytes=VMEM_LIMIT_BYTES)


def _iota(shape, dim):
    return lax.broadcasted_iota(jnp.int32, shape, dim)


def _rms_scale(x, g):
    ms = jnp.mean(x * x, axis=-1, keepdims=True)
    return x * lax.rsqrt(ms + RMS_EPS) * g


def _norm_matmul_kernel(x_ref, g_ref, w_ref, o_ref):
    h = _rms_scale(x_ref[...], g_ref[...])
    o_ref[...] = jnp.dot(h.astype(BF16), w_ref[...], preferred_element_type=F32).astype(o_ref.dtype)


def _norm_matmul(x2d, g, w_bf16, out_dtype=F32, row_tile=ROW_TILE):
    n, d = x2d.shape
    f = w_bf16.shape[1]
    tm = min(row_tile, n)
    return pl.pallas_call(
        _norm_matmul_kernel,
        grid=(n // tm,),
        in_specs=[pl.BlockSpec((tm, d), lambda i: (i, 0)),
                  pl.BlockSpec((1, d), lambda i: (0, 0)),
                  pl.BlockSpec((d, f), lambda i: (0, 0))],
        out_specs=pl.BlockSpec((tm, f), lambda i: (i, 0)),
        out_shape=jax.ShapeDtypeStruct((n, f), out_dtype),
        compiler_params=_params(1),
        name="norm_matmul",
    )(x2d, g.reshape(1, d), w_bf16)


def _s5_operators(lam_re, lam_im, log_dt, b_re, b_im, c_re, c_im):
    hi = lax.Precision.HIGHEST
    t_len = S5_CHUNK
    g_cnt, p_cnt = lam_re.shape
    c_cnt = b_re.shape[-1]
    gpb = S5_LANE_GROUPS
    n_lb = g_cnt // gpb
    spt = S5_STEPS_PER_TILE
    lr = lam_re.astype(F32)
    li = lam_im.astype(F32)
    dt = jnp.exp(log_dt.astype(F32))[:, None]
    mag = jnp.exp(lr * dt)
    ar = mag * jnp.cos(li * dt)
    ai = mag * jnp.sin(li * dt)
    den = lr * lr + li * li
    nr = ar - 1.0
    fr = (nr * lr + ai * li) / den
    fi = (ai * lr - nr * li) / den
    br = jnp.swapaxes(b_re.astype(F32), 1, 2)
    bi = jnp.swapaxes(b_im.astype(F32), 1, 2)
    bbr = fr[:, None, :] * br - fi[:, None, :] * bi
    bbi = fr[:, None, :] * bi + fi[:, None, :] * br
    cr = c_re.astype(F32)
    ci = c_im.astype(F32)

    k = jnp.arange(t_len + 1, dtype=F32)[:, None, None]
    pm = jnp.exp(k * (lr * dt)[None])
    pr = pm * jnp.cos(k * (li * dt)[None])
    pi = pm * jnp.sin(k * (li * dt)[None])

    wr = pr[:, :, None, :] * bbr[None] - pi[:, :, None, :] * bbi[None]
    wi = pr[:, :, None, :] * bbi[None] + pi[:, :, None, :] * bbr[None]
    kern = (jnp.einsum('gop,kgcp->kgoc', cr, wr[:t_len], precision=hi)
            - jnp.einsum('gop,kgcp->kgoc', ci, wi[:t_len], precision=hi))
    lag = (jnp.arange(S5_LAG_TILES)[:, None, None] * spt
           + jnp.arange(spt)[None, None, :] - jnp.arange(spt)[None, :, None])
    kd = jnp.where((lag >= 0)[..., None, None, None], kern[jnp.clip(lag, 0, t_len - 1)], 0.0)
    kd = kd.reshape(S5_LAG_TILES, spt, spt, n_lb, gpb, c_cnt, c_cnt)
    lag_t = kd.transpose(3, 0, 2, 5, 1, 4, 6).reshape(n_lb, S5_LAG_TILES, spt * c_cnt, spt * gpb * c_cnt)

    def by_lane_block(x):
        return x.reshape(t_len, n_lb, gpb, c_cnt, 2 * p_cnt).transpose(1, 0, 2, 3, 4).reshape(
            n_lb, t_len * gpb * c_cnt, 2 * p_cnt)

    p_c = by_lane_block(jnp.concatenate([wr[:t_len][::-1], wi[:t_len][::-1]], axis=-1))
    qr = cr[None] * pr[1:, :, None, :] - ci[None] * pi[1:, :, None, :]
    qi = -(cr[None] * pi[1:, :, None, :] + ci[None] * pr[1:, :, None, :])
    q_c = by_lane_block(jnp.concatenate([qr, qi], axis=-1))

    coef = jnp.stack([pr[t_len].reshape(n_lb, gpb * p_cnt),
                      pi[t_len].reshape(n_lb, gpb * p_cnt)], axis=1)
    return lag_t.astype(BF16), p_c.astype(BF16), q_c.astype(BF16), coef


def _s5_ssm_kernel(u_ref, lag_ref, pc_ref, qc_ref, coef_ref, y_ref,
                   toep_scr, p_scr, q_scr, h_scr, s_scr, xp_scr, *, n_chunks, n_seqs):
    t_len = S5_CHUNK
    seq_len = n_chunks * t_len
    gpb = S5_LANE_GROUPS
    st = S5_STATE
    cc = S5_GROUP_CH
    half = gpb * st
    flat_w = t_len * LANES
    nt_dims = (((1,), (1,)), ((), ()))

    @pl.when(pl.program_id(1) == 0)
    def _expand_operators():
        def group_diag(x, row_unit, col_unit):
            same = ((_iota(x.shape, 0) // row_unit) % gpb) == ((_iota(x.shape, 1) // col_unit) % gpb)
            return jnp.where(same, x, 0.0).astype(BF16)

        shape = (2 * st, 2 * half)
        rep = ((_iota(shape, 0) // st == _iota(shape, 1) // half)
               & (_iota(shape, 0) % st == _iota(shape, 1) % st)).astype(BF16)
        for r in range(flat_w // MXU_TILE):
            rows = slice(r * MXU_TILE, (r + 1) * MXU_TILE)
            p_scr[rows, :] = group_diag(jnp.dot(pc_ref[0, rows, :], rep, preferred_element_type=F32), cc, st)
        shape = (2 * half, 2 * st)
        rep = ((_iota(shape, 0) // half == _iota(shape, 1) // st)
               & (_iota(shape, 0) % st == _iota(shape, 1) % st)).astype(BF16)
        for c in range(flat_w // MXU_TILE):
            cols = slice(c * MXU_TILE, (c + 1) * MXU_TILE)
            q_scr[:, cols] = group_diag(lax.dot_general(rep, qc_ref[0, cols, :], nt_dims,
                                                        preferred_element_type=F32), st, cc)
        shape = (MXU_TILE, S5_STEPS_PER_TILE * cc)
        rep = ((_iota(shape, 0) // LANES == _iota(shape, 1) // cc)
               & (_iota(shape, 0) % cc == _iota(shape, 1) % cc)).astype(BF16)
        for l in range(S5_LAG_TILES):
            toep_scr[l] = group_diag(jnp.dot(rep, lag_ref[0, l], preferred_element_type=F32), cc, cc)

    for j in range(t_len):
        for b in range(n_seqs):
            h_scr[b * n_chunks:(b + 1) * n_chunks, j * LANES:(j + 1) * LANES] = (
                u_ref[pl.ds(b * seq_len + j, n_chunks, stride=t_len), :].astype(BF16))
    s_scr[...] = jnp.dot(h_scr[...], p_scr[...], preferred_element_type=F32)
    cr = coef_ref[0, 0:1, :]
    ci = coef_ref[0, 1:2, :]

    xr = [jnp.zeros((1, half), F32)] * n_seqs
    xi = [jnp.zeros((1, half), F32)] * n_seqs
    for n in range(n_chunks):
        for b in range(n_seqs):
            row = slice(b * n_chunks + n, b * n_chunks + n + 1)
            xp_scr[row, 0:half] = xr[b]
            xp_scr[row, half:2 * half] = xi[b]
            sr = s_scr[row, 0:half]
            si = s_scr[row, half:2 * half]
            xr[b], xi[b] = cr * xr[b] - ci * xi[b] + sr, cr * xi[b] + ci * xr[b] + si
    xp = xp_scr[...].astype(BF16)
    for nt in range(S5_LAG_TILES):
        cols = slice(nt * MXU_TILE, (nt + 1) * MXU_TILE)
        acc = jnp.dot(xp, q_scr[:, cols], preferred_element_type=F32)
        for kt in range(nt + 1):
            acc = acc + lax.dot_general(h_scr[:, kt * MXU_TILE:(kt + 1) * MXU_TILE], toep_scr[nt - kt],
                                        nt_dims, preferred_element_type=F32)
        for tt in range(S5_STEPS_PER_TILE):
            t = nt * S5_STEPS_PER_TILE + tt
            for b in range(n_seqs):
                y_ref[pl.ds(b * seq_len + t, n_chunks, stride=t_len), :] = (
                    acc[b * n_chunks:(b + 1) * n_chunks, tt * LANES:(tt + 1) * LANES])


def _s5_ssm(proj, ops, bsz, s, tok_w):
    lag_t, p_c, q_c, coef = ops
    n_lb = tok_w // LANES
    n_chunks = s // S5_CHUNK
    flat_w = p_c.shape[1]
    state_w = S5_LANE_GROUPS * p_c.shape[2]
    const3 = lambda l, b: (l, 0, 0)
    n_seqs = S5_SEQS_PER_STEP if bsz % S5_SEQS_PER_STEP == 0 else 1
    rows = n_seqs * n_chunks
    return pl.pallas_call(
        functools.partial(_s5_ssm_kernel, n_chunks=n_chunks, n_seqs=n_seqs),
        grid=(n_lb, bsz // n_seqs),
        in_specs=[pl.BlockSpec((n_seqs * s, LANES), lambda l, b: (b, l)),
                  pl.BlockSpec((1,) + lag_t.shape[1:], lambda l, b: (l, 0, 0, 0)),
                  pl.BlockSpec((1,) + p_c.shape[1:], const3),
                  pl.BlockSpec((1,) + q_c.shape[1:], const3),
                  pl.BlockSpec((1,) + coef.shape[1:], const3)],
        out_specs=pl.BlockSpec((n_seqs * s, LANES), lambda l, b: (b, l)),
        out_shape=jax.ShapeDtypeStruct((bsz * s, tok_w), F32),
        scratch_shapes=[pltpu.VMEM((S5_LAG_TILES, MXU_TILE, MXU_TILE), BF16),
                        pltpu.VMEM((flat_w, state_w), BF16),
                        pltpu.VMEM((state_w, flat_w), BF16),
                        pltpu.VMEM((rows, flat_w), BF16),
                        pltpu.VMEM((rows, state_w), F32),
                        pltpu.VMEM((rows, state_w), F32)],
        compiler_params=_params(2),
        name="s5_ssm",
    )(proj, lag_t, p_c, q_c, coef)


MOBA_Q_BLOCKS = 2
MOBA_VT_ROWS = HEAD_DIM + 16
LOG2E = math.log2(math.e)
MOBA_AUG_HI = 8
MOBA_ITEMS_PER_STEP = 18


def _moba_schedule(n_tiles):
    rows = []
    for ti in range(n_tiles):
        for n, c in enumerate([ti] + list(range(ti))):
            rows.append((ti, c, int(n == 0)))
    assert len(rows) % 2 == 0, "the pipelined loop retires an even number of work items per step"
    rows += [(rows[-1][0], rows[-1][1], 0)] * 2
    return np.asarray(rows, np.int32).T.copy()


def _moba_kernel(slope_ref, sched_ref, q_ref, k_ref, v_ref, o_ref,
                 ka_ref, kb_ref, vt_ref, km_ref, tri_ref, qt_ref, qa_ref, acc_ref,
                 s0_ref, s1_ref, p0_ref, p1_ref, *, n_blocks, n_items):
    blk = MOBA_BLOCK
    nq = MOBA_Q_BLOCKS
    tq = nq * blk
    n_tiles = n_blocks // nq
    seq = n_blocks * blk
    hp = pl.program_id(1)
    slopes = (slope_ref[2 * hp] * LOG2E, slope_ref[2 * hp + 1] * LOG2E)
    k_refs = (ka_ref, kb_ref)

    lane = _iota((blk, LANES), 1)
    key_off = _iota((blk, LANES), 0).astype(F32)
    ones_row = (_iota((MOBA_VT_ROWS - HEAD_DIM, blk), 0) == 0).astype(BF16)
    in_heads = (lane < HEAD_DIM, lane >= HEAD_DIM)
    aug_lane = (lane - HEAD_DIM, lane)
    key_bias = []
    for e in range(2):
        a = slopes[e] * key_off
        a_hi = a.astype(BF16).astype(F32)
        key_bias.append(jnp.where(aug_lane[e] == 0, a_hi, jnp.where(aug_lane[e] == 1, a - a_hi, 0.0)))
    for j in range(n_blocks):
        kj = k_ref[j * blk:(j + 1) * blk, :].astype(F32)
        for e in range(2):
            one_hot = (aug_lane[e] == MOBA_AUG_HI + j) | (aug_lane[e] == MOBA_AUG_HI + n_blocks + j)
            aug = jnp.where(one_hot, 1.0, key_bias[e])
            k_refs[e][j] = jnp.where(in_heads[e], kj, aug).astype(BF16)
        v_t = v_ref[j * blk:(j + 1) * blk, :].astype(F32).T
        for e in range(2):
            vt_ref[j, e, 0:HEAD_DIM, :] = v_t[e * HEAD_DIM:(e + 1) * HEAD_DIM, :].astype(BF16)
            vt_ref[j, e, HEAD_DIM:MOBA_VT_ROWS, :] = ones_row
        km_ref[j:j + 1, :] = jnp.mean(kj, axis=0, keepdims=True)
    d = _iota((blk, blk), 1) - _iota((blk, blk), 0)
    tri_ref[0] = jnp.zeros((blk, blk), F32)
    tri_ref[1] = jnp.where(d >= 0, 0.0, NEG_INF)

    for ti in range(n_tiles):
        cols = slice(ti * tq, (ti + 1) * tq)
        q_tile = q_ref[cols, :].astype(F32) * (HEAD_DIM ** -0.5 * LOG2E)
        qt_ref[:, cols] = q_tile.T
    q_bf = qt_ref[...].astype(BF16)
    km = km_ref[...]
    lane_m = _iota(km.shape, 1)
    sub = 8
    groups = range(0, n_blocks, sub)
    q_off = _iota((sub, blk), 1)
    row_ids = [_iota((sub, blk), 0) + g0 for g0 in groups]
    hidden_rows = jnp.full((sub, blk), NEG_INF, F32)
    ones_rows = (_iota((MOBA_AUG_HI, tq), 0) < 2).astype(F32)
    pad_rows = jnp.zeros((HEAD_DIM - MOBA_AUG_HI - 2 * n_blocks, tq), F32)
    for e in range(2):
        in_head = (lane_m < HEAD_DIM) if e == 0 else (lane_m >= HEAD_DIM)
        km_e = jnp.where(in_head, km, 0.0).astype(BF16)
        gate = jnp.dot(km_e, q_bf, preferred_element_type=F32)
        bias_hi = [[] for _ in groups]
        bias_lo = [[] for _ in groups]
        for qb in range(n_blocks):
            cols = slice(qb * blk, (qb + 1) * blk)
            for gi, g0 in enumerate(groups):
                if g0 > qb:
                    bias_hi[gi].append(hidden_rows)
                    bias_lo[gi].append(jnp.zeros((sub, blk), F32))
                    continue
                rows = gate[g0:g0 + sub, cols]
                row_id = row_ids[gi]
                rank = jnp.zeros((sub, blk), jnp.int32)
                for jp in range(qb):
                    gj = gate[jp:jp + 1, cols]
                    if jp < g0:
                        beats = gj >= rows
                    elif jp >= g0 + sub:
                        beats = gj > rows
                    else:
                        beats = (gj > rows) | ((gj == rows) & (row_id > jp))
                    rank = rank + beats.astype(jnp.int32)
                visible = ((row_id < qb) & (rank < MOBA_TOPK)) | (row_id == qb)
                distance = ((qb - row_id) * blk + q_off).astype(F32)
                bias = jnp.where(visible, -slopes[e] * distance, NEG_INF)
                hi = bias.astype(BF16).astype(F32)
                bias_hi[gi].append(hi)
                bias_lo[gi].append(bias - hi)
        for ti in range(n_tiles):
            cols = slice(ti * tq, (ti + 1) * tq)
            pieces = [jnp.concatenate(per_group[nq * ti:nq * (ti + 1)], axis=1) for per_group in bias_hi + bias_lo]
            aug = jnp.concatenate([ones_rows] + pieces + [pad_rows], axis=0)
            q_head = qt_ref[e * HEAD_DIM:(e + 1) * HEAD_DIM, cols]
            parts = [q_head, aug] if e == 0 else [aug, q_head]
            qa_ref[ti, e] = jnp.concatenate(parts, axis=0).astype(BF16)

    def score(w, s_ref):
        ti = sched_ref[0, w]
        c = sched_ref[1, w]
        on_diag = sched_ref[2, w]
        maxes = []
        for e in range(2):
            q_aug = qa_ref[ti, e]
            m_blk = None
            for slab in range(nq):
                s_t = jnp.dot(k_refs[e][nq * c + slab], q_aug, preferred_element_type=F32)
                parts = [s_t[:, h * blk:(h + 1) * blk] + tri_ref[on_diag] if h == slab
                         else s_t[:, h * blk:(h + 1) * blk] for h in range(nq)]
                s_t = jnp.concatenate(parts, axis=1)
                s_ref[e, slab] = s_t
                mx = jnp.max(s_t, axis=0, keepdims=True)
                m_blk = mx if m_blk is None else jnp.maximum(m_blk, mx)
            maxes.append(m_blk)
        return tuple(maxes)

    def exponentiate(s_ref, p_ref, m):
        for e in range(2):
            for slab in range(nq):
                p_ref[e, slab] = jnp.exp2(s_ref[e, slab] - m[e]).astype(BF16)

    def accumulate(w, p_ref, alpha):
        ti = sched_ref[0, w]
        c = sched_ref[1, w]
        for e in range(2):
            pv = None
            for slab in range(nq):
                dd = jnp.dot(vt_ref[nq * c + slab, e], p_ref[e, slab], preferred_element_type=F32)
                pv = dd if pv is None else pv + dd
            acc_ref[ti, e] = alpha[e] * acc_ref[ti, e] + pv

    def retire(w, carry, s_next, p_next, s_free, p_cur):
        m_prev, m_cur, m_blk_next = carry
        starts_tile = sched_ref[2, w] == 1
        next_starts_tile = sched_ref[2, w + 1] == 1
        m_next = tuple(jnp.where(next_starts_tile, m_blk_next[e], jnp.maximum(m_cur[e], m_blk_next[e]))
                       for e in range(2))
        m_blk_new = score(w + 2, s_free)
        exponentiate(s_next, p_next, m_next)
        alpha = tuple(jnp.exp2(jnp.where(starts_tile, NEG_INF, m_prev[e]) - m_cur[e]) for e in range(2))
        accumulate(w, p_cur, alpha)
        return m_cur, m_next, m_blk_new

    acc_ref[...] = jnp.zeros_like(acc_ref)
    m_first = score(0, s0_ref)
    exponentiate(s0_ref, p0_ref, m_first)
    carry = (m_first, m_first, score(1, s1_ref))

    per_step = MOBA_ITEMS_PER_STEP if n_items % MOBA_ITEMS_PER_STEP == 0 else 2

    def body(t, carry):
        for i in range(per_step):
            bufs = (s1_ref, p1_ref, s0_ref, p0_ref) if i % 2 == 0 else (s0_ref, p0_ref, s1_ref, p1_ref)
            carry = retire(per_step * t + i, carry, *bufs)
        return carry

    lax.fori_loop(0, n_items // per_step, body, carry)

    for ti in range(n_tiles):
        out_t = jnp.concatenate([acc_ref[ti, e, 0:HEAD_DIM, :] / acc_ref[ti, e, HEAD_DIM:HEAD_DIM + 1, :]
                                 for e in range(2)], axis=0)
        o_ref[ti * tq:(ti + 1) * tq, :] = out_t.T.astype(o_ref.dtype)


def _moba(qkv, slopes, bsz, s, n_heads):
    n_blocks = s // MOBA_BLOCK
    n_pairs = n_heads // 2
    blk = MOBA_BLOCK
    tq = MOBA_Q_BLOCKS * blk
    n_tiles = s // tq
    sched = _moba_schedule(n_tiles)
    n_items = sched.shape[1] - 2
    logits = pltpu.VMEM((2, MOBA_Q_BLOCKS, blk, tq), F32)
    probs = pltpu.VMEM((2, MOBA_Q_BLOCKS, blk, tq), BF16)
    return pl.pallas_call(
        functools.partial(_moba_kernel, n_blocks=n_blocks, n_items=n_items),
        grid=(bsz, n_pairs),
        in_specs=[pl.BlockSpec(memory_space=pltpu.SMEM),
                  pl.BlockSpec(memory_space=pltpu.SMEM),
                  pl.BlockSpec((s, LANES), lambda b, h: (b, h)),
                  pl.BlockSpec((s, LANES), lambda b, h: (b, n_pairs + h)),
                  pl.BlockSpec((s, LANES), lambda b, h: (b, 2 * n_pairs + h))],
        out_specs=pl.BlockSpec((s, LANES), lambda b, h: (b, h)),
        out_shape=jax.ShapeDtypeStruct((bsz * s, n_heads * HEAD_DIM), BF16),
        scratch_shapes=[pltpu.VMEM((n_blocks, blk, LANES), BF16),
                        pltpu.VMEM((n_blocks, blk, LANES), BF16),
                        pltpu.VMEM((n_blocks, 2, MOBA_VT_ROWS, blk), BF16),
                        pltpu.VMEM((n_blocks, LANES), F32),
                        pltpu.VMEM((2, blk, blk), F32),
                        pltpu.VMEM((LANES, s), F32),
                        pltpu.VMEM((n_tiles, 2, LANES, tq), BF16),
                        pltpu.VMEM((n_tiles, 2, MOBA_VT_ROWS, tq), F32),
                        logits, logits, probs, probs],
        compiler_params=_params(2),
        name="moba",
    )(slopes, jnp.asarray(sched), qkv, qkv, qkv)


def _memory_attention(q, mem_k, mem_v):
    q = q * (HEAD_DIM ** -0.5)
    lane = _iota(q.shape, 1)
    out = jnp.zeros(q.shape, F32)
    for h in range(MEM_HEADS):
        in_head = (lane >= h * HEAD_DIM) & (lane < (h + 1) * HEAD_DIM)
        qh = jnp.where(in_head, q, 0.0).astype(BF16)
        s = lax.dot_general(qh, mem_k, (((1,), (1,)), ((), ())), preferred_element_type=F32)
        m = jnp.max(s, axis=-1, keepdims=True)
        p = jnp.exp(s - m)
        l = jnp.sum(p, axis=-1, keepdims=True)
        pv = jnp.dot(p.astype(BF16), mem_v, preferred_element_type=F32)
        out = jnp.where(in_head, pv / l, out)
    return out


def _project_out(tok, mem_out, wo_ref, x):
    tw = tok.shape[1]
    return (x + jnp.dot(tok.astype(BF16), wo_ref[0:tw, :], preferred_element_type=F32)
            + jnp.dot(mem_out.astype(BF16), wo_ref[tw:, :], preferred_element_type=F32))


def _s5_mix_out_kernel(y_ref, u_ref, d_ref, wglu_ref, bglu_ref, mq_ref, mk_ref, mv_ref, wo_ref, x_ref, o_ref):
    y = y_ref[...] + d_ref[...] * u_ref[...]
    y = 0.5 * y * (1.0 + lax.erf(y * math.sqrt(0.5)))
    z = jnp.dot(y.astype(BF16), wglu_ref[...], preferred_element_type=F32) + bglu_ref[...]
    tok = y * jax.nn.sigmoid(z)
    mem_out = _memory_attention(mq_ref[...], mk_ref[...], mv_ref[...])
    o_ref[...] = _project_out(tok, mem_out, wo_ref, x_ref[...])


def _moba_mix_out_kernel(tok_ref, mq_ref, mk_ref, mv_ref, wo_ref, x_ref, o_ref):
    mem_out = _memory_attention(mq_ref[...].astype(F32), mk_ref[...], mv_ref[...])
    o_ref[...] = _project_out(tok_ref[...], mem_out, wo_ref, x_ref[...])


def _mix_out_specs(n, s, d, tok_w, mem_len, mq_block_col):
    tm = min(2 * ROW_TILE, s)
    per_b = s // tm
    row = lambda i: (i, 0)
    const = lambda i: (0, 0)
    mem_specs = [pl.BlockSpec((tm, MEM_WIDTH), lambda i: (i, mq_block_col)),
                 pl.BlockSpec((mem_len, MEM_WIDTH), lambda i: (i // per_b, 0)),
                 pl.BlockSpec((mem_len, MEM_WIDTH), lambda i: (i // per_b, 1)),
                 pl.BlockSpec((d, d), const),
                 pl.BlockSpec((tm, d), row)]
    return tm, row, const, mem_specs


def _s5_mix_out(y_ssm, proj, d_skip, w_glu, b_glu, mem_kv, w_out, x2d, s, mem_len):
    n, d = x2d.shape
    tok_w = y_ssm.shape[1]
    tm, row, const, mem_specs = _mix_out_specs(n, s, d, tok_w, mem_len, tok_w // MEM_WIDTH)
    return pl.pallas_call(
        _s5_mix_out_kernel,
        grid=(n // tm,),
        in_specs=[pl.BlockSpec((tm, tok_w), row),
                  pl.BlockSpec((tm, tok_w), row),
                  pl.BlockSpec((1, tok_w), const),
                  pl.BlockSpec((tok_w, tok_w), const),
                  pl.BlockSpec((1, tok_w), const)] + mem_specs,
        out_specs=pl.BlockSpec((tm, d), row),
        out_shape=jax.ShapeDtypeStruct((n, d), F32),
        compiler_params=_params(1),
        name="s5_mix_out",
    )(y_ssm, proj, d_skip.reshape(1, tok_w), w_glu, b_glu.reshape(1, tok_w), proj, mem_kv, mem_kv, w_out, x2d)


def _moba_mix_out(tok, proj, mem_kv, w_out, x2d, s, mem_len):
    n, d = x2d.shape
    tok_w = tok.shape[1]
    tm, row, const, mem_specs = _mix_out_specs(n, s, d, tok_w, mem_len, (proj.shape[1] - MEM_WIDTH) // MEM_WIDTH)
    return pl.pallas_call(
        _moba_mix_out_kernel,
        grid=(n // tm,),
        in_specs=[pl.BlockSpec((tm, tok_w), row)] + mem_specs,
        out_specs=pl.BlockSpec((tm, d), row),
        out_shape=jax.ShapeDtypeStruct((n, d), F32),
        compiler_params=_params(1),
        name="moba_mix_out",
    )(tok, proj, mem_kv, mem_kv, w_out, x2d)


FFN_CHUNK = MXU_TILE


def _ffn_kernel(x_ref, g_ref, wg_ref, wu_ref, wd_ref, gf_ref, o_ref, act_ref, *, final_norm):
    x = x_ref[...]
    h = _rms_scale(x, g_ref[...]).astype(BF16)
    hidden = wg_ref.shape[1]
    for c in range(hidden // FFN_CHUNK):
        cols = slice(c * FFN_CHUNK, (c + 1) * FFN_CHUNK)
        gate = jnp.dot(h, wg_ref[:, cols], preferred_element_type=F32)
        up = jnp.dot(h, wu_ref[:, cols], preferred_element_type=F32)
        act_ref[:, cols] = (gate * jax.nn.sigmoid(gate) * up).astype(BF16)
    y = x + jnp.dot(act_ref[...], wd_ref[...], preferred_element_type=F32)
    if final_norm:
        y = _rms_scale(y, gf_ref[...])
    o_ref[...] = y


def _ffn(x2d, g, w_gate, w_up, w_down, g_final, final_norm):
    n, d = x2d.shape
    hidden = w_gate.shape[1]
    tm = min(2 * ROW_TILE, n)
    row = lambda i: (i, 0)
    const = lambda i: (0, 0)
    resident = pl.Buffered(1)
    return pl.pallas_call(
        functools.partial(_ffn_kernel, final_norm=final_norm),
        grid=(n // tm,),
        in_specs=[pl.BlockSpec((tm, d), row),
                  pl.BlockSpec((1, d), const),
                  pl.BlockSpec((d, hidden), const, pipeline_mode=resident),
                  pl.BlockSpec((d, hidden), const, pipeline_mode=resident),
                  pl.BlockSpec((hidden, d), const, pipeline_mode=resident),
                  pl.BlockSpec((1, d), const)],
        out_specs=pl.BlockSpec((tm, d), row),
        out_shape=jax.ShapeDtypeStruct((n, d), F32),
        scratch_shapes=[pltpu.VMEM((tm, hidden), BF16)],
        compiler_params=_params(1),
        name="ffn",
    )(x2d, g.reshape(1, d), w_gate, w_up, w_down, g_final.reshape(1, d))


def kernel(x, mem, mem_norm_g, w_mem_kv, mix_norm_g, s5_w_in, s5_lambda_re, s5_lambda_im, s5_log_dt,
           s5_b_re, s5_b_im, s5_c_re, s5_c_im, s5_d, s5_w_glu, s5_b_glu, moba_w_in, w_out, ffn_norm_g,
           w_gate, w_up, w_down, final_norm_g):
    bsz, s, d = x.shape
    mem_len = mem.shape[1]
    depth = mix_norm_g.shape[0]
    tok_w = d - MEM_WIDTH
    n_heads = tok_w // HEAD_DIM
    slopes = jnp.asarray(2.0 ** (-8.0 * (np.arange(n_heads) + 1) / n_heads), dtype=F32)

    mem_kv = _norm_matmul(mem.reshape(bsz * mem_len, d), mem_norm_g, w_mem_kv.astype(BF16), out_dtype=BF16)
    x2d = x.reshape(bsz * s, d)
    for i in range(depth):
        j = i // 2
        if i % 2 == 0:
            proj = _norm_matmul(x2d, mix_norm_g[i], s5_w_in[j].astype(BF16), row_tile=4 * ROW_TILE)
            ops = _s5_operators(s5_lambda_re[j], s5_lambda_im[j], s5_log_dt[j],
                                s5_b_re[j], s5_b_im[j], s5_c_re[j], s5_c_im[j])
            y_ssm = _s5_ssm(proj, ops, bsz, s, tok_w)
            x2d = _s5_mix_out(y_ssm, proj, s5_d[j], s5_w_glu[j].astype(BF16), s5_b_glu[j],
                              mem_kv, w_out[i].astype(BF16), x2d, s, mem_len)
        else:
            proj = _norm_matmul(x2d, mix_norm_g[i], moba_w_in[j].astype(BF16), out_dtype=BF16,
                                row_tile=2 * ROW_TILE)
            tok = _moba(proj, slopes, bsz, s, n_heads)
            x2d = _moba_mix_out(tok, proj, mem_kv, w_out[i].astype(BF16), x2d, s, mem_len)
        x2d = _ffn(x2d, ffn_norm_g[i], w_gate[i].astype(BF16), w_up[i].astype(BF16),
                   w_down[i].astype(BF16), final_norm_g, final_norm=(i == depth - 1))
    return x2d.reshape(bsz, s, d)
```

```python
import functools
import math

import numpy as np
import jax
import jax.numpy as jnp
from jax import lax
from jax.experimental import pallas as pl
from jax.experimental.pallas import tpu as pltpu

F32 = jnp.float32
BF16 = jnp.bfloat16

HEAD_DIM = 64
MEM_HEADS = 4
MEM_WIDTH = MEM_HEADS * HEAD_DIM
S5_GROUP_CH = 16
S5_STATE = 64
S5_CHUNK = 8
MOBA_BLOCK = 256
MOBA_TOPK = 3
RMS_EPS = 1e-6
NEG_INF = -1e30

LANES = 128
MXU_TILE = 256
VMEM_LIMIT_BYTES = 56 * 1024 * 1024
ROW_TILE = 512

S5_LANE_GROUPS = LANES // S5_GROUP_CH
S5_STEPS_PER_TILE = MXU_TILE // LANES
S5_LAG_TILES = S5_CHUNK // S5_STEPS_PER_TILE
S5_SEQS_PER_STEP = 2


def _params(n_axes):
    return pltpu.CompilerParams(dimension_semantics=("arbitrary",) * n_axes,
                                vmem_limit_bytes=VMEM_LIMIT_BYTES)


def _iota(shape, dim):
    return lax.broadcasted_iota(jnp.int32, shape, dim)


def _rms_scale(x, g):
    ms = jnp.mean(x * x, axis=-1, keepdims=True)
    return x * lax.rsqrt(ms + RMS_EPS) * g


def _norm_matmul_kernel(x_ref, g_ref, w_ref, o_ref):
    h = _rms_scale(x_ref[...], g_ref[...])
    o_ref[...] = jnp.dot(h.astype(BF16), w_ref[...], preferred_element_type=F32).astype(o_ref.dtype)


def _norm_matmul(x2d, g, w_bf16, out_dtype=F32, row_tile=ROW_TILE):
    n, d = x2d.shape
    f = w_bf16.shape[1]
    tm = min(row_tile, n)
    return pl.pallas_call(
        _norm_matmul_kernel,
        grid=(n // tm,),
        in_specs=[pl.BlockSpec((tm, d), lambda i: (i, 0)),
                  pl.BlockSpec((1, d), lambda i: (0, 0)),
                  pl.BlockSpec((d, f), lambda i: (0, 0))],
        out_specs=pl.BlockSpec((tm, f), lambda i: (i, 0)),
        out_shape=jax.ShapeDtypeStruct((n, f), out_dtype),
        compiler_params=_params(1),
        name="norm_matmul",
    )(x2d, g.reshape(1, d), w_bf16)


def _s5_operators(lam_re, lam_im, log_dt, b_re, b_im, c_re, c_im):
    hi = lax.Precision.HIGHEST
    t_len = S5_CHUNK
    g_cnt, p_cnt = lam_re.shape
    c_cnt = b_re.shape[-1]
    gpb = S5_LANE_GROUPS
    n_lb = g_cnt // gpb
    spt = S5_STEPS_PER_TILE
    lr = lam_re.astype(F32)
    li = lam_im.astype(F32)
    dt = jnp.exp(log_dt.astype(F32))[:, None]
    mag = jnp.exp(lr * dt)
    ar = mag * jnp.cos(li * dt)
    ai = mag * jnp.sin(li * dt)
    den = lr * lr + li * li
    nr = ar - 1.0
    fr = (nr * lr + ai * li) / den
    fi = (ai * lr - nr * li) / den
    br = jnp.swapaxes(b_re.astype(F32), 1, 2)
    bi = jnp.swapaxes(b_im.astype(F32), 1, 2)
    bbr = fr[:, None, :] * br - fi[:, None, :] * bi
    bbi = fr[:, None, :] * bi + fi[:, None, :] * br
    cr = c_re.astype(F32)
    ci = c_im.astype(F32)

    k = jnp.arange(t_len + 1, dtype=F32)[:, None, None]
    pm = jnp.exp(k * (lr * dt)[None])
    pr = pm * jnp.cos(k * (li * dt)[None])
    pi = pm * jnp.sin(k * (li * dt)[None])

    wr = pr[:, :, None, :] * bbr[None] - pi[:, :, None, :] * bbi[None]
    wi = pr[:, :, None, :] * bbi[None] + pi[:, :, None, :] * bbr[None]
    kern = (jnp.einsum('gop,kgcp->kgoc', cr, wr[:t_len], precision=hi)
            - jnp.einsum('gop,kgcp->kgoc', ci, wi[:t_len], precision=hi))
    lag = (jnp.arange(S5_LAG_TILES)[:, None, None] * spt
           + jnp.arange(spt)[None, None, :] - jnp.arange(spt)[None, :, None])
    kd = jnp.where((lag >= 0)[..., None, None, None], kern[jnp.clip(lag, 0, t_len - 1)], 0.0)
    kd = kd.reshape(S5_LAG_TILES, spt, spt, n_lb, gpb, c_cnt, c_cnt)
    lag_t = kd.transpose(3, 0, 2, 5, 1, 4, 6).reshape(n_lb, S5_LAG_TILES, spt * c_cnt, spt * gpb * c_cnt)

    def by_lane_block(x):
        return x.reshape(t_len, n_lb, gpb, c_cnt, 2 * p_cnt).transpose(1, 0, 2, 3, 4).reshape(
            n_lb, t_len * gpb * c_cnt, 2 * p_cnt)

    p_c = by_lane_block(jnp.concatenate([wr[:t_len][::-1], wi[:t_len][::-1]], axis=-1))
    qr = cr[None] * pr[1:, :, None, :] - ci[None] * pi[1:, :, None, :]
    qi = -(cr[None] * pi[1:, :, None, :] + ci[None] * pr[1:, :, None, :])
    q_c = by_lane_block(jnp.concatenate([qr, qi], axis=-1))

    coef = jnp.stack([pr[t_len].reshape(n_lb, gpb * p_cnt),
                      pi[t_len].reshape(n_lb, gpb * p_cnt)], axis=1)
    return lag_t.astype(BF16), p_c.astype(BF16), q_c.astype(BF16), coef


def _s5_ssm_kernel(u_ref, lag_ref, pc_ref, qc_ref, coef_ref, y_ref,
                   toep_scr, p_scr, q_scr, h_scr, s_scr, xp_scr, *, n_chunks, n_seqs):
    t_len = S5_CHUNK
    seq_len = n_chunks * t_len
    gpb = S5_LANE_GROUPS
    st = S5_STATE
    cc = S5_GROUP_CH
    half = gpb * st
    flat_w = t_len * LANES
    nt_dims = (((1,), (1,)), ((), ()))

    @pl.when(pl.program_id(1) == 0)
    def _expand_operators():
        def group_diag(x, row_unit, col_unit):
            same = ((_iota(x.shape, 0) // row_unit) % gpb) == ((_iota(x.shape, 1) // col_unit) % gpb)
            return jnp.where(same, x, 0.0).astype(BF16)

        shape = (2 * st, 2 * half)
        rep = ((_iota(shape, 0) // st == _iota(shape, 1) // half)
               & (_iota(shape, 0) % st == _iota(shape, 1) % st)).astype(BF16)
        for r in range(flat_w // MXU_TILE):
            rows = slice(r * MXU_TILE, (r + 1) * MXU_TILE)
            p_scr[rows, :] = group_diag(jnp.dot(pc_ref[0, rows, :], rep, preferred_element_type=F32), cc, st)
        shape = (2 * half, 2 * st)
        rep = ((_iota(shape, 0) // half == _iota(shape, 1) // st)
               & (_iota(shape, 0) % st == _iota(shape, 1) % st)).astype(BF16)
        for c in range(flat_w // MXU_TILE):
            cols = slice(c * MXU_TILE, (c + 1) * MXU_TILE)
            q_scr[:, cols] = group_diag(lax.dot_general(rep, qc_ref[0, cols, :], nt_dims,
                                                        preferred_element_type=F32), st, cc)
        shape = (MXU_TILE, S5_STEPS_PER_TILE * cc)
        rep = ((_iota(shape, 0) // LANES == _iota(shape, 1) // cc)
               & (_iota(shape, 0) % cc == _iota(shape, 1) % cc)).astype(BF16)
        for l in range(S5_LAG_TILES):
            toep_scr[l] = group_diag(jnp.dot(rep, lag_ref[0, l], preferred_element_type=F32), cc, cc)

    for j in range(t_len):
        for b in range(n_seqs):
            h_scr[b * n_chunks:(b + 1) * n_chunks, j * LANES:(j + 1) * LANES] = (
                u_ref[pl.ds(b * seq_len + j, n_chunks, stride=t_len), :].astype(BF16))
    s_scr[...] = jnp.dot(h_scr[...], p_scr[...], preferred_element_type=F32)
    cr = coef_ref[0, 0:1, :]
    ci = coef_ref[0, 1:2, :]

    xr = [jnp.zeros((1, half), F32)] * n_seqs
    xi = [jnp.zeros((1, half), F32)] * n_seqs
    for n in range(n_chunks):
        for b in range(n_seqs):
            row = slice(b * n_chunks + n, b * n_chunks + n + 1)
            xp_scr[row, 0:half] = xr[b]
            xp_scr[row, half:2 * half] = xi[b]
            sr = s_scr[row, 0:half]
            si = s_scr[row, half:2 * half]
            xr[b], xi[b] = cr * xr[b] - ci * xi[b] + sr, cr * xi[b] + ci * xr[b] + si
    xp = xp_scr[...].astype(BF16)
    for nt in range(S5_LAG_TILES):
        cols = slice(nt * MXU_TILE, (nt + 1) * MXU_TILE)
        acc = jnp.dot(xp, q_scr[:, cols], preferred_element_type=F32)
        for kt in range(nt + 1):
            acc = acc + lax.dot_general(h_scr[:, kt * MXU_TILE:(kt + 1) * MXU_TILE], toep_scr[nt - kt],
                                        nt_dims, preferred_element_type=F32)
        for tt in range(S5_STEPS_PER_TILE):
            t = nt * S5_STEPS_PER_TILE + tt
            for b in range(n_seqs):
                y_ref[pl.ds(b * seq_len + t, n_chunks, stride=t_len), :] = (
                    acc[b * n_chunks:(b + 1) * n_chunks, tt * LANES:(tt + 1) * LANES])


def _s5_ssm(proj, ops, bsz, s, tok_w):
    lag_t, p_c, q_c, coef = ops
    n_lb = tok_w // LANES
    n_chunks = s // S5_CHUNK
    flat_w = p_c.shape[1]
    state_w = S5_LANE_GROUPS * p_c.shape[2]
    const3 = lambda l, b: (l, 0, 0)
    n_seqs = S5_SEQS_PER_STEP if bsz % S5_SEQS_PER_STEP == 0 else 1
    rows = n_seqs * n_chunks
    return pl.pallas_call(
        functools.partial(_s5_ssm_kernel, n_chunks=n_chunks, n_seqs=n_seqs),
        grid=(n_lb, bsz // n_seqs),
        in_specs=[pl.BlockSpec((n_seqs * s, LANES), lambda l, b: (b, l)),
                  pl.BlockSpec((1,) + lag_t.shape[1:], lambda l, b: (l, 0, 0, 0)),
                  pl.BlockSpec((1,) + p_c.shape[1:], const3),
                  pl.BlockSpec((1,) + q_c.shape[1:], const3),
                  pl.BlockSpec((1,) + coef.shape[1:], const3)],
        out_specs=pl.BlockSpec((n_seqs * s, LANES), lambda l, b: (b, l)),
        out_shape=jax.ShapeDtypeStruct((bsz * s, tok_w), F32),
        scratch_shapes=[pltpu.VMEM((S5_LAG_TILES, MXU_TILE, MXU_TILE), BF16),
                        pltpu.VMEM((flat_w, state_w), BF16),
                        pltpu.VMEM((state_w, flat_w), BF16),
                        pltpu.VMEM((rows, flat_w), BF16),
                        pltpu.VMEM((rows, state_w), F32),
                        pltpu.VMEM((rows, state_w), F32)],
        compiler_params=_params(2),
        name="s5_ssm",
    )(proj, lag_t, p_c, q_c, coef)


MOBA_Q_BLOCKS = 2
MOBA_VT_ROWS = HEAD_DIM + 16
LOG2E = math.log2(math.e)
MOBA_AUG_HI = 8
MOBA_ITEMS_PER_STEP = 18


def _moba_schedule(n_tiles):
    rows = []
    for ti in range(n_tiles):
        for n, c in enumerate([ti] + list(range(ti))):
            rows.append((ti, c, int(n == 0)))
    assert len(rows) % 2 == 0, "the pipelined loop retires an even number of work items per step"
    rows += [(rows[-1][0], rows[-1][1], 0)] * 2
    return np.asarray(rows, np.int32).T.copy()


def _moba_kernel(slope_ref, sched_ref, q_ref, k_ref, v_ref, o_ref,
                 ka_ref, kb_ref, vt_ref, km_ref, tri_ref, qt_ref, qa_ref, acc_ref,
                 s0_ref, s1_ref, p0_ref, p1_ref, *, n_blocks, n_items):
    blk = MOBA_BLOCK
    nq = MOBA_Q_BLOCKS
    tq = nq * blk
    n_tiles = n_blocks // nq
    seq = n_blocks * blk
    hp = pl.program_id(1)
    slopes = (slope_ref[2 * hp] * LOG2E, slope_ref[2 * hp + 1] * LOG2E)
    k_refs = (ka_ref, kb_ref)

    lane = _iota((blk, LANES), 1)
    key_off = _iota((blk, LANES), 0).astype(F32)
    ones_row = (_iota((MOBA_VT_ROWS - HEAD_DIM, blk), 0) == 0).astype(BF16)
    in_heads = (lane < HEAD_DIM, lane >= HEAD_DIM)
    aug_lane = (lane - HEAD_DIM, lane)
    key_bias = []
    for e in range(2):
        a = slopes[e] * key_off
        a_hi = a.astype(BF16).astype(F32)
        key_bias.append(jnp.where(aug_lane[e] == 0, a_hi, jnp.where(aug_lane[e] == 1, a - a_hi, 0.0)))
    for j in range(n_blocks):
        kj = k_ref[j * blk:(j + 1) * blk, :].astype(F32)
        for e in range(2):
            one_hot = (aug_lane[e] == MOBA_AUG_HI + j) | (aug_lane[e] == MOBA_AUG_HI + n_blocks + j)
            aug = jnp.where(one_hot, 1.0, key_bias[e])
            k_refs[e][j] = jnp.where(in_heads[e], kj, aug).astype(BF16)
        v_t = v_ref[j * blk:(j + 1) * blk, :].astype(F32).T
        for e in range(2):
            vt_ref[j, e, 0:HEAD_DIM, :] = v_t[e * HEAD_DIM:(e + 1) * HEAD_DIM, :].astype(BF16)
            vt_ref[j, e, HEAD_DIM:MOBA_VT_ROWS, :] = ones_row
        km_ref[j:j + 1, :] = jnp.mean(kj, axis=0, keepdims=True)
    d = _iota((blk, blk), 1) - _iota((blk, blk), 0)
    tri_ref[0] = jnp.zeros((blk, blk), F32)
    tri_ref[1] = jnp.where(d >= 0, 0.0, NEG_INF)

    for ti in range(n_tiles):
        cols = slice(ti * tq, (ti + 1) * tq)
        q_tile = q_ref[cols, :].astype(F32) * (HEAD_DIM ** -0.5 * LOG2E)
        qt_ref[:, cols] = q_tile.T
    q_bf = qt_ref[...].astype(BF16)
    km = km_ref[...]
    lane_m = _iota(km.shape, 1)
    sub = 8
    groups = range(0, n_blocks, sub)
    q_off = _iota((sub, blk), 1)
    row_ids = [_iota((sub, blk), 0) + g0 for g0 in groups]
    hidden_rows = jnp.full((sub, blk), NEG_INF, F32)
    ones_rows = (_iota((MOBA_AUG_HI, tq), 0) < 2).astype(F32)
    pad_rows = jnp.zeros((HEAD_DIM - MOBA_AUG_HI - 2 * n_blocks, tq), F32)
    for e in range(2):
        in_head = (lane_m < HEAD_DIM) if e == 0 else (lane_m >= HEAD_DIM)
        km_e = jnp.where(in_head, km, 0.0).astype(BF16)
        gate = jnp.dot(km_e, q_bf, preferred_element_type=F32)
        bias_hi = [[] for _ in groups]
        bias_lo = [[] for _ in groups]
        for qb in range(n_blocks):
            cols = slice(qb * blk, (qb + 1) * blk)
            for gi, g0 in enumerate(groups):
                if g0 > qb:
                    bias_hi[gi].append(hidden_rows)
                    bias_lo[gi].append(jnp.zeros((sub, blk), F32))
                    continue
                rows = gate[g0:g0 + sub, cols]
                row_id = row_ids[gi]
                rank = jnp.zeros((sub, blk), jnp.int32)
                for jp in range(qb):
                    gj = gate[jp:jp + 1, cols]
                    if jp < g0:
                        beats = gj >= rows
                    elif jp >= g0 + sub:
                        beats = gj > rows
                    else:
                        beats = (gj > rows) | ((gj == rows) & (row_id > jp))
                    rank = rank + beats.astype(jnp.int32)
                visible = ((row_id < qb) & (rank < MOBA_TOPK)) | (row_id == qb)
                distance = ((qb - row_id) * blk + q_off).astype(F32)
                bias = jnp.where(visible, -slopes[e] * distance, NEG_INF)
                hi = bias.astype(BF16).astype(F32)
                bias_hi[gi].append(hi)
                bias_lo[gi].append(bias - hi)
        for ti in range(n_tiles):
            cols = slice(ti * tq, (ti + 1) * tq)
            pieces = [jnp.concatenate(per_group[nq * ti:nq * (ti + 1)], axis=1) for per_group in bias_hi + bias_lo]
            aug = jnp.concatenate([ones_rows] + pieces + [pad_rows], axis=0)
            q_head = qt_ref[e * HEAD_DIM:(e + 1) * HEAD_DIM, cols]
            parts = [q_head, aug] if e == 0 else [aug, q_head]
            qa_ref[ti, e] = jnp.concatenate(parts, axis=0).astype(BF16)

    def score(w, s_ref):
        ti = sched_ref[0, w]
        c = sched_ref[1, w]
        on_diag = sched_ref[2, w]
        maxes = []
        for e in range(2):
            q_aug = qa_ref[ti, e]
            m_blk = None
            for slab in range(nq):
                s_t = jnp.dot(k_refs[e][nq * c + slab], q_aug, preferred_element_type=F32)
                parts = [s_t[:, h * blk:(h + 1) * blk] + tri_ref[on_diag] if h == slab
                         else s_t[:, h * blk:(h + 1) * blk] for h in range(nq)]
                s_t = jnp.concatenate(parts, axis=1)
                s_ref[e, slab] = s_t
                mx = jnp.max(s_t, axis=0, keepdims=True)
                m_blk = mx if m_blk is None else jnp.maximum(m_blk, mx)
            maxes.append(m_blk)
        return tuple(maxes)

    def exponentiate(s_ref, p_ref, m):
        for e in range(2):
            for slab in range(nq):
                p_ref[e, slab] = jnp.exp2(s_ref[e, slab] - m[e]).astype(BF16)

    def accumulate(w, p_ref, alpha):
        ti = sched_ref[0, w]
        c = sched_ref[1, w]
        for e in range(2):
            pv = None
            for slab in range(nq):
                dd = jnp.dot(vt_ref[nq * c + slab, e], p_ref[e, slab], preferred_element_type=F32)
                pv = dd if pv is None else pv + dd
            acc_ref[ti, e] = alpha[e] * acc_ref[ti, e] + pv

    def retire(w, carry, s_next, p_next, s_free, p_cur):
        m_prev, m_cur, m_blk_next = carry
        starts_tile = sched_ref[2, w] == 1
        next_starts_tile = sched_ref[2, w + 1] == 1
        m_next = tuple(jnp.where(next_starts_tile, m_blk_next[e], jnp.maximum(m_cur[e], m_blk_next[e]))
                       for e in range(2))
        m_blk_new = score(w + 2, s_free)
        exponentiate(s_next, p_next, m_next)
        alpha = tuple(jnp.exp2(jnp.where(starts_tile, NEG_INF, m_prev[e]) - m_cur[e]) for e in range(2))
        accumulate(w, p_cur, alpha)
        return m_cur, m_next, m_blk_new

    acc_ref[...] = jnp.zeros_like(acc_ref)
    m_first = score(0, s0_ref)
    exponentiate(s0_ref, p0_ref, m_first)
    carry = (m_first, m_first, score(1, s1_ref))

    per_step = MOBA_ITEMS_PER_STEP if n_items % MOBA_ITEMS_PER_STEP == 0 else 2

    def body(t, carry):
        for i in range(per_step):
            bufs = (s1_ref, p1_ref, s0_ref, p0_ref) if i % 2 == 0 else (s0_ref, p0_ref, s1_ref, p1_ref)
            carry = retire(per_step * t + i, carry, *bufs)
        return carry

    lax.fori_loop(0, n_items // per_step, body, carry)

    for ti in range(n_tiles):
        out_t = jnp.concatenate([acc_ref[ti, e, 0:HEAD_DIM, :] / acc_ref[ti, e, HEAD_DIM:HEAD_DIM + 1, :]
                                 for e in range(2)], axis=0)
        o_ref[ti * tq:(ti + 1) * tq, :] = out_t.T.astype(o_ref.dtype)


def _moba(qkv, slopes, bsz, s, n_heads):
    n_blocks = s // MOBA_BLOCK
    n_pairs = n_heads // 2
    blk = MOBA_BLOCK
    tq = MOBA_Q_BLOCKS * blk
    n_tiles = s // tq
    sched = _moba_schedule(n_tiles)
    n_items = sched.shape[1] - 2
    logits = pltpu.VMEM((2, MOBA_Q_BLOCKS, blk, tq), F32)
    probs = pltpu.VMEM((2, MOBA_Q_BLOCKS, blk, tq), BF16)
    return pl.pallas_call(
        functools.partial(_moba_kernel, n_blocks=n_blocks, n_items=n_items),
        grid=(bsz, n_pairs),
        in_specs=[pl.BlockSpec(memory_space=pltpu.SMEM),
                  pl.BlockSpec(memory_space=pltpu.SMEM),
                  pl.BlockSpec((s, LANES), lambda b, h: (b, h)),
                  pl.BlockSpec((s, LANES), lambda b, h: (b, n_pairs + h)),
                  pl.BlockSpec((s, LANES), lambda b, h: (b, 2 * n_pairs + h))],
        out_specs=pl.BlockSpec((s, LANES), lambda b, h: (b, h)),
        out_shape=jax.ShapeDtypeStruct((bsz * s, n_heads * HEAD_DIM), BF16),
        scratch_shapes=[pltpu.VMEM((n_blocks, blk, LANES), BF16),
                        pltpu.VMEM((n_blocks, blk, LANES), BF16),
                        pltpu.VMEM((n_blocks, 2, MOBA_VT_ROWS, blk), BF16),
                        pltpu.VMEM((n_blocks, LANES), F32),
                        pltpu.VMEM((2, blk, blk), F32),
                        pltpu.VMEM((LANES, s), F32),
                        pltpu.VMEM((n_tiles, 2, LANES, tq), BF16),
                        pltpu.VMEM((n_tiles, 2, MOBA_VT_ROWS, tq), F32),
                        logits, logits, probs, probs],
        compiler_params=_params(2),
        name="moba",
    )(slopes, jnp.asarray(sched), qkv, qkv, qkv)


def _memory_attention(q, mem_k, mem_v):
    q = q * (HEAD_DIM ** -0.5)
    lane = _iota(q.shape, 1)
    out = jnp.zeros(q.shape, F32)
    for h in range(MEM_HEADS):
        in_head = (lane >= h * HEAD_DIM) & (lane < (h + 1) * HEAD_DIM)
        qh = jnp.where(in_head, q, 0.0).astype(BF16)
        s = lax.dot_general(qh, mem_k, (((1,), (1,)), ((), ())), preferred_element_type=F32)
        m = jnp.max(s, axis=-1, keepdims=True)
        p = jnp.exp(s - m)
        l = jnp.sum(p, axis=-1, keepdims=True)
        pv = jnp.dot(p.astype(BF16), mem_v, preferred_element_type=F32)
        out = jnp.where(in_head, pv / l, out)
    return out


def _project_out(tok, mem_out, wo_ref, x):
    tw = tok.shape[1]
    return (x + jnp.dot(tok.astype(BF16), wo_ref[0:tw, :], preferred_element_type=F32)
            + jnp.dot(mem_out.astype(BF16), wo_ref[tw:, :], preferred_element_type=F32))


def _s5_mix_out_kernel(y_ref, u_ref, d_ref, wglu_ref, bglu_ref, mq_ref, mk_ref, mv_ref, wo_ref, x_ref, o_ref):
    y = y_ref[...] + d_ref[...] * u_ref[...]
    y = 0.5 * y * (1.0 + lax.erf(y * math.sqrt(0.5)))
    z = jnp.dot(y.astype(BF16), wglu_ref[...], preferred_element_type=F32) + bglu_ref[...]
    tok = y * jax.nn.sigmoid(z)
    mem_out = _memory_attention(mq_ref[...], mk_ref[...], mv_ref[...])
    o_ref[...] = _project_out(tok, mem_out, wo_ref, x_ref[...])


def _mix_out_specs(n, s, d, tok_w, mem_len, mq_block_col):
    tm = min(2 * ROW_TILE, s)
    per_b = s // tm
    row = lambda i: (i, 0)
    const = lambda i: (0, 0)
    mem_specs = [pl.BlockSpec((tm, MEM_WIDTH), lambda i: (i, mq_block_col)),
                 pl.BlockSpec((mem_len, MEM_WIDTH), lambda i: (i // per_b, 0)),
                 pl.BlockSpec((mem_len, MEM_WIDTH), lambda i: (i // per_b, 1)),
                 pl.BlockSpec((d, d), const),
                 pl.BlockSpec((tm, d), row)]
    return tm, row, const, mem_specs


def _s5_mix_out(y_ssm, proj, d_skip, w_glu, b_glu, mem_kv, w_out, x2d, s, mem_len):
    n, d = x2d.shape
    tok_w = y_ssm.shape[1]
    tm, row, const, mem_specs = _mix_out_specs(n, s, d, tok_w, mem_len, tok_w // MEM_WIDTH)
    return pl.pallas_call(
        _s5_mix_out_kernel,
        grid=(n // tm,),
        in_specs=[pl.BlockSpec((tm, tok_w), row),
                  pl.BlockSpec((tm, tok_w), row),
                  pl.BlockSpec((1, tok_w), const),
                  pl.BlockSpec((tok_w, tok_w), const),
                  pl.BlockSpec((1, tok_w), const)] + mem_specs,
        out_specs=pl.BlockSpec((tm, d), row),
        out_shape=jax.ShapeDtypeStruct((n, d), F32),
        compiler_params=_params(1),
        name="s5_mix_out",
    )(y_ssm, proj, d_skip.reshape(1, tok_w), w_glu, b_glu.reshape(1, tok_w), proj, mem_kv, mem_kv, w_out, x2d)


FFN_CHUNK = MXU_TILE


def _ffn_kernel(x_ref, g_ref, wg_ref, wu_ref, wd_ref, gf_ref, o_ref, act_ref, *, final_norm):
    _ffn_block(x_ref[...], g_ref, wg_ref, wu_ref, wd_ref, gf_ref, o_ref, act_ref, final_norm)


def _moba_tail_kernel(tok_ref, mq_ref, mk_ref, mv_ref, wo_ref, x_ref, g_ref, wg_ref, wu_ref, wd_ref, gf_ref,
                      o_ref, act_ref, *, final_norm):
    mem_out = _memory_attention(mq_ref[...].astype(F32), mk_ref[...], mv_ref[...])
    x = _project_out(tok_ref[...], mem_out, wo_ref, x_ref[...])
    _ffn_block(x, g_ref, wg_ref, wu_ref, wd_ref, gf_ref, o_ref, act_ref, final_norm)


def _ffn_block(x, g_ref, wg_ref, wu_ref, wd_ref, gf_ref, o_ref, act_ref, final_norm):
    h = _rms_scale(x, g_ref[...]).astype(BF16)
    hidden = wg_ref.shape[1]
    for c in range(hidden // FFN_CHUNK):
        cols = slice(c * FFN_CHUNK, (c + 1) * FFN_CHUNK)
        gate = jnp.dot(h, wg_ref[:, cols], preferred_element_type=F32)
        up = jnp.dot(h, wu_ref[:, cols], preferred_element_type=F32)
        act_ref[:, cols] = (gate * jax.nn.sigmoid(gate) * up).astype(BF16)
    y = x + jnp.dot(act_ref[...], wd_ref[...], preferred_element_type=F32)
    if final_norm:
        y = _rms_scale(y, gf_ref[...])
    o_ref[...] = y


def _ffn(x2d, g, w_gate, w_up, w_down, g_final, final_norm):
    n, d = x2d.shape
    hidden = w_gate.shape[1]
    tm = min(2 * ROW_TILE, n)
    row = lambda i: (i, 0)
    const = lambda i: (0, 0)
    resident = pl.Buffered(1)
    return pl.pallas_call(
        functools.partial(_ffn_kernel, final_norm=final_norm),
        grid=(n // tm,),
        in_specs=[pl.BlockSpec((tm, d), row),
                  pl.BlockSpec((1, d), const),
                  pl.BlockSpec((d, hidden), const, pipeline_mode=resident),
                  pl.BlockSpec((d, hidden), const, pipeline_mode=resident),
                  pl.BlockSpec((hidden, d), const, pipeline_mode=resident),
                  pl.BlockSpec((1, d), const)],
        out_specs=pl.BlockSpec((tm, d), row),
        out_shape=jax.ShapeDtypeStruct((n, d), F32),
        scratch_shapes=[pltpu.VMEM((tm, hidden), BF16)],
        compiler_params=_params(1),
        name="ffn",
    )(x2d, g.reshape(1, d), w_gate, w_up, w_down, g_final.reshape(1, d))


def _moba_tail(tok, proj, mem_kv, w_out, x2d, s, mem_len, g, w_gate, w_up, w_down, g_final, final_norm):
    n, d = x2d.shape
    tok_w = tok.shape[1]
    hidden = w_gate.shape[1]
    tm = min(2 * ROW_TILE, s)
    per_b = s // tm
    row = lambda i: (i, 0)
    const = lambda i: (0, 0)
    resident = pl.Buffered(1)
    mq_col = (proj.shape[1] - MEM_WIDTH) // MEM_WIDTH
    return pl.pallas_call(
        functools.partial(_moba_tail_kernel, final_norm=final_norm),
        grid=(n // tm,),
        in_specs=[pl.BlockSpec((tm, tok_w), row),
                  pl.BlockSpec((tm, MEM_WIDTH), lambda i: (i, mq_col)),
                  pl.BlockSpec((mem_len, MEM_WIDTH), lambda i: (i // per_b, 0)),
                  pl.BlockSpec((mem_len, MEM_WIDTH), lambda i: (i // per_b, 1)),
                  pl.BlockSpec((d, d), const, pipeline_mode=resident),
                  pl.BlockSpec((tm, d), row),
                  pl.BlockSpec((1, d), const),
                  pl.BlockSpec((d, hidden), const, pipeline_mode=resident),
                  pl.BlockSpec((d, hidden), const, pipeline_mode=resident),
                  pl.BlockSpec((hidden, d), const, pipeline_mode=resident),
                  pl.BlockSpec((1, d), const)],
        out_specs=pl.BlockSpec((tm, d), row),
        out_shape=jax.ShapeDtypeStruct((n, d), F32),
        scratch_shapes=[pltpu.VMEM((tm, hidden), BF16)],
        compiler_params=_params(1),
        name="moba_tail",
    )(tok, proj, mem_kv, mem_kv, w_out, x2d, g.reshape(1, d), w_gate, w_up, w_down, g_final.reshape(1, d))


def kernel(x, mem, mem_norm_g, w_mem_kv, mix_norm_g, s5_w_in, s5_lambda_re, s5_lambda_im, s5_log_dt,
           s5_b_re, s5_b_im, s5_c_re, s5_c_im, s5_d, s5_w_glu, s5_b_glu, moba_w_in, w_out, ffn_norm_g,
           w_gate, w_up, w_down, final_norm_g):
    bsz, s, d = x.shape
    mem_len = mem.shape[1]
    depth = mix_norm_g.shape[0]
    tok_w = d - MEM_WIDTH
    n_heads = tok_w // HEAD_DIM
    slopes = jnp.asarray(2.0 ** (-8.0 * (np.arange(n_heads) + 1) / n_heads), dtype=F32)

    mem_kv = _norm_matmul(mem.reshape(bsz * mem_len, d), mem_norm_g, w_mem_kv.astype(BF16), out_dtype=BF16)
    x2d = x.reshape(bsz * s, d)
    for i in range(depth):
        j = i // 2
        if i % 2 == 0:
            proj = _norm_matmul(x2d, mix_norm_g[i], s5_w_in[j].astype(BF16), row_tile=4 * ROW_TILE)
            ops = _s5_operators(s5_lambda_re[j], s5_lambda_im[j], s5_log_dt[j],
                                s5_b_re[j], s5_b_im[j], s5_c_re[j], s5_c_im[j])
            y_ssm = _s5_ssm(proj, ops, bsz, s, tok_w)
            x2d = _s5_mix_out(y_ssm, proj, s5_d[j], s5_w_glu[j].astype(BF16), s5_b_glu[j],
                              mem_kv, w_out[i].astype(BF16), x2d, s, mem_len)
        else:
            proj = _norm_matmul(x2d, mix_norm_g[i], moba_w_in[j].astype(BF16), out_dtype=BF16,
                                row_tile=2 * ROW_TILE)
            tok = _moba(proj, slopes, bsz, s, n_heads)
            x2d = _moba_tail(tok, proj, mem_kv, w_out[i].astype(BF16), x2d, s, mem_len, ffn_norm_g[i],
                             w_gate[i].astype(BF16), w_up[i].astype(BF16), w_down[i].astype(BF16),
                             final_norm_g, final_norm=(i == depth - 1))
            continue
        x2d = _ffn(x2d, ffn_norm_g[i], w_gate[i].astype(BF16), w_up[i].astype(BF16),
                   w_down[i].astype(BF16), final_norm_g, final_norm=(i == depth - 1))
    return x2d.reshape(bsz, s, d)
```

```python
import functools
import math

import numpy as np
import jax
import jax.numpy as jnp
from jax import lax
from jax.experimental import pallas as pl
from jax.experimental.pallas import tpu as pltpu

F32 = jnp.float32
BF16 = jnp.bfloat16

HEAD_DIM = 64
MEM_HEADS = 4
MEM_WIDTH = MEM_HEADS * HEAD_DIM
S5_GROUP_CH = 16
S5_STATE = 64
S5_CHUNK = 8
MOBA_BLOCK = 256
MOBA_TOPK = 3
RMS_EPS = 1e-6
NEG_INF = -1e30

LANES = 128
MXU_TILE = 256
VMEM_LIMIT_BYTES = 56 * 1024 * 1024
ROW_TILE = 512

S5_LANE_GROUPS = LANES // S5_GROUP_CH
S5_STEPS_PER_TILE = MXU_TILE // LANES
S5_LAG_TILES = S5_CHUNK // S5_STEPS_PER_TILE
S5_SEQS_PER_STEP = 2


def _params(n_axes):
    return pltpu.CompilerParams(dimension_semantics=("arbitrary",) * n_axes,
                                vmem_limit_bytes=VMEM_LIMIT_BYTES)


def _iota(shape, dim):
    return lax.broadcasted_iota(jnp.int32, shape, dim)


def _rms_scale(x, g):
    ms = jnp.mean(x * x, axis=-1, keepdims=True)
    return x * lax.rsqrt(ms + RMS_EPS) * g


def _norm_matmul_kernel(x_ref, g_ref, w_ref, o_ref):
    h = _rms_scale(x_ref[...], g_ref[...])
    o_ref[...] = jnp.dot(h.astype(BF16), w_ref[...], preferred_element_type=F32).astype(o_ref.dtype)


def _norm_matmul(x2d, g, w_bf16, out_dtype=F32, row_tile=ROW_TILE):
    n, d = x2d.shape
    f = w_bf16.shape[1]
    tm = min(row_tile, n)
    return pl.pallas_call(
        _norm_matmul_kernel,
        grid=(n // tm,),
        in_specs=[pl.BlockSpec((tm, d), lambda i: (i, 0)),
                  pl.BlockSpec((1, d), lambda i: (0, 0)),
                  pl.BlockSpec((d, f), lambda i: (0, 0))],
        out_specs=pl.BlockSpec((tm, f), lambda i: (i, 0)),
        out_shape=jax.ShapeDtypeStruct((n, f), out_dtype),
        compiler_params=_params(1),
        name="norm_matmul",
    )(x2d, g.reshape(1, d), w_bf16)


def _s5_operators(lam_re, lam_im, log_dt, b_re, b_im, c_re, c_im):
    hi = lax.Precision.HIGHEST
    t_len = S5_CHUNK
    g_cnt, p_cnt = lam_re.shape
    c_cnt = b_re.shape[-1]
    gpb = S5_LANE_GROUPS
    n_lb = g_cnt // gpb
    spt = S5_STEPS_PER_TILE
    lr = lam_re.astype(F32)
    li = lam_im.astype(F32)
    dt = jnp.exp(log_dt.astype(F32))[:, None]
    mag = jnp.exp(lr * dt)
    ar = mag * jnp.cos(li * dt)
    ai = mag * jnp.sin(li * dt)
    den = lr * lr + li * li
    nr = ar - 1.0
    fr = (nr * lr + ai * li) / den
    fi = (ai * lr - nr * li) / den
    br = jnp.swapaxes(b_re.astype(F32), 1, 2)
    bi = jnp.swapaxes(b_im.astype(F32), 1, 2)
    bbr = fr[:, None, :] * br - fi[:, None, :] * bi
    bbi = fr[:, None, :] * bi + fi[:, None, :] * br
    cr = c_re.astype(F32)
    ci = c_im.astype(F32)

    k = jnp.arange(t_len + 1, dtype=F32)[:, None, None]
    pm = jnp.exp(k * (lr * dt)[None])
    pr = pm * jnp.cos(k * (li * dt)[None])
    pi = pm * jnp.sin(k * (li * dt)[None])

    wr = pr[:, :, None, :] * bbr[None] - pi[:, :, None, :] * bbi[None]
    wi = pr[:, :, None, :] * bbi[None] + pi[:, :, None, :] * bbr[None]
    kern = (jnp.einsum('gop,kgcp->kgoc', cr, wr[:t_len], precision=hi)
            - jnp.einsum('gop,kgcp->kgoc', ci, wi[:t_len], precision=hi))
    lag = (jnp.arange(S5_LAG_TILES)[:, None, None] * spt
           + jnp.arange(spt)[None, None, :] - jnp.arange(spt)[None, :, None])
    kd = jnp.where((lag >= 0)[..., None, None, None], kern[jnp.clip(lag, 0, t_len - 1)], 0.0)
    kd = kd.reshape(S5_LAG_TILES, spt, spt, n_lb, gpb, c_cnt, c_cnt)
    lag_t = kd.transpose(3, 0, 2, 5, 1, 4, 6).reshape(n_lb, S5_LAG_TILES, spt * c_cnt, spt * gpb * c_cnt)

    def by_lane_block(x):
        return x.reshape(t_len, n_lb, gpb, c_cnt, 2 * p_cnt).transpose(1, 0, 2, 3, 4).reshape(
            n_lb, t_len * gpb * c_cnt, 2 * p_cnt)

    p_c = by_lane_block(jnp.concatenate([wr[:t_len][::-1], wi[:t_len][::-1]], axis=-1))
    qr = cr[None] * pr[1:, :, None, :] - ci[None] * pi[1:, :, None, :]
    qi = -(cr[None] * pi[1:, :, None, :] + ci[None] * pr[1:, :, None, :])
    q_c = by_lane_block(jnp.concatenate([qr, qi], axis=-1))

    coef = jnp.stack([pr[t_len].reshape(n_lb, gpb * p_cnt),
                      pi[t_len].reshape(n_lb, gpb * p_cnt)], axis=1)
    return lag_t.astype(BF16), p_c.astype(BF16), q_c.astype(BF16), coef


def _s5_ssm_kernel(u_ref, lag_ref, pc_ref, qc_ref, coef_ref, y_ref,
                   toep_scr, p_scr, q_scr, h_scr, s_scr, xp_scr, *, n_chunks, n_seqs):
    t_len = S5_CHUNK
    seq_len = n_chunks * t_len
    gpb = S5_LANE_GROUPS
    st = S5_STATE
    cc = S5_GROUP_CH
    half = gpb * st
    flat_w = t_len * LANES
    nt_dims = (((1,), (1,)), ((), ()))

    @pl.when(pl.program_id(1) == 0)
    def _expand_operators():
        def group_diag(x, row_unit, col_unit):
            same = ((_iota(x.shape, 0) // row_unit) % gpb) == ((_iota(x.shape, 1) // col_unit) % gpb)
            return jnp.where(same, x, 0.0).astype(BF16)

        shape = (2 * st, 2 * half)
        rep = ((_iota(shape, 0) // st == _iota(shape, 1) // half)
               & (_iota(shape, 0) % st == _iota(shape, 1) % st)).astype(BF16)
        for r in range(flat_w // MXU_TILE):
            rows = slice(r * MXU_TILE, (r + 1) * MXU_TILE)
            p_scr[rows, :] = group_diag(jnp.dot(pc_ref[0, rows, :], rep, preferred_element_type=F32), cc, st)
        shape = (2 * half, 2 * st)
        rep = ((_iota(shape, 0) // half == _iota(shape, 1) // st)
               & (_iota(shape, 0) % st == _iota(shape, 1) % st)).astype(BF16)
        for c in range(flat_w // MXU_TILE):
            cols = slice(c * MXU_TILE, (c + 1) * MXU_TILE)
            q_scr[:, cols] = group_diag(lax.dot_general(rep, qc_ref[0, cols, :], nt_dims,
                                                        preferred_element_type=F32), st, cc)
        shape = (MXU_TILE, S5_STEPS_PER_TILE * cc)
        rep = ((_iota(shape, 0) // LANES == _iota(shape, 1) // cc)
               & (_iota(shape, 0) % cc == _iota(shape, 1) % cc)).astype(BF16)
        for l in range(S5_LAG_TILES):
            toep_scr[l] = group_diag(jnp.dot(rep, lag_ref[0, l], preferred_element_type=F32), cc, cc)

    for j in range(t_len):
        for b in range(n_seqs):
            h_scr[b * n_chunks:(b + 1) * n_chunks, j * LANES:(j + 1) * LANES] = (
                u_ref[pl.ds(b * seq_len + j, n_chunks, stride=t_len), :].astype(BF16))
    s_scr[...] = jnp.dot(h_scr[...], p_scr[...], preferred_element_type=F32)
    cr = coef_ref[0, 0:1, :]
    ci = coef_ref[0, 1:2, :]

    xr = [jnp.zeros((1, half), F32)] * n_seqs
    xi = [jnp.zeros((1, half), F32)] * n_seqs
    for n in range(n_chunks):
        for b in range(n_seqs):
            row = slice(b * n_chunks + n, b * n_chunks + n + 1)
            xp_scr[row, 0:half] = xr[b]
            xp_scr[row, half:2 * half] = xi[b]
            sr = s_scr[row, 0:half]
            si = s_scr[row, half:2 * half]
            xr[b], xi[b] = cr * xr[b] - ci * xi[b] + sr, cr * xi[b] + ci * xr[b] + si
    xp = xp_scr[...].astype(BF16)
    for nt in range(S5_LAG_TILES):
        cols = slice(nt * MXU_TILE, (nt + 1) * MXU_TILE)
        acc = jnp.dot(xp, q_scr[:, cols], preferred_element_type=F32)
        for kt in range(nt + 1):
            acc = acc + lax.dot_general(h_scr[:, kt * MXU_TILE:(kt + 1) * MXU_TILE], toep_scr[nt - kt],
                                        nt_dims, preferred_element_type=F32)
        for tt in range(S5_STEPS_PER_TILE):
            t = nt * S5_STEPS_PER_TILE + tt
            for b in range(n_seqs):
                y_ref[pl.ds(b * seq_len + t, n_chunks, stride=t_len), :] = (
                    acc[b * n_chunks:(b + 1) * n_chunks, tt * LANES:(tt + 1) * LANES])


def _s5_ssm(proj, ops, bsz, s, tok_w):
    lag_t, p_c, q_c, coef = ops
    n_lb = tok_w // LANES
    n_chunks = s // S5_CHUNK
    flat_w = p_c.shape[1]
    state_w = S5_LANE_GROUPS * p_c.shape[2]
    const3 = lambda l, b: (l, 0, 0)
    n_seqs = S5_SEQS_PER_STEP if bsz % S5_SEQS_PER_STEP == 0 else 1
    rows = n_seqs * n_chunks
    return pl.pallas_call(
        functools.partial(_s5_ssm_kernel, n_chunks=n_chunks, n_seqs=n_seqs),
        grid=(n_lb, bsz // n_seqs),
        in_specs=[pl.BlockSpec((n_seqs * s, LANES), lambda l, b: (b, l)),
                  pl.BlockSpec((1,) + lag_t.shape[1:], lambda l, b: (l, 0, 0, 0)),
                  pl.BlockSpec((1,) + p_c.shape[1:], const3),
                  pl.BlockSpec((1,) + q_c.shape[1:], const3),
                  pl.BlockSpec((1,) + coef.shape[1:], const3)],
        out_specs=pl.BlockSpec((n_seqs * s, LANES), lambda l, b: (b, l)),
        out_shape=jax.ShapeDtypeStruct((bsz * s, tok_w), F32),
        scratch_shapes=[pltpu.VMEM((S5_LAG_TILES, MXU_TILE, MXU_TILE), BF16),
                        pltpu.VMEM((flat_w, state_w), BF16),
                        pltpu.VMEM((state_w, flat_w), BF16),
                        pltpu.VMEM((rows, flat_w), BF16),
                        pltpu.VMEM((rows, state_w), F32),
                        pltpu.VMEM((rows, state_w), F32)],
        compiler_params=_params(2),
        name="s5_ssm",
    )(proj, lag_t, p_c, q_c, coef)


MOBA_Q_BLOCKS = 2
MOBA_VT_ROWS = HEAD_DIM + 16
LOG2E = math.log2(math.e)
MOBA_AUG_HI = 8
MOBA_ITEMS_PER_STEP = 18


def _moba_schedule(n_tiles):
    rows = []
    for ti in range(n_tiles):
        for n, c in enumerate([ti] + list(range(ti))):
            rows.append((ti, c, int(n == 0)))
    assert len(rows) % 2 == 0, "the pipelined loop retires an even number of work items per step"
    rows += [(rows[-1][0], rows[-1][1], 0)] * 2
    return np.asarray(rows, np.int32).T.copy()


def _moba_kernel(slope_ref, sched_ref, q_ref, k_ref, v_ref, o_ref,
                 ka_ref, kb_ref, vt_ref, km_ref, tri_ref, qt_ref, qa_ref, acc_ref,
                 s0_ref, s1_ref, p0_ref, p1_ref, *, n_blocks, n_items):
    blk = MOBA_BLOCK
    nq = MOBA_Q_BLOCKS
    tq = nq * blk
    n_tiles = n_blocks // nq
    seq = n_blocks * blk
    hp = pl.program_id(1)
    slopes = (slope_ref[2 * hp] * LOG2E, slope_ref[2 * hp + 1] * LOG2E)
    k_refs = (ka_ref, kb_ref)

    lane = _iota((blk, LANES), 1)
    key_off = _iota((blk, LANES), 0).astype(F32)
    ones_row = (_iota((MOBA_VT_ROWS - HEAD_DIM, blk), 0) == 0).astype(BF16)
    in_heads = (lane < HEAD_DIM, lane >= HEAD_DIM)
    aug_lane = (lane - HEAD_DIM, lane)
    key_bias = []
    for e in range(2):
        a = slopes[e] * key_off
        a_hi = a.astype(BF16).astype(F32)
        key_bias.append(jnp.where(aug_lane[e] == 0, a_hi, jnp.where(aug_lane[e] == 1, a - a_hi, 0.0)))
    for j in range(n_blocks):
        kj = k_ref[j * blk:(j + 1) * blk, :].astype(F32)
        for e in range(2):
            one_hot = (aug_lane[e] == MOBA_AUG_HI + j) | (aug_lane[e] == MOBA_AUG_HI + n_blocks + j)
            aug = jnp.where(one_hot, 1.0, key_bias[e])
            k_refs[e][j] = jnp.where(in_heads[e], kj, aug).astype(BF16)
        v_t = v_ref[j * blk:(j + 1) * blk, :].astype(F32).T
        for e in range(2):
            vt_ref[j, e, 0:HEAD_DIM, :] = v_t[e * HEAD_DIM:(e + 1) * HEAD_DIM, :].astype(BF16)
            vt_ref[j, e, HEAD_DIM:MOBA_VT_ROWS, :] = ones_row
        km_ref[j:j + 1, :] = jnp.mean(kj, axis=0, keepdims=True)
    d = _iota((blk, blk), 1) - _iota((blk, blk), 0)
    tri_ref[0] = jnp.zeros((blk, blk), F32)
    tri_ref[1] = jnp.where(d >= 0, 0.0, NEG_INF)

    for ti in range(n_tiles):
        cols = slice(ti * tq, (ti + 1) * tq)
        q_tile = q_ref[cols, :].astype(F32) * (HEAD_DIM ** -0.5 * LOG2E)
        qt_ref[:, cols] = q_tile.T
    q_bf = qt_ref[...].astype(BF16)
    km = km_ref[...]
    lane_m = _iota(km.shape, 1)
    sub = 8
    groups = range(0, n_blocks, sub)
    q_off = _iota((sub, blk), 1)
    row_ids = [_iota((sub, blk), 0) + g0 for g0 in groups]
    hidden_rows = jnp.full((sub, blk), NEG_INF, F32)
    ones_rows = (_iota((MOBA_AUG_HI, tq), 0) < 2).astype(F32)
    pad_rows = jnp.zeros((HEAD_DIM - MOBA_AUG_HI - 2 * n_blocks, tq), F32)
    for e in range(2):
        in_head = (lane_m < HEAD_DIM) if e == 0 else (lane_m >= HEAD_DIM)
        km_e = jnp.where(in_head, km, 0.0).astype(BF16)
        gate = jnp.dot(km_e, q_bf, preferred_element_type=F32)
        bias_hi = [[] for _ in groups]
        bias_lo = [[] for _ in groups]
        for qb in range(n_blocks):
            cols = slice(qb * blk, (qb + 1) * blk)
            for gi, g0 in enumerate(groups):
                if g0 > qb:
                    bias_hi[gi].append(hidden_rows)
                    bias_lo[gi].append(jnp.zeros((sub, blk), F32))
                    continue
                rows = gate[g0:g0 + sub, cols]
                row_id = row_ids[gi]
                rank = jnp.zeros((sub, blk), jnp.int32)
                for jp in range(qb):
                    gj = gate[jp:jp + 1, cols]
                    if jp < g0:
                        beats = gj >= rows
                    elif jp >= g0 + sub:
                        beats = gj > rows
                    else:
                        beats = (gj > rows) | ((gj == rows) & (row_id > jp))
                    rank = rank + beats.astype(jnp.int32)
                visible = ((row_id < qb) & (rank < MOBA_TOPK)) | (row_id == qb)
                distance = ((qb - row_id) * blk + q_off).astype(F32)
                bias = jnp.where(visible, -slopes[e] * distance, NEG_INF)
                hi = bias.astype(BF16).astype(F32)
                bias_hi[gi].append(hi)
                bias_lo[gi].append(bias - hi)
        for ti in range(n_tiles):
            cols = slice(ti * tq, (ti + 1) * tq)
            pieces = [jnp.concatenate(per_group[nq * ti:nq * (ti + 1)], axis=1) for per_group in bias_hi + bias_lo]
            aug = jnp.concatenate([ones_rows] + pieces + [pad_rows], axis=0)
            q_head = qt_ref[e * HEAD_DIM:(e + 1) * HEAD_DIM, cols]
            parts = [q_head, aug] if e == 0 else [aug, q_head]
            qa_ref[ti, e] = jnp.concatenate(parts, axis=0).astype(BF16)

    def score(w, s_ref):
        ti = sched_ref[0, w]
        c = sched_ref[1, w]
        on_diag = sched_ref[2, w]
        maxes = []
        for e in range(2):
            q_aug = qa_ref[ti, e]
            m_blk = None
            for slab in range(nq):
                s_t = jnp.dot(k_refs[e][nq * c + slab], q_aug, preferred_element_type=F32)
                parts = [s_t[:, h * blk:(h + 1) * blk] + tri_ref[on_diag] if h == slab
                         else s_t[:, h * blk:(h + 1) * blk] for h in range(nq)]
                s_t = jnp.concatenate(parts, axis=1)
                s_ref[e, slab] = s_t
                mx = jnp.max(s_t, axis=0, keepdims=True)
                m_blk = mx if m_blk is None else jnp.maximum(m_blk, mx)
            maxes.append(m_blk)
        return tuple(maxes)

    def exponentiate(s_ref, p_ref, m):
        for e in range(2):
            for slab in range(nq):
                p_ref[e, slab] = jnp.exp2(s_ref[e, slab] - m[e]).astype(BF16)

    def accumulate(w, p_ref, alpha):
        ti = sched_ref[0, w]
        c = sched_ref[1, w]
        for e in range(2):
            pv = None
            for slab in range(nq):
                dd = jnp.dot(vt_ref[nq * c + slab, e], p_ref[e, slab], preferred_element_type=F32)
                pv = dd if pv is None else pv + dd
            acc_ref[ti, e] = alpha[e] * acc_ref[ti, e] + pv

    def retire(w, carry, s_next, p_next, s_free, p_cur):
        m_prev, m_cur, m_blk_next = carry
        starts_tile = sched_ref[2, w] == 1
        next_starts_tile = sched_ref[2, w + 1] == 1
        m_next = tuple(jnp.where(next_starts_tile, m_blk_next[e], jnp.maximum(m_cur[e], m_blk_next[e]))
                       for e in range(2))
        m_blk_new = score(w + 2, s_free)
        exponentiate(s_next, p_next, m_next)
        alpha = tuple(jnp.exp2(jnp.where(starts_tile, NEG_INF, m_prev[e]) - m_cur[e]) for e in range(2))
        accumulate(w, p_cur, alpha)
        return m_cur, m_next, m_blk_new

    acc_ref[...] = jnp.zeros_like(acc_ref)
    m_first = score(0, s0_ref)
    exponentiate(s0_ref, p0_ref, m_first)
    carry = (m_first, m_first, score(1, s1_ref))

    per_step = MOBA_ITEMS_PER_STEP if n_items % MOBA_ITEMS_PER_STEP == 0 else 2

    def body(t, carry):
        for i in range(per_step):
            bufs = (s1_ref, p1_ref, s0_ref, p0_ref) if i % 2 == 0 else (s0_ref, p0_ref, s1_ref, p1_ref)
            carry = retire(per_step * t + i, carry, *bufs)
        return carry

    lax.fori_loop(0, n_items // per_step, body, carry)

    for ti in range(n_tiles):
        out_t = jnp.concatenate([acc_ref[ti, e, 0:HEAD_DIM, :] / acc_ref[ti, e, HEAD_DIM:HEAD_DIM + 1, :]
                                 for e in range(2)], axis=0)
        o_ref[ti * tq:(ti + 1) * tq, :] = out_t.T.astype(o_ref.dtype)


def _moba(qkv, slopes, bsz, s, n_heads):
    n_blocks = s // MOBA_BLOCK
    n_pairs = n_heads // 2
    blk = MOBA_BLOCK
    tq = MOBA_Q_BLOCKS * blk
    n_tiles = s // tq
    sched = _moba_schedule(n_tiles)
    n_items = sched.shape[1] - 2
    logits = pltpu.VMEM((2, MOBA_Q_BLOCKS, blk, tq), F32)
    probs = pltpu.VMEM((2, MOBA_Q_BLOCKS, blk, tq), BF16)
    return pl.pallas_call(
        functools.partial(_moba_kernel, n_blocks=n_blocks, n_items=n_items),
        grid=(bsz, n_pairs),
        in_specs=[pl.BlockSpec(memory_space=pltpu.SMEM),
                  pl.BlockSpec(memory_space=pltpu.SMEM),
                  pl.BlockSpec((s, LANES), lambda b, h: (b, h)),
                  pl.BlockSpec((s, LANES), lambda b, h: (b, n_pairs + h)),
                  pl.BlockSpec((s, LANES), lambda b, h: (b, 2 * n_pairs + h))],
        out_specs=pl.BlockSpec((s, LANES), lambda b, h: (b, h)),
        out_shape=jax.ShapeDtypeStruct((bsz * s, n_heads * HEAD_DIM), BF16),
        scratch_shapes=[pltpu.VMEM((n_blocks, blk, LANES), BF16),
                        pltpu.VMEM((n_blocks, blk, LANES), BF16),
                        pltpu.VMEM((n_blocks, 2, MOBA_VT_ROWS, blk), BF16),
                        pltpu.VMEM((n_blocks, LANES), F32),
                        pltpu.VMEM((2, blk, blk), F32),
                        pltpu.VMEM((LANES, s), F32),
                        pltpu.VMEM((n_tiles, 2, LANES, tq), BF16),
                        pltpu.VMEM((n_tiles, 2, MOBA_VT_ROWS, tq), F32),
                        logits, logits, probs, probs],
        compiler_params=_params(2),
        name="moba",
    )(slopes, jnp.asarray(sched), qkv, qkv, qkv)


def _memory_attention(q, mem_k, mem_v):
    q = q * (HEAD_DIM ** -0.5)
    lane = _iota(q.shape, 1)
    out = jnp.zeros(q.shape, F32)
    for h in range(MEM_HEADS):
        in_head = (lane >= h * HEAD_DIM) & (lane < (h + 1) * HEAD_DIM)
        qh = jnp.where(in_head, q, 0.0).astype(BF16)
        s = lax.dot_general(qh, mem_k, (((1,), (1,)), ((), ())), preferred_element_type=F32)
        m = jnp.max(s, axis=-1, keepdims=True)
        p = jnp.exp(s - m)
        l = jnp.sum(p, axis=-1, keepdims=True)
        pv = jnp.dot(p.astype(BF16), mem_v, preferred_element_type=F32)
        out = jnp.where(in_head, pv / l, out)
    return out


def _project_out(tok, mem_out, wo_ref, x):
    tw = tok.shape[1]
    return (x + jnp.dot(tok.astype(BF16), wo_ref[0:tw, :], preferred_element_type=F32)
            + jnp.dot(mem_out.astype(BF16), wo_ref[tw:, :], preferred_element_type=F32))


def _s5_mixed_residual(y_ref, u_ref, d_ref, wglu_ref, bglu_ref, mq_ref, mk_ref, mv_ref, wo_ref, x_ref):
    y = y_ref[...] + d_ref[...] * u_ref[...]
    y = 0.5 * y * (1.0 + lax.erf(y * math.sqrt(0.5)))
    z = jnp.dot(y.astype(BF16), wglu_ref[...], preferred_element_type=F32) + bglu_ref[...]
    tok = y * jax.nn.sigmoid(z)
    mem_out = _memory_attention(mq_ref[...], mk_ref[...], mv_ref[...])
    return _project_out(tok, mem_out, wo_ref, x_ref[...])


def _s5_mix_out_kernel(y_ref, u_ref, d_ref, wglu_ref, bglu_ref, mq_ref, mk_ref, mv_ref, wo_ref, x_ref, o_ref):
    o_ref[...] = _s5_mixed_residual(y_ref, u_ref, d_ref, wglu_ref, bglu_ref, mq_ref, mk_ref, mv_ref, wo_ref, x_ref)


def _s5_tail_kernel(y_ref, u_ref, d_ref, wglu_ref, bglu_ref, mq_ref, mk_ref, mv_ref, wo_ref, x_ref,
                    g_ref, wg_ref, wu_ref, wd_ref, gf_ref, o_ref, act_ref, *, final_norm):
    x = _s5_mixed_residual(y_ref, u_ref, d_ref, wglu_ref, bglu_ref, mq_ref, mk_ref, mv_ref, wo_ref, x_ref)
    _ffn_block(x, g_ref, wg_ref, wu_ref, wd_ref, gf_ref, o_ref, act_ref, final_norm)


def _mix_out_specs(n, s, d, tok_w, mem_len, mq_block_col):
    tm = min(2 * ROW_TILE, s)
    per_b = s // tm
    row = lambda i: (i, 0)
    const = lambda i: (0, 0)
    mem_specs = [pl.BlockSpec((tm, MEM_WIDTH), lambda i: (i, mq_block_col)),
                 pl.BlockSpec((mem_len, MEM_WIDTH), lambda i: (i // per_b, 0)),
                 pl.BlockSpec((mem_len, MEM_WIDTH), lambda i: (i // per_b, 1)),
                 pl.BlockSpec((d, d), const),
                 pl.BlockSpec((tm, d), row)]
    return tm, row, const, mem_specs


def _s5_mix_out(y_ssm, proj, d_skip, w_glu, b_glu, mem_kv, w_out, x2d, s, mem_len):
    n, d = x2d.shape
    tok_w = y_ssm.shape[1]
    tm, row, const, mem_specs = _mix_out_specs(n, s, d, tok_w, mem_len, tok_w // MEM_WIDTH)
    return pl.pallas_call(
        _s5_mix_out_kernel,
        grid=(n // tm,),
        in_specs=[pl.BlockSpec((tm, tok_w), row),
                  pl.BlockSpec((tm, tok_w), row),
                  pl.BlockSpec((1, tok_w), const),
                  pl.BlockSpec((tok_w, tok_w), const),
                  pl.BlockSpec((1, tok_w), const)] + mem_specs,
        out_specs=pl.BlockSpec((tm, d), row),
        out_shape=jax.ShapeDtypeStruct((n, d), F32),
        compiler_params=_params(1),
        name="s5_mix_out",
    )(y_ssm, proj, d_skip.reshape(1, tok_w), w_glu, b_glu.reshape(1, tok_w), proj, mem_kv, mem_kv, w_out, x2d)


FFN_CHUNK = MXU_TILE


def _ffn_kernel(x_ref, g_ref, wg_ref, wu_ref, wd_ref, gf_ref, o_ref, act_ref, *, final_norm):
    _ffn_block(x_ref[...], g_ref, wg_ref, wu_ref, wd_ref, gf_ref, o_ref, act_ref, final_norm)


def _moba_tail_kernel(tok_ref, mq_ref, mk_ref, mv_ref, wo_ref, x_ref, g_ref, wg_ref, wu_ref, wd_ref, gf_ref,
                      o_ref, act_ref, *, final_norm):
    mem_out = _memory_attention(mq_ref[...].astype(F32), mk_ref[...], mv_ref[...])
    x = _project_out(tok_ref[...], mem_out, wo_ref, x_ref[...])
    _ffn_block(x, g_ref, wg_ref, wu_ref, wd_ref, gf_ref, o_ref, act_ref, final_norm)


def _ffn_block(x, g_ref, wg_ref, wu_ref, wd_ref, gf_ref, o_ref, act_ref, final_norm):
    h = _rms_scale(x, g_ref[...]).astype(BF16)
    hidden = wg_ref.shape[1]
    for c in range(hidden // FFN_CHUNK):
        cols = slice(c * FFN_CHUNK, (c + 1) * FFN_CHUNK)
        gate = jnp.dot(h, wg_ref[:, cols], preferred_element_type=F32)
        up = jnp.dot(h, wu_ref[:, cols], preferred_element_type=F32)
        act_ref[:, cols] = (gate * jax.nn.sigmoid(gate) * up).astype(BF16)
    y = x + jnp.dot(act_ref[...], wd_ref[...], preferred_element_type=F32)
    if final_norm:
        y = _rms_scale(y, gf_ref[...])
    o_ref[...] = y


def _ffn(x2d, g, w_gate, w_up, w_down, g_final, final_norm):
    n, d = x2d.shape
    hidden = w_gate.shape[1]
    tm = min(2 * ROW_TILE, n)
    row = lambda i: (i, 0)
    const = lambda i: (0, 0)
    resident = pl.Buffered(1)
    return pl.pallas_call(
        functools.partial(_ffn_kernel, final_norm=final_norm),
        grid=(n // tm,),
        in_specs=[pl.BlockSpec((tm, d), row),
                  pl.BlockSpec((1, d), const),
                  pl.BlockSpec((d, hidden), const, pipeline_mode=resident),
                  pl.BlockSpec((d, hidden), const, pipeline_mode=resident),
                  pl.BlockSpec((hidden, d), const, pipeline_mode=resident),
                  pl.BlockSpec((1, d), const)],
        out_specs=pl.BlockSpec((tm, d), row),
        out_shape=jax.ShapeDtypeStruct((n, d), F32),
        scratch_shapes=[pltpu.VMEM((tm, hidden), BF16)],
        compiler_params=_params(1),
        name="ffn",
    )(x2d, g.reshape(1, d), w_gate, w_up, w_down, g_final.reshape(1, d))


def _s5_tail(y_ssm, proj, d_skip, w_glu, b_glu, mem_kv, w_out, x2d, s, mem_len,
             g, w_gate, w_up, w_down, g_final, final_norm):
    n, d = x2d.shape
    tok_w = y_ssm.shape[1]
    hidden = w_gate.shape[1]
    tm = min(ROW_TILE, s)
    per_b = s // tm
    row = lambda i: (i, 0)
    const = lambda i: (0, 0)
    resident = pl.Buffered(1)
    return pl.pallas_call(
        functools.partial(_s5_tail_kernel, final_norm=final_norm),
        grid=(n // tm,),
        in_specs=[pl.BlockSpec((tm, tok_w), row),
                  pl.BlockSpec((tm, tok_w), row),
                  pl.BlockSpec((1, tok_w), const),
                  pl.BlockSpec((tok_w, tok_w), const, pipeline_mode=resident),
                  pl.BlockSpec((1, tok_w), const),
                  pl.BlockSpec((tm, MEM_WIDTH), lambda i: (i, tok_w // MEM_WIDTH)),
                  pl.BlockSpec((mem_len, MEM_WIDTH), lambda i: (i // per_b, 0)),
                  pl.BlockSpec((mem_len, MEM_WIDTH), lambda i: (i // per_b, 1)),
                  pl.BlockSpec((d, d), const, pipeline_mode=resident),
                  pl.BlockSpec((tm, d), row),
                  pl.BlockSpec((1, d), const),
                  pl.BlockSpec((d, hidden), const, pipeline_mode=resident),
                  pl.BlockSpec((d, hidden), const, pipeline_mode=resident),
                  pl.BlockSpec((hidden, d), const, pipeline_mode=resident),
                  pl.BlockSpec((1, d), const)],
        out_specs=pl.BlockSpec((tm, d), row),
        out_shape=jax.ShapeDtypeStruct((n, d), F32),
        scratch_shapes=[pltpu.VMEM((tm, hidden), BF16)],
        compiler_params=_params(1),
        name="s5_tail",
    )(y_ssm, proj, d_skip.reshape(1, tok_w), w_glu, b_glu.reshape(1, tok_w), proj, mem_kv, mem_kv, w_out, x2d,
      g.reshape(1, d), w_gate, w_up, w_down, g_final.reshape(1, d))


def _moba_tail(tok, proj, mem_kv, w_out, x2d, s, mem_len, g, w_gate, w_up, w_down, g_final, final_norm):
    n, d = x2d.shape
    tok_w = tok.shape[1]
    hidden = w_gate.shape[1]
    tm = min(2 * ROW_TILE, s)
    per_b = s // tm
    row = lambda i: (i, 0)
    const = lambda i: (0, 0)
    resident = pl.Buffered(1)
    mq_col = (proj.shape[1] - MEM_WIDTH) // MEM_WIDTH
    return pl.pallas_call(
        functools.partial(_moba_tail_kernel, final_norm=final_norm),
        grid=(n // tm,),
        in_specs=[pl.BlockSpec((tm, tok_w), row),
                  pl.BlockSpec((tm, MEM_WIDTH), lambda i: (i, mq_col)),
                  pl.BlockSpec((mem_len, MEM_WIDTH), lambda i: (i // per_b, 0)),
                  pl.BlockSpec((mem_len, MEM_WIDTH), lambda i: (i // per_b, 1)),
                  pl.BlockSpec((d, d), const, pipeline_mode=resident),
                  pl.BlockSpec((tm, d), row),
                  pl.BlockSpec((1, d), const),
                  pl.BlockSpec((d, hidden), const, pipeline_mode=resident),
                  pl.BlockSpec((d, hidden), const, pipeline_mode=resident),
                  pl.BlockSpec((hidden, d), const, pipeline_mode=resident),
                  pl.BlockSpec((1, d), const)],
        out_specs=pl.BlockSpec((tm, d), row),
        out_shape=jax.ShapeDtypeStruct((n, d), F32),
        scratch_shapes=[pltpu.VMEM((tm, hidden), BF16)],
        compiler_params=_params(1),
        name="moba_tail",
    )(tok, proj, mem_kv, mem_kv, w_out, x2d, g.reshape(1, d), w_gate, w_up, w_down, g_final.reshape(1, d))


def kernel(x, mem, mem_norm_g, w_mem_kv, mix_norm_g, s5_w_in, s5_lambda_re, s5_lambda_im, s5_log_dt,
           s5_b_re, s5_b_im, s5_c_re, s5_c_im, s5_d, s5_w_glu, s5_b_glu, moba_w_in, w_out, ffn_norm_g,
           w_gate, w_up, w_down, final_norm_g):
    bsz, s, d = x.shape
    mem_len = mem.shape[1]
    depth = mix_norm_g.shape[0]
    tok_w = d - MEM_WIDTH
    n_heads = tok_w // HEAD_DIM
    slopes = jnp.asarray(2.0 ** (-8.0 * (np.arange(n_heads) + 1) / n_heads), dtype=F32)

    mem_kv = _norm_matmul(mem.reshape(bsz * mem_len, d), mem_norm_g, w_mem_kv.astype(BF16), out_dtype=BF16)
    x2d = x.reshape(bsz * s, d)
    for i in range(depth):
        j = i // 2
        if i % 2 == 0:
            proj = _norm_matmul(x2d, mix_norm_g[i], s5_w_in[j].astype(BF16), row_tile=4 * ROW_TILE)
            ops = _s5_operators(s5_lambda_re[j], s5_lambda_im[j], s5_log_dt[j],
                                s5_b_re[j], s5_b_im[j], s5_c_re[j], s5_c_im[j])
            y_ssm = _s5_ssm(proj, ops, bsz, s, tok_w)
            x2d = _s5_tail(y_ssm, proj, s5_d[j], s5_w_glu[j].astype(BF16), s5_b_glu[j],
                           mem_kv, w_out[i].astype(BF16), x2d, s, mem_len, ffn_norm_g[i],
                           w_gate[i].astype(BF16), w_up[i].astype(BF16), w_down[i].astype(BF16),
                           final_norm_g, final_norm=(i == depth - 1))
            continue
        else:
            proj = _norm_matmul(x2d, mix_norm_g[i], moba_w_in[j].astype(BF16), out_dtype=BF16,
                                row_tile=2 * ROW_TILE)
            tok = _moba(proj, slopes, bsz, s, n_heads)
            x2d = _moba_tail(tok, proj, mem_kv, w_out[i].astype(BF16), x2d, s, mem_len, ffn_norm_g[i],
                             w_gate[i].astype(BF16), w_up[i].astype(BF16), w_down[i].astype(BF16),
                             final_norm_g, final_norm=(i == depth - 1))
            continue
        x2d = _ffn(x2d, ffn_norm_g[i], w_gate[i].astype(BF16), w_up[i].astype(BF16),
                   w_down[i].astype(BF16), final_norm_g, final_norm=(i == depth - 1))
    return x2d.reshape(bsz, s, d)
```
